```python
import math
import jax, jax.numpy as jnp
from jax import lax
import numpy as np

D_MODEL = 1024
BATCH = 8
SEQ = 4096
DEPTH = 2

PLE_DIM = 256
N_EVEN = (DEPTH + 1) // 2
N_ODD = DEPTH // 2
DEEPNORM_ALPHA = (2.0 * DEPTH) ** 0.25
DEEPNORM_BETA = (8.0 * DEPTH) ** -0.25
BLOCK = 128
EPS = 1e-6
A_HEAD_DIM = 64
A_HEADS = (D_MODEL // 2) // A_HEAD_DIM
A_KV_HEADS = A_HEADS // 4
A_WIDTH = A_HEADS * A_HEAD_DIM
WINDOW = 128
B_WIDTH = D_MODEL // 2
CONV_W = 3
C_HEADS = 8
C_NOPE = 64
C_ROPE = 32
C_V = 64
C_WIDTH = C_HEADS * C_V
C_Q_RANK = D_MODEL // 4
C_KV_RANK = D_MODEL // 8
ROPE_THETA = 10000.0
D_WIDTH = D_MODEL // 2
D_GROUPS = 4
D_GROUP_DIM = D_WIDTH // D_GROUPS
D_CHUNK = 128

EVEN_SIZES = (A_WIDTH, A_KV_HEADS * A_HEAD_DIM, A_KV_HEADS * A_HEAD_DIM,
              B_WIDTH, B_WIDTH, B_WIDTH, A_WIDTH + B_WIDTH)
ODD_SIZES = (C_Q_RANK, C_KV_RANK, C_ROPE, D_WIDTH, D_WIDTH, C_WIDTH + D_WIDTH)
EVEN_IN = sum(EVEN_SIZES)
ODD_IN = sum(ODD_SIZES)
MIX_OUT = D_MODEL

kernel_name = "hybrid_swa_shortconv_mla_gmlp_deepnorm"


def _split(h, sizes):
    idx = np.cumsum(np.array(sizes))[:-1].tolist()
    return jnp.split(h, idx, axis=-1)


def rms_norm(x, g):
    xf = x.astype(jnp.float32)
    y = xf * lax.rsqrt(jnp.mean(xf * xf, axis=-1, keepdims=True) + EPS)
    return (y * g.astype(jnp.float32)).astype(x.dtype)


def layer_norm(x, g, b):
    xf = x.astype(jnp.float32)
    mu = jnp.mean(xf, axis=-1, keepdims=True)
    xc = xf - mu
    var = jnp.mean(xc * xc, axis=-1, keepdims=True)
    y = xc * lax.rsqrt(var + EPS) * g.astype(jnp.float32) + b.astype(jnp.float32)
    return y.astype(x.dtype)


def rope(x, pos):
    half = x.shape[-1] // 2
    inv = ROPE_THETA ** (-jnp.arange(half, dtype=jnp.float32) / half)
    ang = pos.astype(jnp.float32)[..., None] * inv
    cos = jnp.cos(ang)[:, :, None, :]
    sin = jnp.sin(ang)[:, :, None, :]
    xf = x.astype(jnp.float32)
    x1, x2 = xf[..., :half], xf[..., half:]
    return jnp.concatenate([x1 * cos - x2 * sin, x1 * sin + x2 * cos], axis=-1).astype(x.dtype)


def _band_windows(t, nb):
    tp = jnp.pad(t, [(0, 0), (BLOCK, BLOCK)] + [(0, 0)] * (t.ndim - 2))
    tb = tp.reshape(t.shape[0], nb + 2, BLOCK, *t.shape[2:])
    return jnp.concatenate([tb[:, :-2], tb[:, 1:-1], tb[:, 2:]], axis=2)


def windowed_gqa_sink(q, k, v, pos, sink):
    Bn, S, H, dh = q.shape
    Hk = k.shape[2]
    G = H // Hk
    nb = S // BLOCK
    kw = _band_windows(k, nb)
    vw = _band_windows(v, nb)
    posk = _band_windows(pos, nb)
    posq = pos.reshape(Bn, nb, BLOCK)
    q_idx = jnp.arange(BLOCK)[:, None]
    w_idx = jnp.arange(3 * BLOCK)[None, :]
    in_band = jnp.abs(w_idx - BLOCK - q_idx) <= WINDOW
    k_glob = jnp.arange(nb)[:, None] * BLOCK + jnp.arange(3 * BLOCK)[None, :] - BLOCK
    in_seq = (k_glob >= 0) & (k_glob < S)
    valid = in_band[None] & in_seq[:, None, :]
    slopes = jnp.exp2(-8.0 * jnp.arange(1, H + 1, dtype=jnp.float32) / H).reshape(Hk, G, 1, 1)
    dist = jnp.abs(posq[..., :, None] - posk[..., None, :]).astype(jnp.float32)
    qb = q.reshape(Bn, nb, BLOCK, Hk, G, dh)
    logits = jnp.einsum('bnqkgd,bnskd->bnkgqs', qb, kw).astype(jnp.float32) * (dh ** -0.5)
    logits = logits - slopes * dist[:, :, None, None]
    logits = jnp.where(valid[None, :, None, None], logits, -jnp.inf)
    sink_l = jnp.broadcast_to(sink.astype(jnp.float32).reshape(Hk, G, 1, 1), logits.shape[:-1] + (1,))
    probs = jax.nn.softmax(jnp.concatenate([logits, sink_l], axis=-1), axis=-1)[..., :-1]
    out = jnp.einsum('bnkgqs,bnskd->bnqkgd', probs.astype(v.dtype), vw)
    return out.reshape(Bn, S, H * dh)


def short_conv_mixer(bg, cg, xin, conv_w):
    z = cg * xin
    C = z.shape[-1]
    y = lax.conv_general_dilated(z, conv_w[:, None, :].astype(z.dtype), window_strides=(1,),
                                 padding=((CONV_W // 2, CONV_W // 2),),
                                 dimension_numbers=('NWC', 'WIO', 'NWC'), feature_group_count=C)
    return bg * y


def mla_attention(q_nope, q_rope, k_nope, k_rope, v):
    Bn, S, H, _ = q_nope.shape
    nb = S // BLOCK
    scale = (C_NOPE + C_ROPE) ** -0.5

    def to_blocks(t):
        return jnp.moveaxis(t.reshape(Bn, nb, BLOCK, *t.shape[2:]), 1, 0)

    def attend(blk):
        qn, qr = blk
        s = (jnp.einsum('bqhd,bshd->bhqs', qn, k_nope)
             + jnp.einsum('bqhr,bsr->bhqs', qr, k_rope)).astype(jnp.float32) * scale
        pr = jax.nn.softmax(s, axis=-1).astype(v.dtype)
        return jnp.einsum('bhqs,bshd->bqhd', pr, v)

    out = lax.map(attend, (to_blocks(q_nope), to_blocks(q_rope)))
    return jnp.moveaxis(out, 0, 1).reshape(Bn, S, H * v.shape[-1])


def chunked_spatial_gate(u, v, v_ln_g, v_ln_b, w_s, b_s):
    Bn, S, _ = v.shape
    nc = S // D_CHUNK
    vn = layer_norm(v, v_ln_g, v_ln_b)
    vc = vn.reshape(Bn, nc, D_CHUNK, D_GROUPS, D_GROUP_DIM)
    mixed = jnp.einsum('gts,bcsgd->bctgd', w_s.astype(v.dtype), vc) + b_s.T.astype(v.dtype)[None, None, :, :, None]
    return u * mixed.reshape(Bn, S, D_WIDTH)


def even_mixer(x, pos, w_in, conv_w, sink, a_norm, b_norm, w_out):
    Bn, S, _ = x.shape
    q, k, v, bg, cg, xin, z = _split(x @ w_in, EVEN_SIZES)
    q = q.reshape(Bn, S, A_HEADS, A_HEAD_DIM)
    k = k.reshape(Bn, S, A_KV_HEADS, A_HEAD_DIM)
    v = v.reshape(Bn, S, A_KV_HEADS, A_HEAD_DIM)
    ya = windowed_gqa_sink(q, k, v, pos, sink)
    yb = short_conv_mixer(bg, cg, xin, conv_w)
    y = jnp.concatenate([rms_norm(ya, a_norm), rms_norm(yb, b_norm)], axis=-1) * jax.nn.silu(z)
    return y @ w_out


def odd_mixer(x, pos, w_in, q_norm, w_uq, kv_norm, w_ukv, v_ln_g, v_ln_b, w_s, b_s, c_norm, d_norm, w_out):
    Bn, S, _ = x.shape
    cq, ckv, kr, du, dv, z = _split(x @ w_in, ODD_SIZES)
    q = (rms_norm(cq, q_norm) @ w_uq).reshape(Bn, S, C_HEADS, C_NOPE + C_ROPE)
    q_nope = q[..., :C_NOPE]
    q_rope = rope(q[..., C_NOPE:], pos)
    kv = (rms_norm(ckv, kv_norm) @ w_ukv).reshape(Bn, S, C_HEADS, C_NOPE + C_V)
    k_nope, v = kv[..., :C_NOPE], kv[..., C_NOPE:]
    k_rope = rope(kr[:, :, None, :], pos)[:, :, 0]
    yc = mla_attention(q_nope, q_rope, k_nope, k_rope, v)
    yd = chunked_spatial_gate(jax.nn.gelu(du), jax.nn.gelu(dv), v_ln_g, v_ln_b, w_s, b_s)
    y = jnp.concatenate([rms_norm(yc, c_norm), rms_norm(yd, d_norm)], axis=-1) * jax.nn.silu(z)
    return y @ w_out


def _normal(k, shape, std):
    return std * jax.random.normal(k, shape, jnp.float32)


def setup_inputs(seed: int = 0) -> dict:
    key = jax.random.key(seed)
    ks = iter(jax.random.split(key, 32))
    E, O, L = N_EVEN, N_ODD, DEPTH
    d = {}
    d['x'] = _normal(next(ks), (BATCH, SEQ, D_MODEL), 1.0)
    d['p'] = _normal(next(ks), (DEPTH, BATCH, SEQ, PLE_DIM), 1.0)
    d['positions'] = jnp.broadcast_to(jnp.arange(SEQ, dtype=jnp.int32), (BATCH, SEQ))
    d['ev_w_in'] = _normal(next(ks), (E, D_MODEL, EVEN_IN), D_MODEL ** -0.5)
    d['ev_conv_w'] = _normal(next(ks), (E, CONV_W, B_WIDTH), CONV_W ** -0.5)
    d['ev_sink'] = _normal(next(ks), (E, A_HEADS), 0.5)
    d['ev_a_norm'] = 1.0 + _normal(next(ks), (E, A_WIDTH), 0.02)
    d['ev_b_norm'] = 1.0 + _normal(next(ks), (E, B_WIDTH), 0.02)
    d['ev_w_out'] = _normal(next(ks), (E, MIX_OUT, D_MODEL), DEEPNORM_BETA * MIX_OUT ** -0.5)
    d['od_w_in'] = _normal(next(ks), (O, D_MODEL, ODD_IN), D_MODEL ** -0.5)
    d['od_q_norm'] = 1.0 + _normal(next(ks), (O, C_Q_RANK), 0.02)
    d['od_w_uq'] = _normal(next(ks), (O, C_Q_RANK, C_HEADS * (C_NOPE + C_ROPE)), C_Q_RANK ** -0.5)
    d['od_kv_norm'] = 1.0 + _normal(next(ks), (O, C_KV_RANK), 0.02)
    d['od_w_ukv'] = _normal(next(ks), (O, C_KV_RANK, C_HEADS * (C_NOPE + C_V)), C_KV_RANK ** -0.5)
    d['od_v_ln_g'] = 1.0 + _normal(next(ks), (O, D_WIDTH), 0.02)
    d['od_v_ln_b'] = _normal(next(ks), (O, D_WIDTH), 0.02)
    d['od_w_s'] = _normal(next(ks), (O, D_GROUPS, D_CHUNK, D_CHUNK), D_CHUNK ** -0.5)
    d['od_b_s'] = 1.0 + _normal(next(ks), (O, D_GROUPS, D_CHUNK), 0.1)
    d['od_c_norm'] = 1.0 + _normal(next(ks), (O, C_WIDTH), 0.02)
    d['od_d_norm'] = 1.0 + _normal(next(ks), (O, D_WIDTH), 0.02)
    d['od_w_out'] = _normal(next(ks), (O, MIX_OUT, D_MODEL), DEEPNORM_BETA * MIX_OUT ** -0.5)
    d['post_ln_g'] = 1.0 + _normal(next(ks), (L, D_MODEL), 0.02)
    d['post_ln_b'] = _normal(next(ks), (L, D_MODEL), 0.02)
    d['ple_proj'] = _normal(next(ks), (L, PLE_DIM, D_MODEL), PLE_DIM ** -0.5)
    d['ple_gate'] = _normal(next(ks), (L, D_MODEL, D_MODEL), D_MODEL ** -0.5)
    return d


def reference(x, p, positions, ev_w_in, ev_conv_w, ev_sink, ev_a_norm, ev_b_norm, ev_w_out,
              od_w_in, od_q_norm, od_w_uq, od_kv_norm, od_w_ukv, od_v_ln_g, od_v_ln_b, od_w_s, od_b_s,
              od_c_norm, od_d_norm, od_w_out, post_ln_g, post_ln_b, ple_proj, ple_gate):
    for i in range(DEPTH):
        j = i // 2
        if i % 2 == 0:
            y = even_mixer(x, positions, ev_w_in[j], ev_conv_w[j], ev_sink[j],
                           ev_a_norm[j], ev_b_norm[j], ev_w_out[j])
        else:
            y = odd_mixer(x, positions, od_w_in[j], od_q_norm[j], od_w_uq[j], od_kv_norm[j], od_w_ukv[j],
                          od_v_ln_g[j], od_v_ln_b[j], od_w_s[j], od_b_s[j], od_c_norm[j], od_d_norm[j],
                          od_w_out[j])
        h = layer_norm(DEEPNORM_ALPHA * x + y, post_ln_g[i], post_ln_b[i])
        x = h + jax.nn.sigmoid(h @ ple_gate[i]) * (p[i] @ ple_proj[i])
    return x
```

```python
import functools
import math

import jax
import jax.numpy as jnp
from jax import lax
from jax.experimental import pallas as pl
from jax.experimental.pallas import tpu as pltpu

F32 = jnp.float32
BF16 = jnp.bfloat16

D_MODEL = 1024
DEPTH = 2
PLE_DIM = 256
DEEPNORM_ALPHA = (2.0 * DEPTH) ** 0.25
BLOCK = 128
EPS = 1e-6
A_HEAD_DIM = 64
A_HEADS = 8
A_KV_HEADS = 2
A_GROUP = A_HEADS // A_KV_HEADS
A_WIDTH = A_HEADS * A_HEAD_DIM
WINDOW = 128
B_WIDTH = 512
C_HEADS = 8
C_NOPE = 64
C_ROPE = 32
C_V = 64
C_WIDTH = C_HEADS * C_V
C_Q_RANK = 256
C_KV_RANK = 128
ROPE_THETA = 10000.0
D_WIDTH = 512
D_GROUPS = 4
D_GROUP_DIM = D_WIDTH // D_GROUPS
D_CHUNK = 128

LANES = 128
SLOT = LANES
VMEM_LIMIT = 56 * 1024 * 1024

_EV_OFF = (0, 512, 640, 768, 1280, 1792, 2304, 3328)
_OD_OFF = (0, 384, 640, 1152, 1664, 2688)

A_SLOPES = tuple(2.0 ** (-8.0 * (h + 1) / A_HEADS) for h in range(A_HEADS))
MLA_SCALE_LOG2E = (C_NOPE + C_ROPE) ** -0.5 * math.log2(math.e)


def _dot(a, b):
    return jnp.dot(a, b, preferred_element_type=F32)


def _dot_nt(a, b):
    return lax.dot_general(a, b, (((1,), (1,)), ((), ())), preferred_element_type=F32)


def _rms(x, g):
    return x * lax.rsqrt(jnp.mean(x * x, axis=-1, keepdims=True) + EPS) * g


def _layer_norm(x, g, b):
    mu = jnp.mean(x, axis=-1, keepdims=True)
    xc = x - mu
    var = jnp.mean(xc * xc, axis=-1, keepdims=True)
    return xc * lax.rsqrt(var + EPS) * g + b


def _silu(z):
    return z * jax.nn.sigmoid(z)


def _gelu_tanh(x):
    c = math.sqrt(2.0 / math.pi)
    return 0.5 * x * (1.0 + jnp.tanh(c * (x + 0.044715 * (x * x * x))))


def _post_block(y_bf16_dot, x, p, g_ref, b_ref, gate_ref, proj_ref):
    h = _layer_norm(DEEPNORM_ALPHA * x + y_bf16_dot, g_ref[...], b_ref[...])
    gate = jax.nn.sigmoid(_dot(h.astype(BF16), gate_ref[...]))
    return h + gate * _dot(p.astype(BF16), proj_ref[...])


def _even_proj_kernel(x_ref, w_ref, q_ref, k_ref, v_ref, bg_ref, cz_ref, sz_ref):
    xb = x_ref[...].astype(BF16)

    def seg(i):
        return _dot(xb, w_ref[:, _EV_OFF[i]:_EV_OFF[i + 1]])

    q_ref[...] = (seg(0) * (A_HEAD_DIM ** -0.5)).astype(BF16)
    k_ref[...] = seg(1).astype(BF16)
    v_ref[...] = seg(2).astype(BF16)
    bg_ref[...] = seg(3)
    cz_ref[...] = seg(4) * seg(5)
    sz_ref[...] = _silu(seg(6))


def _even_mix_kernel(seq_len, tile,
                     q_ref, kc_ref, kp_ref, kn_ref, vc_ref, vp_ref, vn_ref,
                     posc_ref, prc_ref, prp_ref, prn_ref,
                     cz_ref, czp_ref, czn_ref, bg_ref, sz_ref, x_ref, p_ref,
                     convw_ref, sink_ref, anorm_ref, bnorm_ref, wout_ref,
                     g_ref, b_ref, gate_ref, proj_ref,
                     o_ref, kbuf, vbuf, posbuf, ybuf):
    i = pl.program_id(1)
    n_tiles = pl.num_programs(1)
    sub = tile // BLOCK

    kbuf[0:BLOCK] = kp_ref[...]
    kbuf[BLOCK:BLOCK + tile] = kc_ref[...]
    kbuf[BLOCK + tile:] = kn_ref[...]
    vbuf[0:BLOCK] = vp_ref[...]
    vbuf[BLOCK:BLOCK + tile] = vc_ref[...]
    vbuf[BLOCK + tile:] = vn_ref[...]
    posbuf[:, 0:BLOCK] = prp_ref[...]
    posbuf[:, BLOCK:BLOCK + tile] = prc_ref[...]
    posbuf[:, BLOCK + tile:] = prn_ref[...]

    q_idx = lax.broadcasted_iota(jnp.int32, (BLOCK, 3 * BLOCK), 0)
    w_idx = lax.broadcasted_iota(jnp.int32, (BLOCK, 3 * BLOCK), 1)
    in_band = jnp.abs(w_idx - BLOCK - q_idx) <= WINDOW

    for j in range(sub):
        rows = slice(j * BLOCK, (j + 1) * BLOCK)
        win = slice(j * BLOCK, j * BLOCK + 3 * BLOCK)
        k_glob = (i * sub + j) * BLOCK + w_idx - BLOCK
        valid = in_band & (k_glob >= 0) & (k_glob < seq_len)
        dist = jnp.abs(posc_ref[rows, :] - posbuf[:, win]).astype(F32)
        dist = jnp.where(valid, dist, jnp.inf)
        for h in range(A_HEADS):
            g = h // A_GROUP
            qh = q_ref[rows, h * A_HEAD_DIM:(h + 1) * A_HEAD_DIM]
            kh = kbuf[win, g * A_HEAD_DIM:(g + 1) * A_HEAD_DIM]
            vh = vbuf[win, g * A_HEAD_DIM:(g + 1) * A_HEAD_DIM]
            logits = _dot_nt(qh, kh) - A_SLOPES[h] * dist
            sink = sink_ref[h]
            m = jnp.maximum(jnp.max(logits, axis=-1, keepdims=True), sink)
            e = jnp.exp(logits - m)
            denom = jnp.sum(e, axis=-1, keepdims=True) + jnp.exp(sink - m)
            out = _dot(e.astype(BF16), vh) / denom
            ybuf[rows, h * A_HEAD_DIM:(h + 1) * A_HEAD_DIM] = out

    cz = cz_ref[...]
    row = lax.broadcasted_iota(jnp.int32, cz.shape, 0)
    prev_row = jnp.where(i > 0, czp_ref[7:8, :], 0.0)
    next_row = jnp.where(i < n_tiles - 1, czn_ref[0:1, :], 0.0)
    z_prev = jnp.where(row == 0, prev_row, pltpu.roll(cz, 1, axis=0))
    z_next = jnp.where(row == tile - 1, next_row, pltpu.roll(cz, tile - 1, axis=0))
    conv = (convw_ref[0:1, :] * z_prev + convw_ref[1:2, :] * cz
            + convw_ref[2:3, :] * z_next)
    yb = bg_ref[...] * conv

    ya_n = _rms(ybuf[...], anorm_ref[...]) * sz_ref[:, 0:A_WIDTH]
    yb_n = _rms(yb, bnorm_ref[...]) * sz_ref[:, A_WIDTH:]
    mix = (_dot(ya_n.astype(BF16), wout_ref[0:A_WIDTH, :])
           + _dot(yb_n.astype(BF16), wout_ref[A_WIDTH:, :]))
    o_ref[...] = _post_block(mix, x_ref[...], p_ref[...], g_ref, b_ref, gate_ref, proj_ref)


def _odd_proj_kernel(x_ref, pos_ref, w_ref, inv_ref, qn_ref, wq_ref, wqs_ref,
                     kvn_ref, wkn_ref, wke_ref, wv_ref, vones_ref,
                     lng_ref, lnb_ref, ws_ref, bs_ref, dnorm_ref,
                     q_ref, k_ref, v_ref, yd_ref, szc_ref):
    tile = x_ref.shape[0]
    xb = x_ref[...].astype(BF16)

    def seg(i):
        return _dot(xb, w_ref[:, _OD_OFF[i]:_OD_OFF[i + 1]])

    ang = pos_ref[...].astype(F32) * inv_ref[...]
    cos_t = jnp.cos(ang)
    sin_t = jnp.sin(ang)

    a = seg(0)
    cqn = _rms(a[:, 0:C_Q_RANK], qn_ref[...]).astype(BF16)
    ckvn = _rms(a[:, C_Q_RANK:], kvn_ref[...]).astype(BF16)

    qa = _dot(cqn, wq_ref[...])
    qs = _dot(cqn, wqs_ref[...])
    for h in range(C_HEADS):
        sl = slice(h * SLOT, (h + 1) * SLOT)
        q_ref[:, sl] = ((qa[:, sl] * cos_t + qs[:, sl] * sin_t) * MLA_SCALE_LOG2E).astype(BF16)

    kr = seg(1)
    kr_rot = (kr[:, 0:LANES] * cos_t + kr[:, LANES:] * sin_t).astype(BF16)
    k_ref[...] = (_dot(ckvn, wkn_ref[...]) + _dot(kr_rot, wke_ref[...])).astype(BF16)
    v_ref[...] = (_dot(ckvn, wv_ref[...]) + vones_ref[...]).astype(BF16)

    gu = _gelu_tanh(seg(2))
    vn = _layer_norm(_gelu_tanh(seg(3)), lng_ref[...], lnb_ref[...]).astype(BF16)
    chunks = []
    for c in range(tile // D_CHUNK):
        rows = slice(c * D_CHUNK, (c + 1) * D_CHUNK)
        groups = [_dot(ws_ref[g], vn[rows, g * D_GROUP_DIM:(g + 1) * D_GROUP_DIM])
                  for g in range(D_GROUPS)]
        chunks.append(jnp.concatenate(groups, axis=1) + bs_ref[...])
    yd = gu * jnp.concatenate(chunks, axis=0)
    sz = _silu(seg(4))
    yd_ref[...] = (_rms(yd, dnorm_ref[...]) * sz[:, C_WIDTH:]).astype(BF16)
    szc_ref[...] = sz[:, 0:C_WIDTH]


def _mla_kernel(seq_len, kv_tile, q_ref, k_ref, v_ref, o_ref, acc_ref, m_ref):
    outs = []
    for hh in range(2):
        sl = slice(hh * SLOT, (hh + 1) * SLOT)
        q = q_ref[:, sl]
        m_ref[...] = jnp.full(m_ref.shape, -jnp.inf, F32)
        acc_ref[...] = jnp.zeros(acc_ref.shape, F32)

        def body(c, carry):
            start = pl.multiple_of(c * kv_tile, kv_tile)
            s = _dot_nt(q, k_ref[pl.ds(start, kv_tile), sl])
            m_old = m_ref[...]
            m_new = jnp.maximum(m_old, jnp.max(s, axis=-1, keepdims=True))
            p = jnp.exp2(s - m_new)
            acc_ref[...] = (jnp.exp2(m_old - m_new) * acc_ref[...]
                            + _dot(p.astype(BF16), v_ref[pl.ds(start, kv_tile), sl]))
            m_ref[...] = m_new
            return carry

        lax.fori_loop(0, seq_len // kv_tile, body, 0)
        acc = acc_ref[...]
        outs.append(acc / acc[:, C_V:C_V + 1])
    lane = lax.broadcasted_iota(jnp.int32, outs[0].shape, 1)
    o_ref[...] = jnp.where(lane < C_V, outs[0], pltpu.roll(outs[1], C_V, axis=1))


def _odd_out_kernel(yc_ref, szc_ref, yd_ref, x_ref, p_ref, cnorm_ref, wout_ref,
                    g_ref, b_ref, gate_ref, proj_ref, o_ref):
    yc_n = _rms(yc_ref[...], cnorm_ref[...]) * szc_ref[...]
    mix = (_dot(yc_n.astype(BF16), wout_ref[0:C_WIDTH, :])
           + _dot(yd_ref[...], wout_ref[C_WIDTH:, :]))
    o_ref[...] = _post_block(mix, x_ref[...], p_ref[...], g_ref, b_ref, gate_ref, proj_ref)


def _full(shape):
    return pl.BlockSpec(shape, lambda *_: (0,) * len(shape))


def _params(sem):
    return pltpu.CompilerParams(dimension_semantics=sem, vmem_limit_bytes=VMEM_LIMIT)


def _even_layer(x, p, layer, positions, w_in, conv_w, sink, a_norm, b_norm, w_out,
                ln_g, ln_b, ple_proj, ple_gate):
    bsz, seq, _ = x.shape
    n_tok = bsz * seq
    tm = 512
    w_in_b = w_in.astype(BF16)
    row_spec = lambda w: pl.BlockSpec((tm, w), lambda t: (t, 0))
    q, k, v, bg, cz, sz = pl.pallas_call(
        _even_proj_kernel,
        grid=(n_tok // tm,),
        in_specs=[row_spec(D_MODEL), _full(w_in_b.shape)],
        out_specs=[row_spec(512), row_spec(128), row_spec(128), row_spec(512),
                   row_spec(512), row_spec(1024)],
        out_shape=[jax.ShapeDtypeStruct((n_tok, 512), BF16),
                   jax.ShapeDtypeStruct((n_tok, 128), BF16),
                   jax.ShapeDtypeStruct((n_tok, 128), BF16),
                   jax.ShapeDtypeStruct((n_tok, 512), F32),
                   jax.ShapeDtypeStruct((n_tok, 512), F32),
                   jax.ShapeDtypeStruct((n_tok, 1024), F32)],
        compiler_params=_params(("parallel",)),
        name="even_proj",
    )(x.reshape(n_tok, D_MODEL), w_in_b)

    tile = 512
    sub = tile // BLOCK
    nb = seq // BLOCK
    r8 = tile // 8
    q = q.reshape(bsz, seq, 512)
    k = k.reshape(bsz, seq, 128)
    v = v.reshape(bsz, seq, 128)
    bg = bg.reshape(bsz, seq, 512)
    cz = cz.reshape(bsz, seq, 512)
    sz = sz.reshape(bsz, seq, 1024)
    pos_col = positions.reshape(bsz, seq, 1)
    pos_row = positions.reshape(bsz, 1, seq)

    cur = lambda w: pl.BlockSpec((None, tile, w), lambda b, i: (b, i, 0))
    prev_blk = pl.BlockSpec((None, BLOCK, 128), lambda b, i: (b, jnp.maximum(i * sub - 1, 0), 0))
    next_blk = pl.BlockSpec((None, BLOCK, 128), lambda b, i: (b, jnp.minimum((i + 1) * sub, nb - 1), 0))
    in_specs = [
        cur(512),
        cur(128), prev_blk, next_blk,
        cur(128), prev_blk, next_blk,
        pl.BlockSpec((None, tile, 1), lambda b, i: (b, i, 0)),
        pl.BlockSpec((None, 1, tile), lambda b, i: (b, 0, i)),
        pl.BlockSpec((None, 1, BLOCK), lambda b, i: (b, 0, jnp.maximum(i * sub - 1, 0))),
        pl.BlockSpec((None, 1, BLOCK), lambda b, i: (b, 0, jnp.minimum((i + 1) * sub, nb - 1))),
        cur(512),
        pl.BlockSpec((None, 8, 512), lambda b, i: (b, jnp.maximum(i * r8 - 1, 0), 0)),
        pl.BlockSpec((None, 8, 512), lambda b, i: (b, jnp.minimum((i + 1) * r8, seq // 8 - 1), 0)),
        cur(512), cur(1024), cur(D_MODEL),
        pl.BlockSpec((None, None, tile, PLE_DIM), lambda b, i: (layer, b, i, 0)),
        _full((3, 512)),
        pl.BlockSpec(memory_space=pltpu.SMEM),
        _full((1, 512)), _full((1, 512)), _full((D_MODEL, D_MODEL)),
        _full((1, D_MODEL)), _full((1, D_MODEL)), _full((D_MODEL, D_MODEL)),
        _full((PLE_DIM, D_MODEL)),
    ]
    return pl.pallas_call(
        functools.partial(_even_mix_kernel, seq, tile),
        grid=(bsz, seq // tile),
        in_specs=in_specs,
        out_specs=cur(D_MODEL),
        out_shape=jax.ShapeDtypeStruct((bsz, seq, D_MODEL), F32),
        scratch_shapes=[pltpu.VMEM((tile + 2 * BLOCK, 128), BF16),
                        pltpu.VMEM((tile + 2 * BLOCK, 128), BF16),
                        pltpu.VMEM((1, tile + 2 * BLOCK), jnp.int32),
                        pltpu.VMEM((tile, A_WIDTH), F32)],
        compiler_params=_params(("parallel", "parallel")),
        name="even_mix",
    )(q, k, k, k, v, v, v, pos_col, pos_row, pos_row, pos_row, cz, cz, cz, bg, sz, x, p,
      conv_w, sink, a_norm.reshape(1, -1), b_norm.reshape(1, -1), w_out.astype(BF16),
      ln_g.reshape(1, -1), ln_b.reshape(1, -1), ple_gate.astype(BF16), ple_proj.astype(BF16))


def _swap_halves(r):
    half = r.shape[-1] // 2
    return jnp.concatenate([-r[..., half:], r[..., :half]], axis=-1)


def _odd_weights(w_in, w_uq, w_ukv, b_s):
    d = w_in.shape[0]
    o_kr = C_Q_RANK + C_KV_RANK
    o_du = o_kr + C_ROPE
    w_kr = w_in[:, o_kr:o_du]
    kr_pair = jnp.zeros((d, 2 * LANES), F32)
    kr_pair = kr_pair.at[:, C_NOPE:C_NOPE + C_ROPE].set(w_kr)
    kr_pair = kr_pair.at[:, LANES + C_NOPE:LANES + C_NOPE + C_ROPE].set(_swap_halves(w_kr))
    w1 = jnp.concatenate([w_in[:, :o_kr], kr_pair, w_in[:, o_du:]], axis=1).astype(BF16)

    uq = w_uq.reshape(C_Q_RANK, C_HEADS, C_NOPE + C_ROPE)
    pad = SLOT - C_NOPE - C_ROPE
    wq = jnp.pad(uq, ((0, 0), (0, 0), (0, pad)))
    wqs = jnp.pad(_swap_halves(uq[..., C_NOPE:]), ((0, 0), (0, 0), (C_NOPE, pad)))
    ukv = w_ukv.reshape(C_KV_RANK, C_HEADS, C_NOPE + C_V)
    wkn = jnp.pad(ukv[..., :C_NOPE], ((0, 0), (0, 0), (0, SLOT - C_NOPE)))
    wv = jnp.pad(ukv[..., C_NOPE:], ((0, 0), (0, 0), (0, SLOT - C_V)))
    flat = lambda w: w.reshape(w.shape[0], C_HEADS * SLOT).astype(BF16)

    lane = jnp.arange(LANES)
    rope_lane = (lane >= C_NOPE) & (lane < C_NOPE + C_ROPE)
    wke = jnp.tile(jnp.where(rope_lane[:, None] & (lane[:, None] == lane[None, :]), 1.0, 0.0),
                   (1, C_HEADS)).astype(BF16)
    vones = jnp.tile(jnp.where(lane == C_V, 1.0, 0.0), C_HEADS).reshape(1, -1).astype(F32)
    half = C_ROPE // 2
    inv = ROPE_THETA ** (-jnp.arange(half, dtype=F32) / half)
    inv_lane = jnp.where(rope_lane, inv[(lane - C_NOPE) % half], 0.0).reshape(1, LANES).astype(F32)
    bs_full = jnp.repeat(b_s.T, D_GROUP_DIM, axis=1)
    return w1, flat(wq), flat(wqs), flat(wkn), wke, flat(wv), vones, inv_lane, bs_full


def _odd_layer(x, p, layer, positions, w_in, q_norm, w_uq, kv_norm, w_ukv, v_ln_g, v_ln_b,
               w_s, b_s, c_norm, d_norm, w_out, ln_g, ln_b, ple_proj, ple_gate):
    bsz, seq, _ = x.shape
    n_tok = bsz * seq
    w1, wq, wqs, wkn, wke, wv, vones, inv_lane, bs_full = _odd_weights(w_in, w_uq, w_ukv, b_s)
    tm = 512
    row_spec = lambda w: pl.BlockSpec((tm, w), lambda t: (t, 0))
    wide = C_HEADS * SLOT
    q, k, v, yd, szc = pl.pallas_call(
        _odd_proj_kernel,
        grid=(n_tok // tm,),
        in_specs=[row_spec(D_MODEL), row_spec(1), _full(w1.shape), _full((1, LANES)),
                  _full((1, C_Q_RANK)), _full(wq.shape), _full(wqs.shape),
                  _full((1, C_KV_RANK)), _full(wkn.shape), _full(wke.shape), _full(wv.shape),
                  _full((1, wide)), _full((1, D_WIDTH)), _full((1, D_WIDTH)),
                  _full((D_GROUPS, D_CHUNK, D_CHUNK)), _full((D_CHUNK, D_WIDTH)),
                  _full((1, D_WIDTH))],
        out_specs=[row_spec(wide), row_spec(wide), row_spec(wide), row_spec(D_WIDTH),
                   row_spec(C_WIDTH)],
        out_shape=[jax.ShapeDtypeStruct((n_tok, wide), BF16),
                   jax.ShapeDtypeStruct((n_tok, wide), BF16),
                   jax.ShapeDtypeStruct((n_tok, wide), BF16),
                   jax.ShapeDtypeStruct((n_tok, D_WIDTH), BF16),
                   jax.ShapeDtypeStruct((n_tok, C_WIDTH), F32)],
        compiler_params=_params(("parallel",)),
        name="odd_proj",
    )(x.reshape(n_tok, D_MODEL), positions.reshape(n_tok, 1), w1, inv_lane,
      q_norm.reshape(1, -1), wq, wqs, kv_norm.reshape(1, -1), wkn, wke, wv, vones,
      v_ln_g.reshape(1, -1), v_ln_b.reshape(1, -1), w_s.astype(BF16), bs_full,
      d_norm.reshape(1, -1))

    tq = 512
    kv_tile = 512
    q = q.reshape(bsz, seq, wide)
    k = k.reshape(bsz, seq, wide)
    v = v.reshape(bsz, seq, wide)
    yc = pl.pallas_call(
        functools.partial(_mla_kernel, seq, kv_tile),
        grid=(bsz, C_HEADS // 2, seq // tq),
        in_specs=[pl.BlockSpec((None, tq, 2 * SLOT), lambda b, h, i: (b, i, h)),
                  pl.BlockSpec((None, seq, 2 * SLOT), lambda b, h, i: (b, 0, h)),
                  pl.BlockSpec((None, seq, 2 * SLOT), lambda b, h, i: (b, 0, h))],
        out_specs=pl.BlockSpec((None, tq, 2 * C_V), lambda b, h, i: (b, i, h)),
        out_shape=jax.ShapeDtypeStruct((bsz, seq, C_WIDTH), F32),
        scratch_shapes=[pltpu.VMEM((tq, SLOT), F32), pltpu.VMEM((tq, 1), F32)],
        compiler_params=_params(("parallel", "parallel", "parallel")),
        name="mla_attn",
    )(q, k, v)

    out = pl.pallas_call(
        _odd_out_kernel,
        grid=(n_tok // tm,),
        in_specs=[row_spec(C_WIDTH), row_spec(C_WIDTH), row_spec(D_WIDTH), row_spec(D_MODEL),
                  pl.BlockSpec((None, tm, PLE_DIM), lambda t: (layer, t, 0)),
                  _full((1, C_WIDTH)), _full((D_MODEL, D_MODEL)), _full((1, D_MODEL)),
                  _full((1, D_MODEL)), _full((D_MODEL, D_MODEL)), _full((PLE_DIM, D_MODEL))],
        out_specs=row_spec(D_MODEL),
        out_shape=jax.ShapeDtypeStruct((n_tok, D_MODEL), F32),
        compiler_params=_params(("parallel",)),
        name="odd_out",
    )(yc.reshape(n_tok, C_WIDTH), szc, yd, x.reshape(n_tok, D_MODEL),
      p.reshape(DEPTH, n_tok, PLE_DIM), c_norm.reshape(1, -1), w_out.astype(BF16),
      ln_g.reshape(1, -1), ln_b.reshape(1, -1), ple_gate.astype(BF16), ple_proj.astype(BF16))
    return out.reshape(bsz, seq, D_MODEL)


def kernel(x, p, positions, ev_w_in, ev_conv_w, ev_sink, ev_a_norm, ev_b_norm, ev_w_out, od_w_in, od_q_norm, od_w_uq, od_kv_norm, od_w_ukv, od_v_ln_g, od_v_ln_b, od_w_s, od_b_s, od_c_norm, od_d_norm, od_w_out, post_ln_g, post_ln_b, ple_proj, ple_gate):
    for i in range(DEPTH):
        j = i // 2
        if i % 2 == 0:
            x = _even_layer(x, p, i, positions, ev_w_in[j], ev_conv_w[j], ev_sink[j],
                            ev_a_norm[j], ev_b_norm[j], ev_w_out[j],
                            post_ln_g[i], post_ln_b[i], ple_proj[i], ple_gate[i])
        else:
            x = _odd_layer(x, p, i, positions, od_w_in[j], od_q_norm[j], od_w_uq[j],
                           od_kv_norm[j], od_w_ukv[j], od_v_ln_g[j], od_v_ln_b[j],
                           od_w_s[j], od_b_s[j], od_c_norm[j], od_d_norm[j], od_w_out[j],
                           post_ln_g[i], post_ln_b[i], ple_proj[i], ple_gate[i])
    return x
```

```python
import functools
import math

import jax
import jax.numpy as jnp
from jax import lax
from jax.experimental import pallas as pl
from jax.experimental.pallas import tpu as pltpu

F32 = jnp.float32
BF16 = jnp.bfloat16

D_MODEL = 1024
DEPTH = 2
PLE_DIM = 256
DEEPNORM_ALPHA = (2.0 * DEPTH) ** 0.25
BLOCK = 128
EPS = 1e-6
A_HEAD_DIM = 64
A_HEADS = 8
A_KV_HEADS = 2
A_GROUP = A_HEADS // A_KV_HEADS
A_WIDTH = A_HEADS * A_HEAD_DIM
WINDOW = 128
B_WIDTH = 512
C_HEADS = 8
C_NOPE = 64
C_ROPE = 32
C_V = 64
C_WIDTH = C_HEADS * C_V
C_Q_RANK = 256
C_KV_RANK = 128
ROPE_THETA = 10000.0
D_WIDTH = 512
D_GROUPS = 4
D_GROUP_DIM = D_WIDTH // D_GROUPS
D_CHUNK = 128

LANES = 128
SLOT = LANES
VMEM_LIMIT = 56 * 1024 * 1024

_EV_OFF = (0, 512, 640, 768, 1280, 1792, 2304, 3328)
_OD_OFF = (0, 384, 640, 1152, 1664, 2688)

A_SLOPES = tuple(2.0 ** (-8.0 * (h + 1) / A_HEADS) for h in range(A_HEADS))
MLA_SCALE_LOG2E = (C_NOPE + C_ROPE) ** -0.5 * math.log2(math.e)


def _dot(a, b):
    return jnp.dot(a, b, preferred_element_type=F32)


def _dot_nt(a, b):
    return lax.dot_general(a, b, (((1,), (1,)), ((), ())), preferred_element_type=F32)


def _rms(x, g):
    return x * lax.rsqrt(jnp.mean(x * x, axis=-1, keepdims=True) + EPS) * g


def _layer_norm(x, g, b):
    mu = jnp.mean(x, axis=-1, keepdims=True)
    xc = x - mu
    var = jnp.mean(xc * xc, axis=-1, keepdims=True)
    return xc * lax.rsqrt(var + EPS) * g + b


def _silu(z):
    return z * jax.nn.sigmoid(z)


def _gelu_tanh(x):
    c = math.sqrt(2.0 / math.pi)
    return 0.5 * x * (1.0 + jnp.tanh(c * (x + 0.044715 * (x * x * x))))


def _post_block(y_bf16_dot, x, p, g_ref, b_ref, gate_ref, proj_ref):
    h = _layer_norm(DEEPNORM_ALPHA * x + y_bf16_dot, g_ref[...], b_ref[...])
    gate = jax.nn.sigmoid(_dot(h.astype(BF16), gate_ref[...]))
    return h + gate * _dot(p.astype(BF16), proj_ref[...])


def _even_proj_kernel(x_ref, w_ref, q_ref, k_ref, v_ref, bg_ref, cz_ref, sz_ref):
    xb = x_ref[...].astype(BF16)

    def seg(i):
        return _dot(xb, w_ref[:, _EV_OFF[i]:_EV_OFF[i + 1]])

    q_ref[...] = (seg(0) * (A_HEAD_DIM ** -0.5)).astype(BF16)
    k_ref[...] = seg(1).astype(BF16)
    v_ref[...] = seg(2).astype(BF16)
    bg_ref[...] = seg(3)
    cz_ref[...] = seg(4) * seg(5)
    sz_ref[...] = _silu(seg(6))


def _even_mix_kernel(seq_len, tile,
                     q_ref, kc_ref, kp_ref, kn_ref, vc_ref, vp_ref, vn_ref,
                     posc_ref, prc_ref, prp_ref, prn_ref,
                     cz_ref, czp_ref, czn_ref, bg_ref, sz_ref, x_ref, p_ref,
                     convw_ref, sink_ref, anorm_ref, bnorm_ref, wout_ref,
                     g_ref, b_ref, gate_ref, proj_ref,
                     o_ref, kbuf, vbuf, posbuf, ybuf):
    i = pl.program_id(1)
    n_tiles = pl.num_programs(1)
    sub = tile // BLOCK

    kbuf[0:BLOCK] = kp_ref[...]
    kbuf[BLOCK:BLOCK + tile] = kc_ref[...]
    kbuf[BLOCK + tile:] = kn_ref[...]
    vbuf[0:BLOCK] = vp_ref[...]
    vbuf[BLOCK:BLOCK + tile] = vc_ref[...]
    vbuf[BLOCK + tile:] = vn_ref[...]
    posbuf[:, 0:BLOCK] = prp_ref[...]
    posbuf[:, BLOCK:BLOCK + tile] = prc_ref[...]
    posbuf[:, BLOCK + tile:] = prn_ref[...]

    q_idx = lax.broadcasted_iota(jnp.int32, (BLOCK, 3 * BLOCK), 0)
    w_idx = lax.broadcasted_iota(jnp.int32, (BLOCK, 3 * BLOCK), 1)
    in_band = jnp.abs(w_idx - BLOCK - q_idx) <= WINDOW

    for j in range(sub):
        rows = slice(j * BLOCK, (j + 1) * BLOCK)
        win = slice(j * BLOCK, j * BLOCK + 3 * BLOCK)
        k_glob = (i * sub + j) * BLOCK + w_idx - BLOCK
        valid = in_band & (k_glob >= 0) & (k_glob < seq_len)
        dist = jnp.abs(posc_ref[rows, :] - posbuf[:, win]).astype(F32)
        dist = jnp.where(valid, dist, jnp.inf)
        for h in range(A_HEADS):
            g = h // A_GROUP
            qh = q_ref[rows, h * A_HEAD_DIM:(h + 1) * A_HEAD_DIM]
            kh = kbuf[win, g * A_HEAD_DIM:(g + 1) * A_HEAD_DIM]
            vh = vbuf[win, g * A_HEAD_DIM:(g + 1) * A_HEAD_DIM]
            logits = _dot_nt(qh, kh) - A_SLOPES[h] * dist
            sink = sink_ref[h]
            m = jnp.maximum(jnp.max(logits, axis=-1, keepdims=True), sink)
            e = jnp.exp(logits - m)
            denom = jnp.sum(e, axis=-1, keepdims=True) + jnp.exp(sink - m)
            out = _dot(e.astype(BF16), vh) / denom
            ybuf[rows, h * A_HEAD_DIM:(h + 1) * A_HEAD_DIM] = out

    cz = cz_ref[...]
    row = lax.broadcasted_iota(jnp.int32, cz.shape, 0)
    prev_row = jnp.where(i > 0, czp_ref[7:8, :], 0.0)
    next_row = jnp.where(i < n_tiles - 1, czn_ref[0:1, :], 0.0)
    z_prev = jnp.where(row == 0, prev_row, pltpu.roll(cz, 1, axis=0))
    z_next = jnp.where(row == tile - 1, next_row, pltpu.roll(cz, tile - 1, axis=0))
    conv = (convw_ref[0:1, :] * z_prev + convw_ref[1:2, :] * cz
            + convw_ref[2:3, :] * z_next)
    yb = bg_ref[...] * conv

    ya_n = _rms(ybuf[...], anorm_ref[...]) * sz_ref[:, 0:A_WIDTH]
    yb_n = _rms(yb, bnorm_ref[...]) * sz_ref[:, A_WIDTH:]
    mix = (_dot(ya_n.astype(BF16), wout_ref[0:A_WIDTH, :])
           + _dot(yb_n.astype(BF16), wout_ref[A_WIDTH:, :]))
    o_ref[...] = _post_block(mix, x_ref[...], p_ref[...], g_ref, b_ref, gate_ref, proj_ref)


def _odd_proj_kernel(x_ref, pos_ref, w_ref, inv_ref, qn_ref, wq_ref, wqs_ref,
                     kvn_ref, wkn_ref, wke_ref, wv_ref, vones_ref,
                     lng_ref, lnb_ref, ws_ref, bs_ref, dnorm_ref,
                     q_ref, k_ref, v_ref, yd_ref, szc_ref):
    tile = x_ref.shape[0]
    xb = x_ref[...].astype(BF16)

    def seg(i):
        return _dot(xb, w_ref[:, _OD_OFF[i]:_OD_OFF[i + 1]])

    ang = pos_ref[...].astype(F32) * inv_ref[...]
    cos_t = jnp.cos(ang)
    sin_t = jnp.sin(ang)

    a = seg(0)
    cqn = _rms(a[:, 0:C_Q_RANK], qn_ref[...]).astype(BF16)
    ckvn = _rms(a[:, C_Q_RANK:], kvn_ref[...]).astype(BF16)

    qa = _dot(cqn, wq_ref[...])
    qs = _dot(cqn, wqs_ref[...])
    for h in range(C_HEADS):
        sl = slice(h * SLOT, (h + 1) * SLOT)
        q_ref[:, sl] = ((qa[:, sl] * cos_t + qs[:, sl] * sin_t) * MLA_SCALE_LOG2E).astype(BF16)

    kr = seg(1)
    kr_rot = (kr[:, 0:LANES] * cos_t + kr[:, LANES:] * sin_t).astype(BF16)
    k_ref[...] = (_dot(ckvn, wkn_ref[...]) + _dot(kr_rot, wke_ref[...])).astype(BF16)
    v_ref[...] = (_dot(ckvn, wv_ref[...]) + vones_ref[...]).astype(BF16)

    gu = _gelu_tanh(seg(2))
    vn = _layer_norm(_gelu_tanh(seg(3)), lng_ref[...], lnb_ref[...]).astype(BF16)
    chunks = []
    for c in range(tile // D_CHUNK):
        rows = slice(c * D_CHUNK, (c + 1) * D_CHUNK)
        groups = [_dot(ws_ref[g], vn[rows, g * D_GROUP_DIM:(g + 1) * D_GROUP_DIM])
                  for g in range(D_GROUPS)]
        chunks.append(jnp.concatenate(groups, axis=1) + bs_ref[...])
    yd = gu * jnp.concatenate(chunks, axis=0)
    sz = _silu(seg(4))
    yd_ref[...] = (_rms(yd, dnorm_ref[...]) * sz[:, C_WIDTH:]).astype(BF16)
    szc_ref[...] = sz[:, 0:C_WIDTH]


def _mla_kernel(seq_len, tq, tk, q_ref, k_ref, v_ref, o_ref, s_a, s_b, m_a, m_b):
    n_qt = seq_len // tq
    n_c = seq_len // tk

    def scores(qt, hh, s_ref, m_ref):
        sl = slice(hh * SLOT, (hh + 1) * SLOT)
        q = q_ref[pl.ds(pl.multiple_of(qt * tq, tq), tq), sl]
        part = None
        for c in range(n_c):
            s = _dot_nt(q, k_ref[c * tk:(c + 1) * tk, sl])
            s_ref[:, c * tk:(c + 1) * tk] = s
            for j in range(tk // LANES):
                t = s[:, j * LANES:(j + 1) * LANES]
                part = t if part is None else jnp.maximum(part, t)
        m_ref[...] = jnp.broadcast_to(jnp.max(part, axis=-1, keepdims=True), m_ref.shape)

    def attend(hh, s_ref, m_ref):
        sl = slice(hh * SLOT, (hh + 1) * SLOT)
        m = m_ref[...]
        acc = None
        for c in range(n_c):
            p = jnp.concatenate(
                [jnp.exp2(s_ref[:, c * tk + j * LANES:c * tk + (j + 1) * LANES] - m).astype(BF16)
                 for j in range(tk // LANES)], axis=1)
            d = _dot(p, v_ref[c * tk:(c + 1) * tk, sl])
            acc = d if acc is None else acc + d
        return acc / acc[:, C_V:C_V + 1]

    scores(0, 0, s_a, m_a)

    def body(qt, carry):
        scores(qt, 1, s_b, m_b)
        out0 = attend(0, s_a, m_a)
        scores(jnp.minimum(qt + 1, n_qt - 1), 0, s_a, m_a)
        out1 = attend(1, s_b, m_b)
        lane = lax.broadcasted_iota(jnp.int32, out0.shape, 1)
        o_ref[pl.ds(pl.multiple_of(qt * tq, tq), tq), :] = jnp.where(
            lane < C_V, out0, pltpu.roll(out1, C_V, axis=1))
        return carry

    lax.fori_loop(0, n_qt, body, 0)


def _odd_out_kernel(yc_ref, szc_ref, yd_ref, x_ref, p_ref, cnorm_ref, wout_ref,
                    g_ref, b_ref, gate_ref, proj_ref, o_ref):
    yc_n = _rms(yc_ref[...], cnorm_ref[...]) * szc_ref[...]
    mix = (_dot(yc_n.astype(BF16), wout_ref[0:C_WIDTH, :])
           + _dot(yd_ref[...], wout_ref[C_WIDTH:, :]))
    o_ref[...] = _post_block(mix, x_ref[...], p_ref[...], g_ref, b_ref, gate_ref, proj_ref)


def _full(shape):
    return pl.BlockSpec(shape, lambda *_: (0,) * len(shape))


def _params(sem):
    return pltpu.CompilerParams(dimension_semantics=sem, vmem_limit_bytes=VMEM_LIMIT)


def _even_layer(x, p, layer, positions, w_in, conv_w, sink, a_norm, b_norm, w_out,
                ln_g, ln_b, ple_proj, ple_gate):
    bsz, seq, _ = x.shape
    n_tok = bsz * seq
    tm = 512
    w_in_b = w_in.astype(BF16)
    row_spec = lambda w: pl.BlockSpec((tm, w), lambda t: (t, 0))
    q, k, v, bg, cz, sz = pl.pallas_call(
        _even_proj_kernel,
        grid=(n_tok // tm,),
        in_specs=[row_spec(D_MODEL), _full(w_in_b.shape)],
        out_specs=[row_spec(512), row_spec(128), row_spec(128), row_spec(512),
                   row_spec(512), row_spec(1024)],
        out_shape=[jax.ShapeDtypeStruct((n_tok, 512), BF16),
                   jax.ShapeDtypeStruct((n_tok, 128), BF16),
                   jax.ShapeDtypeStruct((n_tok, 128), BF16),
                   jax.ShapeDtypeStruct((n_tok, 512), F32),
                   jax.ShapeDtypeStruct((n_tok, 512), F32),
                   jax.ShapeDtypeStruct((n_tok, 1024), F32)],
        compiler_params=_params(("parallel",)),
        name="even_proj",
    )(x.reshape(n_tok, D_MODEL), w_in_b)

    tile = 512
    sub = tile // BLOCK
    nb = seq // BLOCK
    r8 = tile // 8
    q = q.reshape(bsz, seq, 512)
    k = k.reshape(bsz, seq, 128)
    v = v.reshape(bsz, seq, 128)
    bg = bg.reshape(bsz, seq, 512)
    cz = cz.reshape(bsz, seq, 512)
    sz = sz.reshape(bsz, seq, 1024)
    pos_col = positions.reshape(bsz, seq, 1)
    pos_row = positions.reshape(bsz, 1, seq)

    cur = lambda w: pl.BlockSpec((None, tile, w), lambda b, i: (b, i, 0))
    prev_blk = pl.BlockSpec((None, BLOCK, 128), lambda b, i: (b, jnp.maximum(i * sub - 1, 0), 0))
    next_blk = pl.BlockSpec((None, BLOCK, 128), lambda b, i: (b, jnp.minimum((i + 1) * sub, nb - 1), 0))
    in_specs = [
        cur(512),
        cur(128), prev_blk, next_blk,
        cur(128), prev_blk, next_blk,
        pl.BlockSpec((None, tile, 1), lambda b, i: (b, i, 0)),
        pl.BlockSpec((None, 1, tile), lambda b, i: (b, 0, i)),
        pl.BlockSpec((None, 1, BLOCK), lambda b, i: (b, 0, jnp.maximum(i * sub - 1, 0))),
        pl.BlockSpec((None, 1, BLOCK), lambda b, i: (b, 0, jnp.minimum((i + 1) * sub, nb - 1))),
        cur(512),
        pl.BlockSpec((None, 8, 512), lambda b, i: (b, jnp.maximum(i * r8 - 1, 0), 0)),
        pl.BlockSpec((None, 8, 512), lambda b, i: (b, jnp.minimum((i + 1) * r8, seq // 8 - 1), 0)),
        cur(512), cur(1024), cur(D_MODEL),
        pl.BlockSpec((None, None, tile, PLE_DIM), lambda b, i: (layer, b, i, 0)),
        _full((3, 512)),
        pl.BlockSpec(memory_space=pltpu.SMEM),
        _full((1, 512)), _full((1, 512)), _full((D_MODEL, D_MODEL)),
        _full((1, D_MODEL)), _full((1, D_MODEL)), _full((D_MODEL, D_MODEL)),
        _full((PLE_DIM, D_MODEL)),
    ]
    return pl.pallas_call(
        functools.partial(_even_mix_kernel, seq, tile),
        grid=(bsz, seq // tile),
        in_specs=in_specs,
        out_specs=cur(D_MODEL),
        out_shape=jax.ShapeDtypeStruct((bsz, seq, D_MODEL), F32),
        scratch_shapes=[pltpu.VMEM((tile + 2 * BLOCK, 128), BF16),
                        pltpu.VMEM((tile + 2 * BLOCK, 128), BF16),
                        pltpu.VMEM((1, tile + 2 * BLOCK), jnp.int32),
                        pltpu.VMEM((tile, A_WIDTH), F32)],
        compiler_params=_params(("parallel", "parallel")),
        name="even_mix",
    )(q, k, k, k, v, v, v, pos_col, pos_row, pos_row, pos_row, cz, cz, cz, bg, sz, x, p,
      conv_w, sink, a_norm.reshape(1, -1), b_norm.reshape(1, -1), w_out.astype(BF16),
      ln_g.reshape(1, -1), ln_b.reshape(1, -1), ple_gate.astype(BF16), ple_proj.astype(BF16))


def _swap_halves(r):
    half = r.shape[-1] // 2
    return jnp.concatenate([-r[..., half:], r[..., :half]], axis=-1)


def _odd_weights(w_in, w_uq, w_ukv, b_s):
    d = w_in.shape[0]
    o_kr = C_Q_RANK + C_KV_RANK
    o_du = o_kr + C_ROPE
    w_kr = w_in[:, o_kr:o_du]
    kr_pair = jnp.zeros((d, 2 * LANES), F32)
    kr_pair = kr_pair.at[:, C_NOPE:C_NOPE + C_ROPE].set(w_kr)
    kr_pair = kr_pair.at[:, LANES + C_NOPE:LANES + C_NOPE + C_ROPE].set(_swap_halves(w_kr))
    w1 = jnp.concatenate([w_in[:, :o_kr], kr_pair, w_in[:, o_du:]], axis=1).astype(BF16)

    uq = w_uq.reshape(C_Q_RANK, C_HEADS, C_NOPE + C_ROPE)
    pad = SLOT - C_NOPE - C_ROPE
    wq = jnp.pad(uq, ((0, 0), (0, 0), (0, pad)))
    wqs = jnp.pad(_swap_halves(uq[..., C_NOPE:]), ((0, 0), (0, 0), (C_NOPE, pad)))
    ukv = w_ukv.reshape(C_KV_RANK, C_HEADS, C_NOPE + C_V)
    wkn = jnp.pad(ukv[..., :C_NOPE], ((0, 0), (0, 0), (0, SLOT - C_NOPE)))
    wv = jnp.pad(ukv[..., C_NOPE:], ((0, 0), (0, 0), (0, SLOT - C_V)))
    flat = lambda w: w.reshape(w.shape[0], C_HEADS * SLOT).astype(BF16)

    lane = jnp.arange(LANES)
    rope_lane = (lane >= C_NOPE) & (lane < C_NOPE + C_ROPE)
    wke = jnp.tile(jnp.where(rope_lane[:, None] & (lane[:, None] == lane[None, :]), 1.0, 0.0),
                   (1, C_HEADS)).astype(BF16)
    vones = jnp.tile(jnp.where(lane == C_V, 1.0, 0.0), C_HEADS).reshape(1, -1).astype(F32)
    half = C_ROPE // 2
    inv = ROPE_THETA ** (-jnp.arange(half, dtype=F32) / half)
    inv_lane = jnp.where(rope_lane, inv[(lane - C_NOPE) % half], 0.0).reshape(1, LANES).astype(F32)
    bs_full = jnp.repeat(b_s.T, D_GROUP_DIM, axis=1)
    return w1, flat(wq), flat(wqs), flat(wkn), wke, flat(wv), vones, inv_lane, bs_full


def _odd_layer(x, p, layer, positions, w_in, q_norm, w_uq, kv_norm, w_ukv, v_ln_g, v_ln_b,
               w_s, b_s, c_norm, d_norm, w_out, ln_g, ln_b, ple_proj, ple_gate):
    bsz, seq, _ = x.shape
    n_tok = bsz * seq
    w1, wq, wqs, wkn, wke, wv, vones, inv_lane, bs_full = _odd_weights(w_in, w_uq, w_ukv, b_s)
    tm = 512
    row_spec = lambda w: pl.BlockSpec((tm, w), lambda t: (t, 0))
    wide = C_HEADS * SLOT
    q, k, v, yd, szc = pl.pallas_call(
        _odd_proj_kernel,
        grid=(n_tok // tm,),
        in_specs=[row_spec(D_MODEL), row_spec(1), _full(w1.shape), _full((1, LANES)),
                  _full((1, C_Q_RANK)), _full(wq.shape), _full(wqs.shape),
                  _full((1, C_KV_RANK)), _full(wkn.shape), _full(wke.shape), _full(wv.shape),
                  _full((1, wide)), _full((1, D_WIDTH)), _full((1, D_WIDTH)),
                  _full((D_GROUPS, D_CHUNK, D_CHUNK)), _full((D_CHUNK, D_WIDTH)),
                  _full((1, D_WIDTH))],
        out_specs=[row_spec(wide), row_spec(wide), row_spec(wide), row_spec(D_WIDTH),
                   row_spec(C_WIDTH)],
        out_shape=[jax.ShapeDtypeStruct((n_tok, wide), BF16),
                   jax.ShapeDtypeStruct((n_tok, wide), BF16),
                   jax.ShapeDtypeStruct((n_tok, wide), BF16),
                   jax.ShapeDtypeStruct((n_tok, D_WIDTH), BF16),
                   jax.ShapeDtypeStruct((n_tok, C_WIDTH), F32)],
        compiler_params=_params(("parallel",)),
        name="odd_proj",
    )(x.reshape(n_tok, D_MODEL), positions.reshape(n_tok, 1), w1, inv_lane,
      q_norm.reshape(1, -1), wq, wqs, kv_norm.reshape(1, -1), wkn, wke, wv, vones,
      v_ln_g.reshape(1, -1), v_ln_b.reshape(1, -1), w_s.astype(BF16), bs_full,
      d_norm.reshape(1, -1))

    tq = 256
    tk = 512
    q = q.reshape(bsz, seq, wide)
    k = k.reshape(bsz, seq, wide)
    v = v.reshape(bsz, seq, wide)
    pair = lambda w: pl.BlockSpec((None, seq, w), lambda b, h: (b, 0, h))
    yc = pl.pallas_call(
        functools.partial(_mla_kernel, seq, tq, tk),
        grid=(bsz, C_HEADS // 2),
        in_specs=[pair(2 * SLOT), pair(2 * SLOT), pair(2 * SLOT)],
        out_specs=pair(2 * C_V),
        out_shape=jax.ShapeDtypeStruct((bsz, seq, C_WIDTH), F32),
        scratch_shapes=[pltpu.VMEM((tq, seq), F32), pltpu.VMEM((tq, seq), F32),
                        pltpu.VMEM((tq, LANES), F32), pltpu.VMEM((tq, LANES), F32)],
        compiler_params=_params(("parallel", "parallel")),
        name="mla_attn",
    )(q, k, v)

    out = pl.pallas_call(
        _odd_out_kernel,
        grid=(n_tok // tm,),
        in_specs=[row_spec(C_WIDTH), row_spec(C_WIDTH), row_spec(D_WIDTH), row_spec(D_MODEL),
                  pl.BlockSpec((None, tm, PLE_DIM), lambda t: (layer, t, 0)),
                  _full((1, C_WIDTH)), _full((D_MODEL, D_MODEL)), _full((1, D_MODEL)),
                  _full((1, D_MODEL)), _full((D_MODEL, D_MODEL)), _full((PLE_DIM, D_MODEL))],
        out_specs=row_spec(D_MODEL),
        out_shape=jax.ShapeDtypeStruct((n_tok, D_MODEL), F32),
        compiler_params=_params(("parallel",)),
        name="odd_out",
    )(yc.reshape(n_tok, C_WIDTH), szc, yd, x.reshape(n_tok, D_MODEL),
      p.reshape(DEPTH, n_tok, PLE_DIM), c_norm.reshape(1, -1), w_out.astype(BF16),
      ln_g.reshape(1, -1), ln_b.reshape(1, -1), ple_gate.astype(BF16), ple_proj.astype(BF16))
    return out.reshape(bsz, seq, D_MODEL)


def kernel(x, p, positions, ev_w_in, ev_conv_w, ev_sink, ev_a_norm, ev_b_norm, ev_w_out, od_w_in, od_q_norm, od_w_uq, od_kv_norm, od_w_ukv, od_v_ln_g, od_v_ln_b, od_w_s, od_b_s, od_c_norm, od_d_norm, od_w_out, post_ln_g, post_ln_b, ple_proj, ple_gate):
    for i in range(DEPTH):
        j = i // 2
        if i % 2 == 0:
            x = _even_layer(x, p, i, positions, ev_w_in[j], ev_conv_w[j], ev_sink[j],
                            ev_a_norm[j], ev_b_norm[j], ev_w_out[j],
                            post_ln_g[i], post_ln_b[i], ple_proj[i], ple_gate[i])
        else:
            x = _odd_layer(x, p, i, positions, od_w_in[j], od_q_norm[j], od_w_uq[j],
                           od_kv_norm[j], od_w_ukv[j], od_v_ln_g[j], od_v_ln_b[j],
                           od_w_s[j], od_b_s[j], od_c_norm[j], od_d_norm[j], od_w_out[j],
                           post_ln_g[i], post_ln_b[i], ple_proj[i], ple_gate[i])
    return x
```

```python
import functools
import math

import jax
import jax.numpy as jnp
from jax import lax
from jax.experimental import pallas as pl
from jax.experimental.pallas import tpu as pltpu

F32 = jnp.float32
BF16 = jnp.bfloat16

D_MODEL = 1024
DEPTH = 2
PLE_DIM = 256
DEEPNORM_ALPHA = (2.0 * DEPTH) ** 0.25
BLOCK = 128
EPS = 1e-6
A_HEAD_DIM = 64
A_HEADS = 8
A_KV_HEADS = 2
A_GROUP = A_HEADS // A_KV_HEADS
A_WIDTH = A_HEADS * A_HEAD_DIM
WINDOW = 128
B_WIDTH = 512
C_HEADS = 8
C_NOPE = 64
C_ROPE = 32
C_V = 64
C_WIDTH = C_HEADS * C_V
C_Q_RANK = 256
C_KV_RANK = 128
ROPE_THETA = 10000.0
D_WIDTH = 512
D_GROUPS = 4
D_GROUP_DIM = D_WIDTH // D_GROUPS
D_CHUNK = 128

LANES = 128
SLOT = LANES
VMEM_LIMIT = 56 * 1024 * 1024

_EV_OFF = (0, 512, 640, 768, 1280, 1792, 2304, 3328)
_OD_OFF = (0, 384, 640, 1152, 1664, 2688)

A_SLOPES = tuple(2.0 ** (-8.0 * (h + 1) / A_HEADS) for h in range(A_HEADS))
LOG2E = math.log2(math.e)
A_SCALE_LOG2E = A_HEAD_DIM ** -0.5 * LOG2E
MLA_SCALE_LOG2E =(C_NOPE + C_ROPE) ** -0.5 * math.log2(math.e)


def _dot(a, b):
    return jnp.dot(a, b, preferred_element_type=F32)


def _dot_nt(a, b):
    return lax.dot_general(a, b, (((1,), (1,)), ((), ())), preferred_element_type=F32)


def _rms(x, g):
    return x * lax.rsqrt(jnp.mean(x * x, axis=-1, keepdims=True) + EPS) * g


def _layer_norm(x, g, b):
    mu = jnp.mean(x, axis=-1, keepdims=True)
    xc = x - mu
    var = jnp.mean(xc * xc, axis=-1, keepdims=True)
    return xc * lax.rsqrt(var + EPS) * g + b


def _silu(z):
    return z * jax.nn.sigmoid(z)


def _gelu_tanh(x):
    c = math.sqrt(2.0 / math.pi)
    return 0.5 * x * (1.0 + jnp.tanh(c * (x + 0.044715 * (x * x * x))))


def _post_block(y_bf16_dot, x, p, g_ref, b_ref, gate_ref, proj_ref):
    h = _layer_norm(DEEPNORM_ALPHA * x + y_bf16_dot, g_ref[...], b_ref[...])
    gate = jax.nn.sigmoid(_dot(h.astype(BF16), gate_ref[...]))
    return h + gate * _dot(p.astype(BF16), proj_ref[...])


def _even_proj_kernel(x_ref, w_ref, q_ref, k_ref, v_ref, bg_ref, cz_ref, sz_ref):
    xb = x_ref[...].astype(BF16)

    def seg(i):
        return _dot(xb, w_ref[:, _EV_OFF[i]:_EV_OFF[i + 1]])

    lane = lax.broadcasted_iota(jnp.int32, (x_ref.shape[0], LANES), 1)
    low = lane < A_HEAD_DIM
    one = jnp.where(lane == A_HEAD_DIM, 1.0, 0.0)

    def slots(t, fill):
        return (jnp.where(low, t, fill),
                jnp.where(low, pltpu.roll(t, A_HEAD_DIM, axis=1), fill))

    q = seg(0) * A_SCALE_LOG2E
    for t in range(A_WIDTH // LANES):
        lo, hi = slots(q[:, t * LANES:(t + 1) * LANES], 0.0)
        q_ref[:, (2 * t) * SLOT:(2 * t + 1) * SLOT] = lo.astype(BF16)
        q_ref[:, (2 * t + 1) * SLOT:(2 * t + 2) * SLOT] = hi.astype(BF16)
    lo, hi = slots(seg(1), 0.0)
    k_ref[:, 0:SLOT] = lo.astype(BF16)
    k_ref[:, SLOT:] = hi.astype(BF16)
    lo, hi = slots(seg(2), one)
    v_ref[:, 0:SLOT] = lo.astype(BF16)
    v_ref[:, SLOT:] = hi.astype(BF16)
    bg_ref[...] = seg(3)
    cz_ref[...] = seg(4) * seg(5)
    sz_ref[...] = _silu(seg(6))


def _even_mix_kernel(seq_len, tile,
                     q_ref, kc_ref, kp_ref, kn_ref, vc_ref, vp_ref, vn_ref,
                     posc_ref, prc_ref, prp_ref, prn_ref,
                     cz_ref, czp_ref, czn_ref, bg_ref, sz_ref, x_ref, p_ref,
                     convw_ref, sink_ref, anorm_ref, bnorm_ref, wout_ref,
                     g_ref, b_ref, gate_ref, proj_ref,
                     o_ref, kbuf, vbuf, posbuf, ybuf, lg_a, lg_b, m_a, m_b):
    i = pl.program_id(1)
    n_tiles = pl.num_programs(1)
    sub = tile // BLOCK

    kbuf[0:BLOCK] = kp_ref[...]
    kbuf[BLOCK:BLOCK + tile] = kc_ref[...]
    kbuf[BLOCK + tile:] = kn_ref[...]
    vbuf[0:BLOCK] = vp_ref[...]
    vbuf[BLOCK:BLOCK + tile] = vc_ref[...]
    vbuf[BLOCK + tile:] = vn_ref[...]
    posbuf[:, 0:BLOCK] = prp_ref[...]
    posbuf[:, BLOCK:BLOCK + tile] = prc_ref[...]
    posbuf[:, BLOCK + tile:] = prn_ref[...]

    q_idx = lax.broadcasted_iota(jnp.int32, (BLOCK, 3 * BLOCK), 0)
    w_idx = lax.broadcasted_iota(jnp.int32, (BLOCK, 3 * BLOCK), 1)
    in_band = jnp.abs(w_idx - BLOCK - q_idx) <= WINDOW

    lane = lax.broadcasted_iota(jnp.int32, (BLOCK, LANES), 1)
    dists = {}

    def masked_dist(j):
        if j not in dists:
            k_glob = (i * sub + j) * BLOCK + w_idx - BLOCK
            valid = in_band & (k_glob >= 0) & (k_glob < seq_len)
            d = jnp.abs(posc_ref[j * BLOCK:(j + 1) * BLOCK, :]
                        - posbuf[:, j * BLOCK:(j + 3) * BLOCK]).astype(F32)
            dists[j] = jnp.where(valid, d, jnp.inf)
        return dists[j]

    def scores(j, g, lg_ref, m_ref):
        rows = slice(j * BLOCK, (j + 1) * BLOCK)
        qg = jnp.concatenate([q_ref[rows, (A_GROUP * g + hh) * SLOT:(A_GROUP * g + hh + 1) * SLOT]
                              for hh in range(A_GROUP)], axis=0)
        s = _dot_nt(qg, kbuf[j * BLOCK:(j + 3) * BLOCK, g * SLOT:(g + 1) * SLOT])
        d = masked_dist(j)
        for hh in range(A_GROUP):
            h = A_GROUP * g + hh
            r = slice(hh * BLOCK, (hh + 1) * BLOCK)
            lg = s[r, :] - (A_SLOPES[h] * LOG2E) * d
            lg_ref[r, :] = lg
            mx = jnp.maximum(jnp.max(lg, axis=-1, keepdims=True), sink_ref[h])
            m_ref[r, :] = jnp.broadcast_to(mx, (BLOCK, LANES))

    def attend(j, g, lg_ref, m_ref):
        rows = slice(j * BLOCK, (j + 1) * BLOCK)
        m = m_ref[...]
        e = jnp.concatenate([jnp.exp2(lg_ref[:, t * LANES:(t + 1) * LANES] - m).astype(BF16)
                             for t in range(3)], axis=1)
        pv = _dot(e, vbuf[j * BLOCK:(j + 3) * BLOCK, g * SLOT:(g + 1) * SLOT])
        outs = []
        for hh in range(A_GROUP):
            r = slice(hh * BLOCK, (hh + 1) * BLOCK)
            denom = (jnp.broadcast_to(pv[r, A_HEAD_DIM:A_HEAD_DIM + 1], (BLOCK, LANES))
                     + jnp.exp2(sink_ref[A_GROUP * g + hh] - m[r, :]))
            outs.append(pv[r, :] / denom)
        for pr in range(A_GROUP // 2):
            c0 = (A_GROUP // 2 * g + pr) * LANES
            ybuf[rows, c0:c0 + LANES] = jnp.where(
                lane < A_HEAD_DIM, outs[2 * pr], pltpu.roll(outs[2 * pr + 1], A_HEAD_DIM, axis=1))

    chains = [(j, g) for j in range(sub) for g in range(A_KV_HEADS)]
    bufs = [(lg_a, m_a), (lg_b, m_b)]
    scores(*chains[0], *bufs[0])
    for c, (j, g) in enumerate(chains):
        if c + 1 < len(chains):
            scores(*chains[c + 1], *bufs[(c + 1) % 2])
        attend(j, g, *bufs[c % 2])

    cz = cz_ref[...]
    row = lax.broadcasted_iota(jnp.int32, cz.shape, 0)
    prev_row = jnp.where(i > 0, czp_ref[7:8, :], 0.0)
    next_row = jnp.where(i < n_tiles - 1, czn_ref[0:1, :], 0.0)
    z_prev = jnp.where(row == 0, prev_row, pltpu.roll(cz, 1, axis=0))
    z_next = jnp.where(row == tile - 1, next_row, pltpu.roll(cz, tile - 1, axis=0))
    conv = (convw_ref[0:1, :] * z_prev + convw_ref[1:2, :] * cz
            + convw_ref[2:3, :] * z_next)
    yb = bg_ref[...] * conv

    ya_n = _rms(ybuf[...], anorm_ref[...]) * sz_ref[:, 0:A_WIDTH]
    yb_n = _rms(yb, bnorm_ref[...]) * sz_ref[:, A_WIDTH:]
    mix = (_dot(ya_n.astype(BF16), wout_ref[0:A_WIDTH, :])
           + _dot(yb_n.astype(BF16), wout_ref[A_WIDTH:, :]))
    o_ref[...] = _post_block(mix, x_ref[...], p_ref[...], g_ref, b_ref, gate_ref, proj_ref)


def _odd_proj_kernel(x_ref, pos_ref, w_ref, inv_ref, qn_ref, wq_ref, wqs_ref,
                     kvn_ref, wkn_ref, wke_ref, wv_ref, vones_ref,
                     lng_ref, lnb_ref, ws_ref, bs_ref, dnorm_ref,
                     q_ref, k_ref, v_ref, yd_ref, szc_ref):
    tile = x_ref.shape[0]
    xb = x_ref[...].astype(BF16)

    def seg(i):
        return _dot(xb, w_ref[:, _OD_OFF[i]:_OD_OFF[i + 1]])

    ang = pos_ref[...].astype(F32) * inv_ref[...]
    cos_t = jnp.cos(ang)
    sin_t = jnp.sin(ang)

    a = seg(0)
    cqn = _rms(a[:, 0:C_Q_RANK], qn_ref[...]).astype(BF16)
    ckvn = _rms(a[:, C_Q_RANK:], kvn_ref[...]).astype(BF16)

    qa = _dot(cqn, wq_ref[...])
    qs = _dot(cqn, wqs_ref[...])
    for h in range(C_HEADS):
        sl = slice(h * SLOT, (h + 1) * SLOT)
        q_ref[:, sl] = ((qa[:, sl] * cos_t + qs[:, sl] * sin_t) * MLA_SCALE_LOG2E).astype(BF16)

    kr = seg(1)
    kr_rot = (kr[:, 0:LANES] * cos_t + kr[:, LANES:] * sin_t).astype(BF16)
    k_ref[...] = (_dot(ckvn, wkn_ref[...]) + _dot(kr_rot, wke_ref[...])).astype(BF16)
    v_ref[...] = (_dot(ckvn, wv_ref[...]) + vones_ref[...]).astype(BF16)

    gu = _gelu_tanh(seg(2))
    vn = _layer_norm(_gelu_tanh(seg(3)), lng_ref[...], lnb_ref[...]).astype(BF16)
    chunks = []
    for c in range(tile // D_CHUNK):
        rows = slice(c * D_CHUNK, (c + 1) * D_CHUNK)
        groups = [_dot(ws_ref[g], vn[rows, g * D_GROUP_DIM:(g + 1) * D_GROUP_DIM])
                  for g in range(D_GROUPS)]
        chunks.append(jnp.concatenate(groups, axis=1) + bs_ref[...])
    yd = gu * jnp.concatenate(chunks, axis=0)
    sz = _silu(seg(4))
    yd_ref[...] = (_rms(yd, dnorm_ref[...]) * sz[:, C_WIDTH:]).astype(BF16)
    szc_ref[...] = sz[:, 0:C_WIDTH]


def _mla_kernel(seq_len, tq, tk, q_ref, k_ref, v_ref, o_ref, s_a, s_b, m_a, m_b):
    n_qt = seq_len // tq
    n_c = seq_len // tk

    def scores(qt, hh, s_ref, m_ref):
        sl = slice(hh * SLOT, (hh + 1) * SLOT)
        q = q_ref[pl.ds(pl.multiple_of(qt * tq, tq), tq), sl]
        part = None
        for c in range(n_c):
            s = _dot_nt(q, k_ref[c * tk:(c + 1) * tk, sl])
            s_ref[:, c * tk:(c + 1) * tk] = s
            for j in range(tk // LANES):
                t = s[:, j * LANES:(j + 1) * LANES]
                part = t if part is None else jnp.maximum(part, t)
        m_ref[...] = jnp.broadcast_to(jnp.max(part, axis=-1, keepdims=True), m_ref.shape)

    def attend(hh, s_ref, m_ref):
        sl = slice(hh * SLOT, (hh + 1) * SLOT)
        m = m_ref[...]
        acc = None
        for c in range(n_c):
            p = jnp.concatenate(
                [jnp.exp2(s_ref[:, c * tk + j * LANES:c * tk + (j + 1) * LANES] - m).astype(BF16)
                 for j in range(tk // LANES)], axis=1)
            d = _dot(p, v_ref[c * tk:(c + 1) * tk, sl])
            acc = d if acc is None else acc + d
        return acc / acc[:, C_V:C_V + 1]

    scores(0, 0, s_a, m_a)

    def body(qt, carry):
        scores(qt, 1, s_b, m_b)
        out0 = attend(0, s_a, m_a)
        scores(jnp.minimum(qt + 1, n_qt - 1), 0, s_a, m_a)
        out1 = attend(1, s_b, m_b)
        lane = lax.broadcasted_iota(jnp.int32, out0.shape, 1)
        o_ref[pl.ds(pl.multiple_of(qt * tq, tq), tq), :] = jnp.where(
            lane < C_V, out0, pltpu.roll(out1, C_V, axis=1))
        return carry

    lax.fori_loop(0, n_qt, body, 0)


def _odd_out_kernel(yc_ref, szc_ref, yd_ref, x_ref, p_ref, cnorm_ref, wout_ref,
                    g_ref, b_ref, gate_ref, proj_ref, o_ref):
    yc_n = _rms(yc_ref[...], cnorm_ref[...]) * szc_ref[...]
    mix = (_dot(yc_n.astype(BF16), wout_ref[0:C_WIDTH, :])
           + _dot(yd_ref[...], wout_ref[C_WIDTH:, :]))
    o_ref[...] = _post_block(mix, x_ref[...], p_ref[...], g_ref, b_ref, gate_ref, proj_ref)


def _full(shape):
    return pl.BlockSpec(shape, lambda *_: (0,) * len(shape))


def _params(sem):
    return pltpu.CompilerParams(dimension_semantics=sem, vmem_limit_bytes=VMEM_LIMIT)


def _even_layer(x, p, layer, positions, w_in, conv_w, sink, a_norm, b_norm, w_out,
                ln_g, ln_b, ple_proj, ple_gate):
    bsz, seq, _ = x.shape
    n_tok = bsz * seq
    tm = 512
    q_w = A_HEADS * SLOT
    kv_w = A_KV_HEADS * SLOT
    w_in_b = w_in.astype(BF16)
    row_spec = lambda w: pl.BlockSpec((tm, w), lambda t: (t, 0))
    q, k, v, bg, cz, sz = pl.pallas_call(
        _even_proj_kernel,
        grid=(n_tok // tm,),
        in_specs=[row_spec(D_MODEL), _full(w_in_b.shape)],
        out_specs=[row_spec(q_w), row_spec(kv_w), row_spec(kv_w), row_spec(512),
                   row_spec(512), row_spec(1024)],
        out_shape=[jax.ShapeDtypeStruct((n_tok, q_w), BF16),
                   jax.ShapeDtypeStruct((n_tok, kv_w), BF16),
                   jax.ShapeDtypeStruct((n_tok, kv_w), BF16),
                   jax.ShapeDtypeStruct((n_tok, 512), F32),
                   jax.ShapeDtypeStruct((n_tok, 512), F32),
                   jax.ShapeDtypeStruct((n_tok, 1024), F32)],
        compiler_params=_params(("parallel",)),
        name="even_proj",
    )(x.reshape(n_tok, D_MODEL), w_in_b)

    tile = 512
    sub = tile // BLOCK
    nb = seq // BLOCK
    r8 = tile // 8
    q = q.reshape(bsz, seq, q_w)
    k = k.reshape(bsz, seq, kv_w)
    v = v.reshape(bsz, seq, kv_w)
    bg = bg.reshape(bsz, seq, 512)
    cz = cz.reshape(bsz, seq, 512)
    sz = sz.reshape(bsz, seq, 1024)
    pos_col = positions.reshape(bsz, seq, 1)
    pos_row = positions.reshape(bsz, 1, seq)

    cur = lambda w: pl.BlockSpec((None, tile, w), lambda b, i: (b, i, 0))
    prev_blk = pl.BlockSpec((None, BLOCK, kv_w), lambda b, i: (b, jnp.maximum(i * sub - 1, 0), 0))
    next_blk = pl.BlockSpec((None, BLOCK, kv_w), lambda b, i: (b, jnp.minimum((i + 1) * sub, nb - 1), 0))
    in_specs = [
        cur(q_w),
        cur(kv_w), prev_blk, next_blk,
        cur(kv_w), prev_blk, next_blk,
        pl.BlockSpec((None, tile, 1), lambda b, i: (b, i, 0)),
        pl.BlockSpec((None, 1, tile), lambda b, i: (b, 0, i)),
        pl.BlockSpec((None, 1, BLOCK), lambda b, i: (b, 0, jnp.maximum(i * sub - 1, 0))),
        pl.BlockSpec((None, 1, BLOCK), lambda b, i: (b, 0, jnp.minimum((i + 1) * sub, nb - 1))),
        cur(512),
        pl.BlockSpec((None, 8, 512), lambda b, i: (b, jnp.maximum(i * r8 - 1, 0), 0)),
        pl.BlockSpec((None, 8, 512), lambda b, i: (b, jnp.minimum((i + 1) * r8, seq // 8 - 1), 0)),
        cur(512), cur(1024), cur(D_MODEL),
        pl.BlockSpec((None, None, tile, PLE_DIM), lambda b, i: (layer, b, i, 0)),
        _full((3, 512)),
        pl.BlockSpec(memory_space=pltpu.SMEM),
        _full((1, 512)), _full((1, 512)), _full((D_MODEL, D_MODEL)),
        _full((1, D_MODEL)), _full((1, D_MODEL)), _full((D_MODEL, D_MODEL)),
        _full((PLE_DIM, D_MODEL)),
    ]
    return pl.pallas_call(
        functools.partial(_even_mix_kernel, seq, tile),
        grid=(bsz, seq // tile),
        in_specs=in_specs,
        out_specs=cur(D_MODEL),
        out_shape=jax.ShapeDtypeStruct((bsz, seq, D_MODEL), F32),
        scratch_shapes=[pltpu.VMEM((tile + 2 * BLOCK, kv_w), BF16),
                        pltpu.VMEM((tile + 2 * BLOCK, kv_w), BF16),
                        pltpu.VMEM((1, tile + 2 * BLOCK), jnp.int32),
                        pltpu.VMEM((tile, A_WIDTH), F32),
                        pltpu.VMEM((A_GROUP * BLOCK, 3 * BLOCK), F32),
                        pltpu.VMEM((A_GROUP * BLOCK, 3 * BLOCK), F32),
                        pltpu.VMEM((A_GROUP * BLOCK, LANES), F32),
                        pltpu.VMEM((A_GROUP * BLOCK, LANES), F32)],
        compiler_params=_params(("parallel", "parallel")),
        name="even_mix",
    )(q, k, k, k, v, v, v, pos_col, pos_row, pos_row, pos_row, cz, cz, cz, bg, sz, x, p,
      conv_w, sink * LOG2E, a_norm.reshape(1, -1), b_norm.reshape(1, -1), w_out.astype(BF16),
      ln_g.reshape(1, -1), ln_b.reshape(1, -1), ple_gate.astype(BF16), ple_proj.astype(BF16))


def _swap_halves(r):
    half = r.shape[-1] // 2
    return jnp.concatenate([-r[..., half:], r[..., :half]], axis=-1)


def _odd_weights(w_in, w_uq, w_ukv, b_s):
    d = w_in.shape[0]
    o_kr = C_Q_RANK + C_KV_RANK
    o_du = o_kr + C_ROPE
    w_kr = w_in[:, o_kr:o_du]
    kr_pair = jnp.zeros((d, 2 * LANES), F32)
    kr_pair = kr_pair.at[:, C_NOPE:C_NOPE + C_ROPE].set(w_kr)
    kr_pair = kr_pair.at[:, LANES + C_NOPE:LANES + C_NOPE + C_ROPE].set(_swap_halves(w_kr))
    w1 = jnp.concatenate([w_in[:, :o_kr], kr_pair, w_in[:, o_du:]], axis=1).astype(BF16)

    uq = w_uq.reshape(C_Q_RANK, C_HEADS, C_NOPE + C_ROPE)
    pad = SLOT - C_NOPE - C_ROPE
    wq = jnp.pad(uq, ((0, 0), (0, 0), (0, pad)))
    wqs = jnp.pad(_swap_halves(uq[..., C_NOPE:]), ((0, 0), (0, 0), (C_NOPE, pad)))
    ukv = w_ukv.reshape(C_KV_RANK, C_HEADS, C_NOPE + C_V)
    wkn = jnp.pad(ukv[..., :C_NOPE], ((0, 0), (0, 0), (0, SLOT - C_NOPE)))
    wv = jnp.pad(ukv[..., C_NOPE:], ((0, 0), (0, 0), (0, SLOT - C_V)))
    flat = lambda w: w.reshape(w.shape[0], C_HEADS * SLOT).astype(BF16)

    lane = jnp.arange(LANES)
    rope_lane = (lane >= C_NOPE) & (lane < C_NOPE + C_ROPE)
    wke = jnp.tile(jnp.where(rope_lane[:, None] & (lane[:, None] == lane[None, :]), 1.0, 0.0),
                   (1, C_HEADS)).astype(BF16)
    vones = jnp.tile(jnp.where(lane == C_V, 1.0, 0.0), C_HEADS).reshape(1, -1).astype(F32)
    half = C_ROPE // 2
    inv = ROPE_THETA ** (-jnp.arange(half, dtype=F32) / half)
    inv_lane = jnp.where(rope_lane, inv[(lane - C_NOPE) % half], 0.0).reshape(1, LANES).astype(F32)
    bs_full = jnp.repeat(b_s.T, D_GROUP_DIM, axis=1)
    return w1, flat(wq), flat(wqs), flat(wkn), wke, flat(wv), vones, inv_lane, bs_full


def _odd_layer(x, p, layer, positions, w_in, q_norm, w_uq, kv_norm, w_ukv, v_ln_g, v_ln_b,
               w_s, b_s, c_norm, d_norm, w_out, ln_g, ln_b, ple_proj, ple_gate):
    bsz, seq, _ = x.shape
    n_tok = bsz * seq
    w1, wq, wqs, wkn, wke, wv, vones, inv_lane, bs_full = _odd_weights(w_in, w_uq, w_ukv, b_s)
    tm = 512
    row_spec = lambda w: pl.BlockSpec((tm, w), lambda t: (t, 0))
    wide = C_HEADS * SLOT
    q, k, v, yd, szc = pl.pallas_call(
        _odd_proj_kernel,
        grid=(n_tok // tm,),
        in_specs=[row_spec(D_MODEL), row_spec(1), _full(w1.shape), _full((1, LANES)),
                  _full((1, C_Q_RANK)), _full(wq.shape), _full(wqs.shape),
                  _full((1, C_KV_RANK)), _full(wkn.shape), _full(wke.shape), _full(wv.shape),
                  _full((1, wide)), _full((1, D_WIDTH)), _full((1, D_WIDTH)),
                  _full((D_GROUPS, D_CHUNK, D_CHUNK)), _full((D_CHUNK, D_WIDTH)),
                  _full((1, D_WIDTH))],
        out_specs=[row_spec(wide), row_spec(wide), row_spec(wide), row_spec(D_WIDTH),
                   row_spec(C_WIDTH)],
        out_shape=[jax.ShapeDtypeStruct((n_tok, wide), BF16),
                   jax.ShapeDtypeStruct((n_tok, wide), BF16),
                   jax.ShapeDtypeStruct((n_tok, wide), BF16),
                   jax.ShapeDtypeStruct((n_tok, D_WIDTH), BF16),
                   jax.ShapeDtypeStruct((n_tok, C_WIDTH), F32)],
        compiler_params=_params(("parallel",)),
        name="odd_proj",
    )(x.reshape(n_tok, D_MODEL), positions.reshape(n_tok, 1), w1, inv_lane,
      q_norm.reshape(1, -1), wq, wqs, kv_norm.reshape(1, -1), wkn, wke, wv, vones,
      v_ln_g.reshape(1, -1), v_ln_b.reshape(1, -1), w_s.astype(BF16), bs_full,
      d_norm.reshape(1, -1))

    tq = 256
    tk = 512
    q = q.reshape(bsz, seq, wide)
    k = k.reshape(bsz, seq, wide)
    v = v.reshape(bsz, seq, wide)
    pair = lambda w: pl.BlockSpec((None, seq, w), lambda b, h: (b, 0, h))
    yc = pl.pallas_call(
        functools.partial(_mla_kernel, seq, tq, tk),
        grid=(bsz, C_HEADS // 2),
        in_specs=[pair(2 * SLOT), pair(2 * SLOT), pair(2 * SLOT)],
        out_specs=pair(2 * C_V),
        out_shape=jax.ShapeDtypeStruct((bsz, seq, C_WIDTH), F32),
        scratch_shapes=[pltpu.VMEM((tq, seq), F32), pltpu.VMEM((tq, seq), F32),
                        pltpu.VMEM((tq, LANES), F32), pltpu.VMEM((tq, LANES), F32)],
        compiler_params=_params(("parallel", "parallel")),
        name="mla_attn",
    )(q, k, v)

    out = pl.pallas_call(
        _odd_out_kernel,
        grid=(n_tok // tm,),
        in_specs=[row_spec(C_WIDTH), row_spec(C_WIDTH), row_spec(D_WIDTH), row_spec(D_MODEL),
                  pl.BlockSpec((None, tm, PLE_DIM), lambda t: (layer, t, 0)),
                  _full((1, C_WIDTH)), _full((D_MODEL, D_MODEL)), _full((1, D_MODEL)),
                  _full((1, D_MODEL)), _full((D_MODEL, D_MODEL)), _full((PLE_DIM, D_MODEL))],
        out_specs=row_spec(D_MODEL),
        out_shape=jax.ShapeDtypeStruct((n_tok, D_MODEL), F32),
        compiler_params=_params(("parallel",)),
        name="odd_out",
    )(yc.reshape(n_tok, C_WIDTH), szc, yd, x.reshape(n_tok, D_MODEL),
      p.reshape(DEPTH, n_tok, PLE_DIM), c_norm.reshape(1, -1), w_out.astype(BF16),
      ln_g.reshape(1, -1), ln_b.reshape(1, -1), ple_gate.astype(BF16), ple_proj.astype(BF16))
    return out.reshape(bsz, seq, D_MODEL)


def kernel(x, p, positions, ev_w_in, ev_conv_w, ev_sink, ev_a_norm, ev_b_norm, ev_w_out, od_w_in, od_q_norm, od_w_uq, od_kv_norm, od_w_ukv, od_v_ln_g, od_v_ln_b, od_w_s, od_b_s, od_c_norm, od_d_norm, od_w_out, post_ln_g, post_ln_b, ple_proj, ple_gate):
    for i in range(DEPTH):
        j = i // 2
        if i % 2 == 0:
            x = _even_layer(x, p, i, positions, ev_w_in[j], ev_conv_w[j], ev_sink[j],
                            ev_a_norm[j], ev_b_norm[j], ev_w_out[j],
                            post_ln_g[i], post_ln_b[i], ple_proj[i], ple_gate[i])
        else:
            x = _odd_layer(x, p, i, positions, od_w_in[j], od_q_norm[j], od_w_uq[j],
                           od_kv_norm[j], od_w_ukv[j], od_v_ln_g[j], od_v_ln_b[j],
                           od_w_s[j], od_b_s[j], od_c_norm[j], od_d_norm[j], od_w_out[j],
                           post_ln_g[i], post_ln_b[i], ple_proj[i], ple_gate[i])
    return x
```

```python
import functools
import math

import jax
import jax.numpy as jnp
from jax import lax
from jax.experimental import pallas as pl
from jax.experimental.pallas import tpu as pltpu

F32 = jnp.float32
BF16 = jnp.bfloat16

D_MODEL = 1024
DEPTH = 2
PLE_DIM = 256
DEEPNORM_ALPHA = (2.0 * DEPTH) ** 0.25
BLOCK = 128
EPS = 1e-6
A_HEAD_DIM = 64
A_HEADS = 8
A_KV_HEADS = 2
A_GROUP = A_HEADS // A_KV_HEADS
A_WIDTH = A_HEADS * A_HEAD_DIM
WINDOW = 128
B_WIDTH = 512
C_HEADS = 8
C_NOPE = 64
C_ROPE = 32
C_V = 64
C_WIDTH = C_HEADS * C_V
C_Q_RANK = 256
C_KV_RANK = 128
ROPE_THETA = 10000.0
D_WIDTH = 512
D_GROUPS = 4
D_GROUP_DIM = D_WIDTH // D_GROUPS
D_CHUNK = 128

LANES = 128
SLOT = LANES
VMEM_LIMIT = 56 * 1024 * 1024

_EV_OFF = (0, 512, 640, 768, 1280, 1792, 2304, 3328)
_OD_OFF = (0, 384, 640, 1152, 1664, 2688)

A_SLOPES = tuple(2.0 ** (-8.0 * (h + 1) / A_HEADS) for h in range(A_HEADS))
LOG2E = math.log2(math.e)
A_SCALE_LOG2E = A_HEAD_DIM ** -0.5 * LOG2E
MLA_SCALE_LOG2E = (C_NOPE + C_ROPE) ** -0.5 * LOG2E


def _dot(a, b):
    return jnp.dot(a, b, preferred_element_type=F32)


def _dot_nt(a, b):
    return lax.dot_general(a, b, (((1,), (1,)), ((), ())), preferred_element_type=F32)


def _rms(x, g):
    return x * lax.rsqrt(jnp.mean(x * x, axis=-1, keepdims=True) + EPS) * g


def _layer_norm(x, g, b):
    mu = jnp.mean(x, axis=-1, keepdims=True)
    xc = x - mu
    var = jnp.mean(xc * xc, axis=-1, keepdims=True)
    return xc * lax.rsqrt(var + EPS) * g + b


def _silu(z):
    return z * jax.nn.sigmoid(z)


def _gelu_tanh(x):
    c = math.sqrt(2.0 / math.pi)
    return 0.5 * x * (1.0 + jnp.tanh(c * (x + 0.044715 * (x * x * x))))


def _post_block(y_bf16_dot, x, p, g_ref, b_ref, gate_ref, proj_ref):
    h = _layer_norm(DEEPNORM_ALPHA * x + y_bf16_dot, g_ref[...], b_ref[...])
    gate = jax.nn.sigmoid(_dot(h.astype(BF16), gate_ref[...]))
    return h + gate * _dot(p.astype(BF16), proj_ref[...])


def _even_kernel(seq_len, tile,
                 xc_ref, xp_ref, p_ref, posc_ref, prc_ref, prp_ref, prn_ref,
                 win_ref, convw_ref, sink_ref, anorm_ref, bnorm_ref, wout_ref,
                 g_ref, b_ref, gate_ref, proj_ref,
                 o_ref,
                 kbuf, vbuf, qbuf, bgbuf, szbuf, czbuf, posbuf, ybuf, lg_a, lg_b, m_a, m_b):
    t = pl.program_id(0)
    tiles_per_seq = seq_len // tile
    sub = tile // BLOCK

    @pl.when(t == 0)
    def _():
        for ref in (kbuf, vbuf, qbuf, bgbuf, szbuf, czbuf):
            ref[...] = jnp.zeros(ref.shape, ref.dtype)

    xb = xc_ref[...].astype(BF16)

    def seg(n):
        return _dot(xb, win_ref[:, _EV_OFF[n]:_EV_OFF[n + 1]])

    slot_p = pl.multiple_of((t % 2) * tile, tile)
    ring_p = pl.multiple_of((t % 3) * tile, tile)
    krow_p = pl.multiple_of(BLOCK + (t % tiles_per_seq) * tile, BLOCK)

    lane_t = lax.broadcasted_iota(jnp.int32, (tile, LANES), 1)
    low = lane_t < A_HEAD_DIM
    one = jnp.where(lane_t == A_HEAD_DIM, 1.0, 0.0)

    def slots(v, fill):
        return (jnp.where(low, v, fill),
                jnp.where(low, pltpu.roll(v, A_HEAD_DIM, axis=1), fill))

    q = seg(0) * A_SCALE_LOG2E
    for n in range(A_WIDTH // LANES):
        lo, hi = slots(q[:, n * LANES:(n + 1) * LANES], 0.0)
        qbuf[pl.ds(slot_p, tile), (2 * n) * SLOT:(2 * n + 1) * SLOT] = lo.astype(BF16)
        qbuf[pl.ds(slot_p, tile), (2 * n + 1) * SLOT:(2 * n + 2) * SLOT] = hi.astype(BF16)
    lo, hi = slots(seg(1), 0.0)
    kbuf[pl.ds(krow_p, tile), 0:SLOT] = lo.astype(BF16)
    kbuf[pl.ds(krow_p, tile), SLOT:] = hi.astype(BF16)
    lo, hi = slots(seg(2), one)
    vbuf[pl.ds(krow_p, tile), 0:SLOT] = lo.astype(BF16)
    vbuf[pl.ds(krow_p, tile), SLOT:] = hi.astype(BF16)
    bgbuf[pl.ds(slot_p, tile), :] = seg(3)
    czbuf[pl.ds(ring_p, tile), :] = seg(4) * seg(5)
    szbuf[pl.ds(slot_p, tile), :] = _silu(seg(6))

    i = jnp.maximum(t - 1, 0) % tiles_per_seq
    slot_m = pl.multiple_of(((t + 1) % 2) * tile, tile)
    ring_m = pl.multiple_of(((t + 2) % 3) * tile, tile)
    ring_before = pl.multiple_of(((t + 1) % 3) * tile, tile)
    krow_m = pl.multiple_of(i * tile, BLOCK)

    posbuf[:, 0:BLOCK] = prp_ref[...]
    posbuf[:, BLOCK:BLOCK + tile] = prc_ref[...]
    posbuf[:, BLOCK + tile:] = prn_ref[...]

    q_idx = lax.broadcasted_iota(jnp.int32, (BLOCK, 3 * BLOCK), 0)
    w_idx = lax.broadcasted_iota(jnp.int32, (BLOCK, 3 * BLOCK), 1)
    in_band = jnp.abs(w_idx - BLOCK - q_idx) <= WINDOW
    lane = lax.broadcasted_iota(jnp.int32, (BLOCK, LANES), 1)
    dists = {}

    def masked_dist(j):
        if j not in dists:
            k_glob = (i * sub + j) * BLOCK + w_idx - BLOCK
            valid = in_band & (k_glob >= 0) & (k_glob < seq_len)
            d = jnp.abs(posc_ref[j * BLOCK:(j + 1) * BLOCK, :]
                        - posbuf[:, j * BLOCK:(j + 3) * BLOCK]).astype(F32)
            dists[j] = jnp.where(valid, d, jnp.inf)
        return dists[j]

    def scores(j, g, lg_ref, m_ref):
        qrow = pl.ds(slot_m + j * BLOCK, BLOCK)
        qg = jnp.concatenate([qbuf[qrow, (A_GROUP * g + hh) * SLOT:(A_GROUP * g + hh + 1) * SLOT]
                              for hh in range(A_GROUP)], axis=0)
        s = _dot_nt(qg, kbuf[pl.ds(krow_m + j * BLOCK, 3 * BLOCK), g * SLOT:(g + 1) * SLOT])
        d = masked_dist(j)
        for hh in range(A_GROUP):
            h = A_GROUP * g + hh
            r = slice(hh * BLOCK, (hh + 1) * BLOCK)
            lg = s[r, :] - (A_SLOPES[h] * LOG2E) * d
            lg_ref[r, :] = lg
            mx = jnp.maximum(jnp.max(lg, axis=-1, keepdims=True), sink_ref[h])
            m_ref[r, :] = jnp.broadcast_to(mx, (BLOCK, LANES))

    def attend(j, g, lg_ref, m_ref):
        rows = slice(j * BLOCK, (j + 1) * BLOCK)
        m = m_ref[...]
        e = jnp.concatenate([jnp.exp2(lg_ref[:, n * LANES:(n + 1) * LANES] - m).astype(BF16)
                             for n in range(3)], axis=1)
        pv = _dot(e, vbuf[pl.ds(krow_m + j * BLOCK, 3 * BLOCK), g * SLOT:(g + 1) * SLOT])
        outs = []
        for hh in range(A_GROUP):
            r = slice(hh * BLOCK, (hh + 1) * BLOCK)
            denom = (jnp.broadcast_to(pv[r, A_HEAD_DIM:A_HEAD_DIM + 1], (BLOCK, LANES))
                     + jnp.exp2(sink_ref[A_GROUP * g + hh] - m[r, :]))
            outs.append(pv[r, :] / denom)
        for pr in range(A_GROUP // 2):
            c0 = (A_GROUP // 2 * g + pr) * LANES
            ybuf[rows, c0:c0 + LANES] = jnp.where(
                lane < A_HEAD_DIM, outs[2 * pr], pltpu.roll(outs[2 * pr + 1], A_HEAD_DIM, axis=1))

    chains = [(j, g) for j in range(sub) for g in range(A_KV_HEADS)]
    bufs = [(lg_a, m_a), (lg_b, m_b)]
    scores(*chains[0], *bufs[0])
    for c, (j, g) in enumerate(chains):
        if c + 1 < len(chains):
            scores(*chains[c + 1], *bufs[(c + 1) % 2])
        attend(j, g, *bufs[c % 2])

    cz = czbuf[pl.ds(ring_m, tile), :]
    row = lax.broadcasted_iota(jnp.int32, cz.shape, 0)
    prev_row = jnp.where(i > 0, czbuf[pl.ds(ring_before + tile - 8, 8), :][7:8, :], 0.0)
    next_row = jnp.where(i < tiles_per_seq - 1, czbuf[pl.ds(ring_p, 8), :][0:1, :], 0.0)
    z_prev = jnp.where(row == 0, prev_row, pltpu.roll(cz, 1, axis=0))
    z_next = jnp.where(row == tile - 1, next_row, pltpu.roll(cz, tile - 1, axis=0))
    conv = (convw_ref[0:1, :] * z_prev + convw_ref[1:2, :] * cz
            + convw_ref[2:3, :] * z_next)
    yb = bgbuf[pl.ds(slot_m, tile), :] * conv

    ya_n = _rms(ybuf[...], anorm_ref[...]) * szbuf[pl.ds(slot_m, tile), 0:A_WIDTH]
    yb_n = _rms(yb, bnorm_ref[...]) * szbuf[pl.ds(slot_m, tile), A_WIDTH:]
    mix = (_dot(ya_n.astype(BF16), wout_ref[0:A_WIDTH, :])
           + _dot(yb_n.astype(BF16), wout_ref[A_WIDTH:, :]))
    o_ref[...] = _post_block(mix, xp_ref[...], p_ref[...], g_ref, b_ref, gate_ref, proj_ref)


def _odd_proj_kernel(x_ref, pos_ref, w_ref, inv_ref, qn_ref, wq_ref, wqs_ref,
                     kvn_ref, wkn_ref, wke_ref, wv_ref, vones_ref,
                     lng_ref, lnb_ref, ws_ref, bs_ref, dnorm_ref,
                     q_ref, k_ref, v_ref, yd_ref, szc_ref):
    tile = x_ref.shape[0]
    xb = x_ref[...].astype(BF16)

    def seg(i):
        return _dot(xb, w_ref[:, _OD_OFF[i]:_OD_OFF[i + 1]])

    ang = pos_ref[...].astype(F32) * inv_ref[...]
    cos_t = jnp.cos(ang)
    sin_t = jnp.sin(ang)

    a = seg(0)
    cqn = _rms(a[:, 0:C_Q_RANK], qn_ref[...]).astype(BF16)
    ckvn = _rms(a[:, C_Q_RANK:], kvn_ref[...]).astype(BF16)

    qa = _dot(cqn, wq_ref[...])
    qs = _dot(cqn, wqs_ref[...])
    for h in range(C_HEADS):
        sl = slice(h * SLOT, (h + 1) * SLOT)
        q_ref[:, sl] = ((qa[:, sl] * cos_t + qs[:, sl] * sin_t) * MLA_SCALE_LOG2E).astype(BF16)

    kr = seg(1)
    kr_rot = (kr[:, 0:LANES] * cos_t + kr[:, LANES:] * sin_t).astype(BF16)
    k_ref[...] = (_dot(ckvn, wkn_ref[...]) + _dot(kr_rot, wke_ref[...])).astype(BF16)
    v_ref[...] = (_dot(ckvn, wv_ref[...]) + vones_ref[...]).astype(BF16)

    gu = _gelu_tanh(seg(2))
    vn = _layer_norm(_gelu_tanh(seg(3)), lng_ref[...], lnb_ref[...]).astype(BF16)
    chunks = []
    for c in range(tile // D_CHUNK):
        rows = slice(c * D_CHUNK, (c + 1) * D_CHUNK)
        groups = [_dot(ws_ref[g], vn[rows, g * D_GROUP_DIM:(g + 1) * D_GROUP_DIM])
                  for g in range(D_GROUPS)]
        chunks.append(jnp.concatenate(groups, axis=1) + bs_ref[...])
    yd = gu * jnp.concatenate(chunks, axis=0)
    sz = _silu(seg(4))
    yd_ref[...] = (_rms(yd, dnorm_ref[...]) * sz[:, C_WIDTH:]).astype(BF16)
    szc_ref[...] = sz[:, 0:C_WIDTH]


def _mla_kernel(seq_len, tq, tk, q_ref, k_ref, v_ref, o_ref, s_a, s_b, m_a, m_b, p_ref):
    n_qt = seq_len // tq
    n_c = seq_len // tk

    def scores(qt, hh, s_ref, m_ref):
        sl = slice(hh * SLOT, (hh + 1) * SLOT)
        q = q_ref[pl.ds(pl.multiple_of(qt * tq, tq), tq), sl]
        part = None
        for c in range(n_c):
            s = _dot_nt(q, k_ref[c * tk:(c + 1) * tk, sl])
            s_ref[:, c * tk:(c + 1) * tk] = s
            for j in range(tk // LANES):
                t = s[:, j * LANES:(j + 1) * LANES]
                part = t if part is None else jnp.maximum(part, t)
        m_ref[...] = jnp.broadcast_to(jnp.max(part, axis=-1, keepdims=True), m_ref.shape)

    def attend(hh, s_ref, m_ref):
        m = m_ref[...]
        for j in range(seq_len // LANES):
            cols = slice(j * LANES, (j + 1) * LANES)
            p_ref[:, cols] = jnp.exp2(s_ref[:, cols] - m).astype(BF16)
        acc = _dot(p_ref[...], v_ref[:, hh * SLOT:(hh + 1) * SLOT])
        return acc / acc[:, C_V:C_V + 1]

    scores(0, 0, s_a, m_a)

    def body(qt, carry):
        scores(qt, 1, s_b, m_b)
        out0 = attend(0, s_a, m_a)
        scores(jnp.minimum(qt + 1, n_qt - 1), 0, s_a, m_a)
        out1 = attend(1, s_b, m_b)
        lane = lax.broadcasted_iota(jnp.int32, out0.shape, 1)
        o_ref[pl.ds(pl.multiple_of(qt * tq, tq), tq), :] = jnp.where(
            lane < C_V, out0, pltpu.roll(out1, C_V, axis=1))
        return carry

    lax.fori_loop(0, n_qt, body, 0)


def _odd_out_kernel(yc_ref, szc_ref, yd_ref, x_ref, p_ref, cnorm_ref, wout_ref,
                    g_ref, b_ref, gate_ref, proj_ref, o_ref):
    yc_n = _rms(yc_ref[...], cnorm_ref[...]) * szc_ref[...]
    mix = (_dot(yc_n.astype(BF16), wout_ref[0:C_WIDTH, :])
           + _dot(yd_ref[...], wout_ref[C_WIDTH:, :]))
    o_ref[...] = _post_block(mix, x_ref[...], p_ref[...], g_ref, b_ref, gate_ref, proj_ref)


def _full(shape):
    return pl.BlockSpec(shape, lambda *_: (0,) * len(shape))


def _once(shape):
    return pl.BlockSpec(shape, lambda *_: (0,) * len(shape), pipeline_mode=pl.Buffered(1))


def _params(sem):
    return pltpu.CompilerParams(dimension_semantics=sem, vmem_limit_bytes=VMEM_LIMIT)


def _even_layer(x, p, layer, positions, w_in, conv_w, sink, a_norm, b_norm, w_out,
                ln_g, ln_b, ple_proj, ple_gate):
    bsz, seq, _ = x.shape
    n_tok = bsz * seq
    tile = 512
    n_tiles = n_tok // tile
    tiles_per_seq = seq // tile
    sub = tile // BLOCK
    nb = seq // BLOCK
    q_w = A_HEADS * SLOT
    kv_w = A_KV_HEADS * SLOT
    x2 = x.reshape(n_tok, D_MODEL)
    pos_col = positions.reshape(n_tok, 1)
    pos_row = positions.reshape(bsz, 1, seq)

    cur = lambda t: jnp.minimum(t, n_tiles - 1)
    prv = lambda t: jnp.maximum(t - 1, 0)
    seq_of = lambda t: prv(t) // tiles_per_seq
    blk_of = lambda t: (prv(t) % tiles_per_seq) * sub
    in_specs = [
        pl.BlockSpec((tile, D_MODEL), lambda t: (cur(t), 0)),
        pl.BlockSpec((tile, D_MODEL), lambda t: (prv(t), 0)),
        pl.BlockSpec((None, tile, PLE_DIM), lambda t: (layer, prv(t), 0)),
        pl.BlockSpec((tile, 1), lambda t: (prv(t), 0)),
        pl.BlockSpec((None, 1, tile), lambda t: (seq_of(t), 0, prv(t) % tiles_per_seq)),
        pl.BlockSpec((None, 1, BLOCK), lambda t: (seq_of(t), 0, jnp.maximum(blk_of(t) - 1, 0))),
        pl.BlockSpec((None, 1, BLOCK), lambda t: (seq_of(t), 0, jnp.minimum(blk_of(t) + sub, nb - 1))),
        _once((D_MODEL, _EV_OFF[-1])),
        _once((3, B_WIDTH)),
        pl.BlockSpec(memory_space=pltpu.SMEM),
        _once((1, A_WIDTH)), _once((1, B_WIDTH)), _once((D_MODEL, D_MODEL)),
        _once((1, D_MODEL)), _once((1, D_MODEL)), _once((D_MODEL, D_MODEL)),
        _once((PLE_DIM, D_MODEL)),
    ]
    out = pl.pallas_call(
        functools.partial(_even_kernel, seq, tile),
        grid=(n_tiles + 1,),
        in_specs=in_specs,
        out_specs=pl.BlockSpec((tile, D_MODEL), lambda t: (prv(t), 0)),
        out_shape=jax.ShapeDtypeStruct((n_tok, D_MODEL), F32),
        scratch_shapes=[pltpu.VMEM((seq + 2 * BLOCK, kv_w), BF16),
                        pltpu.VMEM((seq + 2 * BLOCK, kv_w), BF16),
                        pltpu.VMEM((2 * tile, q_w), BF16),
                        pltpu.VMEM((2 * tile, B_WIDTH), F32),
                        pltpu.VMEM((2 * tile, A_WIDTH + B_WIDTH), F32),
                        pltpu.VMEM((3 * tile, B_WIDTH), F32),
                        pltpu.VMEM((1, tile + 2 * BLOCK), jnp.int32),
                        pltpu.VMEM((tile, A_WIDTH), F32),
                        pltpu.VMEM((A_GROUP * BLOCK, 3 * BLOCK), F32),
                        pltpu.VMEM((A_GROUP * BLOCK, 3 * BLOCK), F32),
                        pltpu.VMEM((A_GROUP * BLOCK, LANES), F32),
                        pltpu.VMEM((A_GROUP * BLOCK, LANES), F32)],
        compiler_params=_params(("arbitrary",)),
        name="even_layer",
    )(x2, x2, p.reshape(DEPTH, n_tok, PLE_DIM), pos_col, pos_row, pos_row, pos_row,
      w_in.astype(BF16), conv_w, sink * LOG2E, a_norm.reshape(1, -1), b_norm.reshape(1, -1),
      w_out.astype(BF16), ln_g.reshape(1, -1), ln_b.reshape(1, -1),
      ple_gate.astype(BF16), ple_proj.astype(BF16))
    return out.reshape(bsz, seq, D_MODEL)


def _swap_halves(r):
    half = r.shape[-1] // 2
    return jnp.concatenate([-r[..., half:], r[..., :half]], axis=-1)


def _odd_weights(w_in, w_uq, w_ukv, b_s):
    d = w_in.shape[0]
    o_kr = C_Q_RANK + C_KV_RANK
    o_du = o_kr + C_ROPE
    w_kr = w_in[:, o_kr:o_du]
    kr_pair = jnp.zeros((d, 2 * LANES), F32)
    kr_pair = kr_pair.at[:, C_NOPE:C_NOPE + C_ROPE].set(w_kr)
    kr_pair = kr_pair.at[:, LANES + C_NOPE:LANES + C_NOPE + C_ROPE].set(_swap_halves(w_kr))
    w1 = jnp.concatenate([w_in[:, :o_kr], kr_pair, w_in[:, o_du:]], axis=1).astype(BF16)

    uq = w_uq.reshape(C_Q_RANK, C_HEADS, C_NOPE + C_ROPE)
    pad = SLOT - C_NOPE - C_ROPE
    wq = jnp.pad(uq, ((0, 0), (0, 0), (0, pad)))
    wqs = jnp.pad(_swap_halves(uq[..., C_NOPE:]), ((0, 0), (0, 0), (C_NOPE, pad)))
    ukv = w_ukv.reshape(C_KV_RANK, C_HEADS, C_NOPE + C_V)
    wkn = jnp.pad(ukv[..., :C_NOPE], ((0, 0), (0, 0), (0, SLOT - C_NOPE)))
    wv = jnp.pad(ukv[..., C_NOPE:], ((0, 0), (0, 0), (0, SLOT - C_V)))
    flat = lambda w: w.reshape(w.shape[0], C_HEADS * SLOT).astype(BF16)

    lane = jnp.arange(LANES)
    rope_lane = (lane >= C_NOPE) & (lane < C_NOPE + C_ROPE)
    wke = jnp.tile(jnp.where(rope_lane[:, None] & (lane[:, None] == lane[None, :]), 1.0, 0.0),
                   (1, C_HEADS)).astype(BF16)
    vones = jnp.tile(jnp.where(lane == C_V, 1.0, 0.0), C_HEADS).reshape(1, -1).astype(F32)
    half = C_ROPE // 2
    inv = ROPE_THETA ** (-jnp.arange(half, dtype=F32) / half)
    inv_lane = jnp.where(rope_lane, inv[(lane - C_NOPE) % half], 0.0).reshape(1, LANES).astype(F32)
    bs_full = jnp.repeat(b_s.T, D_GROUP_DIM, axis=1)
    return w1, flat(wq), flat(wqs), flat(wkn), wke, flat(wv), vones, inv_lane, bs_full


def _odd_layer(x, p, layer, positions, w_in, q_norm, w_uq, kv_norm, w_ukv, v_ln_g, v_ln_b,
               w_s, b_s, c_norm, d_norm, w_out, ln_g, ln_b, ple_proj, ple_gate):
    bsz, seq, _ = x.shape
    n_tok = bsz * seq
    w1, wq, wqs, wkn, wke, wv, vones, inv_lane, bs_full = _odd_weights(w_in, w_uq, w_ukv, b_s)
    tm = 512
    row_spec = lambda w: pl.BlockSpec((tm, w), lambda t: (t, 0))
    wide = C_HEADS * SLOT
    q, k, v, yd, szc = pl.pallas_call(
        _odd_proj_kernel,
        grid=(n_tok // tm,),
        in_specs=[row_spec(D_MODEL), row_spec(1), _full(w1.shape), _full((1, LANES)),
                  _full((1, C_Q_RANK)), _full(wq.shape), _full(wqs.shape),
                  _full((1, C_KV_RANK)), _full(wkn.shape), _full(wke.shape), _full(wv.shape),
                  _full((1, wide)), _full((1, D_WIDTH)), _full((1, D_WIDTH)),
                  _full((D_GROUPS, D_CHUNK, D_CHUNK)), _full((D_CHUNK, D_WIDTH)),
                  _full((1, D_WIDTH))],
        out_specs=[row_spec(wide), row_spec(wide), row_spec(wide), row_spec(D_WIDTH),
                   row_spec(C_WIDTH)],
        out_shape=[jax.ShapeDtypeStruct((n_tok, wide), BF16),
                   jax.ShapeDtypeStruct((n_tok, wide), BF16),
                   jax.ShapeDtypeStruct((n_tok, wide), BF16),
                   jax.ShapeDtypeStruct((n_tok, D_WIDTH), BF16),
                   jax.ShapeDtypeStruct((n_tok, C_WIDTH), F32)],
        compiler_params=_params(("parallel",)),
        name="odd_proj",
    )(x.reshape(n_tok, D_MODEL), positions.reshape(n_tok, 1), w1, inv_lane,
      q_norm.reshape(1, -1), wq, wqs, kv_norm.reshape(1, -1), wkn, wke, wv, vones,
      v_ln_g.reshape(1, -1), v_ln_b.reshape(1, -1), w_s.astype(BF16), bs_full,
      d_norm.reshape(1, -1))

    tq = 256
    tk = 512
    q = q.reshape(bsz, seq, wide)
    k = k.reshape(bsz, seq, wide)
    v = v.reshape(bsz, seq, wide)
    pair = lambda w: pl.BlockSpec((None, seq, w), lambda b, h: (b, 0, h))
    yc = pl.pallas_call(
        functools.partial(_mla_kernel, seq, tq, tk),
        grid=(bsz, C_HEADS // 2),
        in_specs=[pair(2 * SLOT), pair(2 * SLOT), pair(2 * SLOT)],
        out_specs=pair(2 * C_V),
        out_shape=jax.ShapeDtypeStruct((bsz, seq, C_WIDTH), F32),
        scratch_shapes=[pltpu.VMEM((tq, seq), F32), pltpu.VMEM((tq, seq), F32),
                        pltpu.VMEM((tq, LANES), F32), pltpu.VMEM((tq, LANES), F32),
                        pltpu.VMEM((tq, seq), BF16)],
        compiler_params=_params(("parallel", "parallel")),
        name="mla_attn",
    )(q, k, v)

    out = pl.pallas_call(
        _odd_out_kernel,
        grid=(n_tok // tm,),
        in_specs=[row_spec(C_WIDTH), row_spec(C_WIDTH), row_spec(D_WIDTH), row_spec(D_MODEL),
                  pl.BlockSpec((None, tm, PLE_DIM), lambda t: (layer, t, 0)),
                  _full((1, C_WIDTH)), _full((D_MODEL, D_MODEL)), _full((1, D_MODEL)),
                  _full((1, D_MODEL)), _full((D_MODEL, D_MODEL)), _full((PLE_DIM, D_MODEL))],
        out_specs=row_spec(D_MODEL),
        out_shape=jax.ShapeDtypeStruct((n_tok, D_MODEL), F32),
        compiler_params=_params(("parallel",)),
        name="odd_out",
    )(yc.reshape(n_tok, C_WIDTH), szc, yd, x.reshape(n_tok, D_MODEL),
      p.reshape(DEPTH, n_tok, PLE_DIM), c_norm.reshape(1, -1), w_out.astype(BF16),
      ln_g.reshape(1, -1), ln_b.reshape(1, -1), ple_gate.astype(BF16), ple_proj.astype(BF16))
    return out.reshape(bsz, seq, D_MODEL)


def kernel(x, p, positions, ev_w_in, ev_conv_w, ev_sink, ev_a_norm, ev_b_norm, ev_w_out, od_w_in, od_q_norm, od_w_uq, od_kv_norm, od_w_ukv, od_v_ln_g, od_v_ln_b, od_w_s, od_b_s, od_c_norm, od_d_norm, od_w_out, post_ln_g, post_ln_b, ple_proj, ple_gate):
    for i in range(DEPTH):
        j = i // 2
        if i % 2 == 0:
            x = _even_layer(x, p, i, positions, ev_w_in[j], ev_conv_w[j], ev_sink[j],
                            ev_a_norm[j], ev_b_norm[j], ev_w_out[j],
                            post_ln_g[i], post_ln_b[i], ple_proj[i], ple_gate[i])
        else:
            x = _odd_layer(x, p, i, positions, od_w_in[j], od_q_norm[j], od_w_uq[j],
                           od_kv_norm[j], od_w_ukv[j], od_v_ln_g[j], od_v_ln_b[j],
                           od_w_s[j], od_b_s[j], od_c_norm[j], od_d_norm[j], od_w_out[j],
                           post_ln_g[i], post_ln_b[i], ple_proj[i], ple_gate[i])
    return x
```

```python
import functools
import math

import jax
import jax.numpy as jnp
from jax import lax
from jax.experimental import pallas as pl
from jax.experimental.pallas import tpu as pltpu

F32 = jnp.float32
BF16 = jnp.bfloat16

D_MODEL = 1024
DEPTH = 2
PLE_DIM = 256
DEEPNORM_ALPHA = (2.0 * DEPTH) ** 0.25
BLOCK = 128
EPS = 1e-6
A_HEAD_DIM = 64
A_HEADS = 8
A_KV_HEADS = 2
A_GROUP = A_HEADS // A_KV_HEADS
A_WIDTH = A_HEADS * A_HEAD_DIM
WINDOW = 128
B_WIDTH = 512
C_HEADS = 8
C_NOPE = 64
C_ROPE = 32
C_V = 64
C_WIDTH = C_HEADS * C_V
C_Q_RANK = 256
C_KV_RANK = 128
ROPE_THETA = 10000.0
D_WIDTH = 512
D_GROUPS = 4
D_GROUP_DIM = D_WIDTH // D_GROUPS
D_CHUNK = 128

LANES = 128
SLOT = LANES
VMEM_LIMIT = 56 * 1024 * 1024

_EV_OFF = (0, 512, 640, 768, 1280, 1792, 2304, 3328)
_OD_OFF = (0, 384, 640, 1152, 1664, 2688)

A_SLOPES = tuple(2.0 ** (-8.0 * (h + 1) / A_HEADS) for h in range(A_HEADS))
LOG2E = math.log2(math.e)
A_SCALE_LOG2E = A_HEAD_DIM ** -0.5 * LOG2E
MLA_SCALE_LOG2E = (C_NOPE + C_ROPE) ** -0.5 * LOG2E


def _dot(a, b):
    return jnp.dot(a, b, preferred_element_type=F32)


def _dot_nt(a, b):
    return lax.dot_general(a, b, (((1,), (1,)), ((), ())), preferred_element_type=F32)


def _rms(x, g):
    return x * lax.rsqrt(jnp.mean(x * x, axis=-1, keepdims=True) + EPS) * g


def _layer_norm(x, g, b):
    mu = jnp.mean(x, axis=-1, keepdims=True)
    xc = x - mu
    var = jnp.mean(xc * xc, axis=-1, keepdims=True)
    return xc * lax.rsqrt(var + EPS) * g + b


def _silu(z):
    return z * jax.nn.sigmoid(z)


def _gelu_tanh(x):
    c = math.sqrt(2.0 / math.pi)
    return 0.5 * x * (1.0 + jnp.tanh(c * (x + 0.044715 * (x * x * x))))


def _post_block(y_bf16_dot, x, p, g_ref, b_ref, gate_ref, proj_ref):
    h = _layer_norm(DEEPNORM_ALPHA * x + y_bf16_dot, g_ref[...], b_ref[...])
    gate = jax.nn.sigmoid(_dot(h.astype(BF16), gate_ref[...]))
    return h + gate * _dot(p.astype(BF16), proj_ref[...])


def _even_kernel(seq_len, tile,
                 xc_ref, xp_ref, p_ref, posc_ref, prc_ref, prp_ref, prn_ref,
                 win_ref, convw_ref, sink_ref, anorm_ref, bnorm_ref, wout_ref,
                 g_ref, b_ref, gate_ref, proj_ref,
                 o_ref,
                 kbuf, vbuf, q_a, q_b, bg_a, bg_b, sz_a, sz_b, cz_a, cz_b,
                 posbuf, ybuf, lg_a, lg_b, m_a, m_b):
    t = pl.program_id(0)

    @pl.when(t == 0)
    def _():
        for ref in (kbuf, vbuf, q_a, q_b, bg_a, bg_b, sz_a, sz_b, cz_a, cz_b):
            ref[...] = jnp.zeros(ref.shape, ref.dtype)

    shared = (seq_len, tile, xc_ref, xp_ref, p_ref, posc_ref, prc_ref, prp_ref, prn_ref,
              win_ref, convw_ref, sink_ref, anorm_ref, bnorm_ref, wout_ref,
              g_ref, b_ref, gate_ref, proj_ref, o_ref,
              kbuf, vbuf, posbuf, ybuf, lg_a, lg_b, m_a, m_b)

    @pl.when(t % 2 == 0)
    def _():
        _even_step(*shared, (q_a, bg_a, sz_a, cz_a), (q_b, bg_b, sz_b, cz_b))

    @pl.when(t % 2 == 1)
    def _():
        _even_step(*shared, (q_b, bg_b, sz_b, cz_b), (q_a, bg_a, sz_a, cz_a))


def _even_step(seq_len, tile, xc_ref, xp_ref, p_ref, posc_ref, prc_ref, prp_ref, prn_ref,
               win_ref, convw_ref, sink_ref, anorm_ref, bnorm_ref, wout_ref,
               g_ref, b_ref, gate_ref, proj_ref, o_ref,
               kbuf, vbuf, posbuf, ybuf, lg_a, lg_b, m_a, m_b, proj_bufs, mix_bufs):
    t = pl.program_id(0)
    tiles_per_seq = seq_len // tile
    sub = tile // BLOCK
    pq, pbg, psz, pcz = proj_bufs
    mq, mbg, msz, mcz = mix_bufs

    before_rows = pcz[tile - 8:tile, :]

    xb = xc_ref[...].astype(BF16)

    def seg(n, lo=0, hi=None):
        hi = _EV_OFF[n + 1] - _EV_OFF[n] if hi is None else hi
        return _dot(xb, win_ref[:, _EV_OFF[n] + lo:_EV_OFF[n] + hi])

    krow_p = pl.multiple_of(BLOCK + (t % tiles_per_seq) * tile, BLOCK)

    lane_t = lax.broadcasted_iota(jnp.int32, (tile, LANES), 1)
    low = lane_t < A_HEAD_DIM
    one = jnp.where(lane_t == A_HEAD_DIM, 1.0, 0.0)

    def slots(v, fill):
        return (jnp.where(low, v, fill),
                jnp.where(low, pltpu.roll(v, A_HEAD_DIM, axis=1), fill))

    def proj_kv():
        lo, hi = slots(seg(1), 0.0)
        kbuf[pl.ds(krow_p, tile), 0:SLOT] = lo.astype(BF16)
        kbuf[pl.ds(krow_p, tile), SLOT:] = hi.astype(BF16)
        lo, hi = slots(seg(2), one)
        vbuf[pl.ds(krow_p, tile), 0:SLOT] = lo.astype(BF16)
        vbuf[pl.ds(krow_p, tile), SLOT:] = hi.astype(BF16)

    piece_w = 2 * LANES

    def proj_q(part):
        q = seg(0, part * piece_w, (part + 1) * piece_w) * A_SCALE_LOG2E
        for n in range(piece_w // LANES):
            lo, hi = slots(q[:, n * LANES:(n + 1) * LANES], 0.0)
            h0 = 2 * (part * (piece_w // LANES) + n)
            pq[:, h0 * SLOT:(h0 + 1) * SLOT] = lo.astype(BF16)
            pq[:, (h0 + 1) * SLOT:(h0 + 2) * SLOT] = hi.astype(BF16)

    def proj_bg(part):
        lo, hi = part * piece_w, (part + 1) * piece_w
        pbg[:, lo:hi] = seg(3, lo, hi)

    def proj_cg(part):
        lo, hi = part * piece_w, (part + 1) * piece_w
        pcz[:, lo:hi] = seg(4, lo, hi)

    def proj_xin(part):
        lo, hi = part * piece_w, (part + 1) * piece_w
        pcz[:, lo:hi] = pcz[:, lo:hi] * seg(5, lo, hi)

    def proj_sz(part):
        lo, hi = part * piece_w, (part + 1) * piece_w
        psz[:, lo:hi] = _silu(seg(6, lo, hi))

    proj_kv()
    pieces = [functools.partial(fn, part)
              for fn, parts in ((proj_q, 2), (proj_bg, 2), (proj_cg, 2), (proj_xin, 2), (proj_sz, 4))
              for part in range(parts)]

    i = jnp.maximum(t - 1, 0) % tiles_per_seq
    krow_m = pl.multiple_of(i * tile, BLOCK)

    posbuf[:, 0:BLOCK] = prp_ref[...]
    posbuf[:, BLOCK:BLOCK + tile] = prc_ref[...]
    posbuf[:, BLOCK + tile:] = prn_ref[...]

    q_idx = lax.broadcasted_iota(jnp.int32, (BLOCK, 3 * BLOCK), 0)
    w_idx = lax.broadcasted_iota(jnp.int32, (BLOCK, 3 * BLOCK), 1)
    in_band = jnp.abs(w_idx - BLOCK - q_idx) <= WINDOW
    lane = lax.broadcasted_iota(jnp.int32, (BLOCK, LANES), 1)
    dists = {}

    def masked_dist(j):
        if j not in dists:
            k_glob = (i * sub + j) * BLOCK + w_idx - BLOCK
            valid = in_band & (k_glob >= 0) & (k_glob < seq_len)
            d = jnp.abs(posc_ref[j * BLOCK:(j + 1) * BLOCK, :]
                        - posbuf[:, j * BLOCK:(j + 3) * BLOCK]).astype(F32)
            dists[j] = jnp.where(valid, d, jnp.inf)
        return dists[j]

    def scores(j, g, lg_ref, m_ref):
        qrow = slice(j * BLOCK, (j + 1) * BLOCK)
        qg = jnp.concatenate([mq[qrow, (A_GROUP * g + hh) * SLOT:(A_GROUP * g + hh + 1) * SLOT]
                              for hh in range(A_GROUP)], axis=0)
        s = _dot_nt(qg, kbuf[pl.ds(krow_m + j * BLOCK, 3 * BLOCK), g * SLOT:(g + 1) * SLOT])
        d = masked_dist(j)
        for hh in range(A_GROUP):
            h = A_GROUP * g + hh
            r = slice(hh * BLOCK, (hh + 1) * BLOCK)
            lg = s[r, :] - (A_SLOPES[h] * LOG2E) * d
            lg_ref[r, :] = lg
            mx = jnp.maximum(jnp.max(lg, axis=-1, keepdims=True), sink_ref[h])
            m_ref[r, :] = jnp.broadcast_to(mx, (BLOCK, LANES))

    def attend(j, g, lg_ref, m_ref):
        rows = slice(j * BLOCK, (j + 1) * BLOCK)
        m = m_ref[...]
        e = jnp.concatenate([jnp.exp2(lg_ref[:, n * LANES:(n + 1) * LANES] - m).astype(BF16)
                             for n in range(3)], axis=1)
        pv = _dot(e, vbuf[pl.ds(krow_m + j * BLOCK, 3 * BLOCK), g * SLOT:(g + 1) * SLOT])
        outs = []
        for hh in range(A_GROUP):
            r = slice(hh * BLOCK, (hh + 1) * BLOCK)
            denom = (jnp.broadcast_to(pv[r, A_HEAD_DIM:A_HEAD_DIM + 1], (BLOCK, LANES))
                     + jnp.exp2(sink_ref[A_GROUP * g + hh] - m[r, :]))
            outs.append(pv[r, :] / denom)
        for pr in range(A_GROUP // 2):
            c0 = (A_GROUP // 2 * g + pr) * LANES
            ybuf[rows, c0:c0 + LANES] = jnp.where(
                lane < A_HEAD_DIM, outs[2 * pr], pltpu.roll(outs[2 * pr + 1], A_HEAD_DIM, axis=1))

    chains = [(j, g) for j in range(sub) for g in range(A_KV_HEADS)]
    bufs = [(lg_a, m_a), (lg_b, m_b)]
    scores(*chains[0], *bufs[0])
    late = [pieces.pop() for _ in range(4)][::-1]
    for c, (j, g) in enumerate(chains):
        if c + 1 < len(chains):
            scores(*chains[c + 1], *bufs[(c + 1) % 2])
        attend(j, g, *bufs[c % 2])
        pieces.pop(0)()
    assert not pieces

    late[0]()
    cz = mcz[...]
    row = lax.broadcasted_iota(jnp.int32, cz.shape, 0)
    prev_row = jnp.where(i > 0, before_rows[7:8, :], 0.0)
    next_row = jnp.where(i < tiles_per_seq - 1, pcz[0:1, :], 0.0)
    z_prev = jnp.where(row == 0, prev_row, pltpu.roll(cz, 1, axis=0))
    z_next = jnp.where(row == tile - 1, next_row, pltpu.roll(cz, tile - 1, axis=0))
    conv = (convw_ref[0:1, :] * z_prev + convw_ref[1:2, :] * cz
            + convw_ref[2:3, :] * z_next)
    yb = mbg[...] * conv

    late[1]()
    ya_n = _rms(ybuf[...], anorm_ref[...]) * msz[:, 0:A_WIDTH]
    yb_n = _rms(yb, bnorm_ref[...]) * msz[:, A_WIDTH:]
    mix = (_dot(ya_n.astype(BF16), wout_ref[0:A_WIDTH, :])
           + _dot(yb_n.astype(BF16), wout_ref[A_WIDTH:, :]))
    late[2]()
    h = _layer_norm(DEEPNORM_ALPHA * xp_ref[...] + mix, g_ref[...], b_ref[...])
    emb = _dot(p_ref[...].astype(BF16), proj_ref[...])
    late[3]()
    gate = jax.nn.sigmoid(_dot(h.astype(BF16), gate_ref[...]))
    o_ref[...] = h + gate * emb


def _odd_proj_kernel(x_ref, pos_ref, w_ref, inv_ref, qn_ref, wq_ref, wqs_ref,
                     kvn_ref, wkn_ref, wke_ref, wv_ref, vones_ref,
                     lng_ref, lnb_ref, ws_ref, bs_ref, dnorm_ref,
                     q_ref, k_ref, v_ref, yd_ref, szc_ref):
    tile = x_ref.shape[0]
    xb = x_ref[...].astype(BF16)

    def seg(i):
        return _dot(xb, w_ref[:, _OD_OFF[i]:_OD_OFF[i + 1]])

    ang = pos_ref[...].astype(F32) * inv_ref[...]
    cos_t = jnp.cos(ang)
    sin_t = jnp.sin(ang)

    a = seg(0)
    cqn = _rms(a[:, 0:C_Q_RANK], qn_ref[...]).astype(BF16)
    ckvn = _rms(a[:, C_Q_RANK:], kvn_ref[...]).astype(BF16)

    qa = _dot(cqn, wq_ref[...])
    qs = _dot(cqn, wqs_ref[...])
    for h in range(C_HEADS):
        sl = slice(h * SLOT, (h + 1) * SLOT)
        q_ref[:, sl] = ((qa[:, sl] * cos_t + qs[:, sl] * sin_t) * MLA_SCALE_LOG2E).astype(BF16)

    kr = seg(1)
    kr_rot = (kr[:, 0:LANES] * cos_t + kr[:, LANES:] * sin_t).astype(BF16)
    k_ref[...] = (_dot(ckvn, wkn_ref[...]) + _dot(kr_rot, wke_ref[...])).astype(BF16)
    v_ref[...] = (_dot(ckvn, wv_ref[...]) + vones_ref[...]).astype(BF16)

    gu = _gelu_tanh(seg(2))
    vn = _layer_norm(_gelu_tanh(seg(3)), lng_ref[...], lnb_ref[...]).astype(BF16)
    chunks = []
    for c in range(tile // D_CHUNK):
        rows = slice(c * D_CHUNK, (c + 1) * D_CHUNK)
        groups = [_dot(ws_ref[g], vn[rows, g * D_GROUP_DIM:(g + 1) * D_GROUP_DIM])
                  for g in range(D_GROUPS)]
        chunks.append(jnp.concatenate(groups, axis=1) + bs_ref[...])
    yd = gu * jnp.concatenate(chunks, axis=0)
    sz = _silu(seg(4))
    yd_ref[...] = (_rms(yd, dnorm_ref[...]) * sz[:, C_WIDTH:]).astype(BF16)
    szc_ref[...] = sz[:, 0:C_WIDTH]


def _mla_kernel(seq_len, tq, tk, q_ref, k_ref, v_ref, o_ref, s_a, s_b, m_a, m_b):
    n_qt = seq_len // tq
    n_c = seq_len // tk

    def scores(qt, hh, s_ref, m_ref):
        sl = slice(hh * SLOT, (hh + 1) * SLOT)
        q = q_ref[pl.ds(pl.multiple_of(qt * tq, tq), tq), sl]
        part = None
        for c in range(n_c):
            s = _dot_nt(q, k_ref[c * tk:(c + 1) * tk, sl])
            s_ref[:, c * tk:(c + 1) * tk] = s
            for j in range(tk // LANES):
                t = s[:, j * LANES:(j + 1) * LANES]
                part = t if part is None else jnp.maximum(part, t)
        m_ref[...] = jnp.broadcast_to(jnp.max(part, axis=-1, keepdims=True), m_ref.shape)

    def attend(hh, s_ref, m_ref):
        sl = slice(hh * SLOT, (hh + 1) * SLOT)
        m = m_ref[...]
        acc = None
        for c in range(n_c):
            p = jnp.concatenate(
                [jnp.exp2(s_ref[:, c * tk + j * LANES:c * tk + (j + 1) * LANES] - m).astype(BF16)
                 for j in range(tk // LANES)], axis=1)
            d = _dot(p, v_ref[c * tk:(c + 1) * tk, sl])
            acc = d if acc is None else acc + d
        return acc / acc[:, C_V:C_V + 1]

    scores(0, 0, s_a, m_a)

    def body(qt, carry):
        scores(qt, 1, s_b, m_b)
        out0 = attend(0, s_a, m_a)
        scores(jnp.minimum(qt + 1, n_qt - 1), 0, s_a, m_a)
        out1 = attend(1, s_b, m_b)
        lane = lax.broadcasted_iota(jnp.int32, out0.shape, 1)
        o_ref[pl.ds(pl.multiple_of(qt * tq, tq), tq), :] = jnp.where(
            lane < C_V, out0, pltpu.roll(out1, C_V, axis=1))
        return carry

    lax.fori_loop(0, n_qt, body, 0)


def _odd_out_kernel(yc_ref, szc_ref, yd_ref, x_ref, p_ref, cnorm_ref, wout_ref,
                    g_ref, b_ref, gate_ref, proj_ref, o_ref):
    yc_n = _rms(yc_ref[...], cnorm_ref[...]) * szc_ref[...]
    mix = (_dot(yc_n.astype(BF16), wout_ref[0:C_WIDTH, :])
           + _dot(yd_ref[...], wout_ref[C_WIDTH:, :]))
    o_ref[...] = _post_block(mix, x_ref[...], p_ref[...], g_ref, b_ref, gate_ref, proj_ref)


def _full(shape):
    return pl.BlockSpec(shape, lambda *_: (0,) * len(shape))


def _once(shape):
    return pl.BlockSpec(shape, lambda *_: (0,) * len(shape), pipeline_mode=pl.Buffered(1))


def _params(sem):
    return pltpu.CompilerParams(dimension_semantics=sem, vmem_limit_bytes=VMEM_LIMIT)


def _even_layer(x, p, layer, positions, w_in, conv_w, sink, a_norm, b_norm, w_out,
                ln_g, ln_b, ple_proj, ple_gate):
    bsz, seq, _ = x.shape
    n_tok = bsz * seq
    tile = 512
    n_tiles = n_tok // tile
    tiles_per_seq = seq // tile
    sub = tile // BLOCK
    nb = seq // BLOCK
    q_w = A_HEADS * SLOT
    kv_w = A_KV_HEADS * SLOT
    x2 = x.reshape(n_tok, D_MODEL)
    pos_col = positions.reshape(n_tok, 1)
    pos_row = positions.reshape(bsz, 1, seq)

    cur = lambda t: jnp.minimum(t, n_tiles - 1)
    prv = lambda t: jnp.maximum(t - 1, 0)
    seq_of = lambda t: prv(t) // tiles_per_seq
    blk_of = lambda t: (prv(t) % tiles_per_seq) * sub
    in_specs = [
        pl.BlockSpec((tile, D_MODEL), lambda t: (cur(t), 0)),
        pl.BlockSpec((tile, D_MODEL), lambda t: (prv(t), 0)),
        pl.BlockSpec((None, tile, PLE_DIM), lambda t: (layer, prv(t), 0)),
        pl.BlockSpec((tile, 1), lambda t: (prv(t), 0)),
        pl.BlockSpec((None, 1, tile), lambda t: (seq_of(t), 0, prv(t) % tiles_per_seq)),
        pl.BlockSpec((None, 1, BLOCK), lambda t: (seq_of(t), 0, jnp.maximum(blk_of(t) - 1, 0))),
        pl.BlockSpec((None, 1, BLOCK), lambda t: (seq_of(t), 0, jnp.minimum(blk_of(t) + sub, nb - 1))),
        _once((D_MODEL, _EV_OFF[-1])),
        _once((3, B_WIDTH)),
        pl.BlockSpec(memory_space=pltpu.SMEM),
        _once((1, A_WIDTH)), _once((1, B_WIDTH)), _once((D_MODEL, D_MODEL)),
        _once((1, D_MODEL)), _once((1, D_MODEL)), _once((D_MODEL, D_MODEL)),
        _once((PLE_DIM, D_MODEL)),
    ]
    out = pl.pallas_call(
        functools.partial(_even_kernel, seq, tile),
        grid=(n_tiles + 1,),
        in_specs=in_specs,
        out_specs=pl.BlockSpec((tile, D_MODEL), lambda t: (prv(t), 0)),
        out_shape=jax.ShapeDtypeStruct((n_tok, D_MODEL), F32),
        scratch_shapes=[pltpu.VMEM((seq + 2 * BLOCK, kv_w), BF16),
                        pltpu.VMEM((seq + 2 * BLOCK, kv_w), BF16),
                        pltpu.VMEM((tile, q_w), BF16), pltpu.VMEM((tile, q_w), BF16),
                        pltpu.VMEM((tile, B_WIDTH), F32), pltpu.VMEM((tile, B_WIDTH), F32),
                        pltpu.VMEM((tile, A_WIDTH + B_WIDTH), F32),
                        pltpu.VMEM((tile, A_WIDTH + B_WIDTH), F32),
                        pltpu.VMEM((tile, B_WIDTH), F32), pltpu.VMEM((tile, B_WIDTH), F32),
                        pltpu.VMEM((1, tile + 2 * BLOCK), jnp.int32),
                        pltpu.VMEM((tile, A_WIDTH), F32),
                        pltpu.VMEM((A_GROUP * BLOCK, 3 * BLOCK), F32),
                        pltpu.VMEM((A_GROUP * BLOCK, 3 * BLOCK), F32),
                        pltpu.VMEM((A_GROUP * BLOCK, LANES), F32),
                        pltpu.VMEM((A_GROUP * BLOCK, LANES), F32)],
        compiler_params=_params(("arbitrary",)),
        name="even_layer",
    )(x2, x2, p.reshape(DEPTH, n_tok, PLE_DIM), pos_col, pos_row, pos_row, pos_row,
      w_in.astype(BF16), conv_w, sink * LOG2E, a_norm.reshape(1, -1), b_norm.reshape(1, -1),
      w_out.astype(BF16), ln_g.reshape(1, -1), ln_b.reshape(1, -1),
      ple_gate.astype(BF16), ple_proj.astype(BF16))
    return out.reshape(bsz, seq, D_MODEL)


def _swap_halves(r):
    half = r.shape[-1] // 2
    return jnp.concatenate([-r[..., half:], r[..., :half]], axis=-1)


def _odd_weights(w_in, w_uq, w_ukv, b_s):
    d = w_in.shape[0]
    o_kr = C_Q_RANK + C_KV_RANK
    o_du = o_kr + C_ROPE
    w_kr = w_in[:, o_kr:o_du]
    kr_pair = jnp.zeros((d, 2 * LANES), F32)
    kr_pair = kr_pair.at[:, C_NOPE:C_NOPE + C_ROPE].set(w_kr)
    kr_pair = kr_pair.at[:, LANES + C_NOPE:LANES + C_NOPE + C_ROPE].set(_swap_halves(w_kr))
    w1 = jnp.concatenate([w_in[:, :o_kr], kr_pair, w_in[:, o_du:]], axis=1).astype(BF16)

    uq = w_uq.reshape(C_Q_RANK, C_HEADS, C_NOPE + C_ROPE)
    pad = SLOT - C_NOPE - C_ROPE
    wq = jnp.pad(uq, ((0, 0), (0, 0), (0, pad)))
    wqs = jnp.pad(_swap_halves(uq[..., C_NOPE:]), ((0, 0), (0, 0), (C_NOPE, pad)))
    ukv = w_ukv.reshape(C_KV_RANK, C_HEADS, C_NOPE + C_V)
    wkn = jnp.pad(ukv[..., :C_NOPE], ((0, 0), (0, 0), (0, SLOT - C_NOPE)))
    wv = jnp.pad(ukv[..., C_NOPE:], ((0, 0), (0, 0), (0, SLOT - C_V)))
    flat = lambda w: w.reshape(w.shape[0], C_HEADS * SLOT).astype(BF16)

    lane = jnp.arange(LANES)
    rope_lane = (lane >= C_NOPE) & (lane < C_NOPE + C_ROPE)
    wke = jnp.tile(jnp.where(rope_lane[:, None] & (lane[:, None] == lane[None, :]), 1.0, 0.0),
                   (1, C_HEADS)).astype(BF16)
    vones = jnp.tile(jnp.where(lane == C_V, 1.0, 0.0), C_HEADS).reshape(1, -1).astype(F32)
    half = C_ROPE // 2
    inv = ROPE_THETA ** (-jnp.arange(half, dtype=F32) / half)
    inv_lane = jnp.where(rope_lane, inv[(lane - C_NOPE) % half], 0.0).reshape(1, LANES).astype(F32)
    bs_full = jnp.repeat(b_s.T, D_GROUP_DIM, axis=1)
    return w1, flat(wq), flat(wqs), flat(wkn), wke, flat(wv), vones, inv_lane, bs_full


def _odd_layer(x, p, layer, positions, w_in, q_norm, w_uq, kv_norm, w_ukv, v_ln_g, v_ln_b,
               w_s, b_s, c_norm, d_norm, w_out, ln_g, ln_b, ple_proj, ple_gate):
    bsz, seq, _ = x.shape
    n_tok = bsz * seq
    w1, wq, wqs, wkn, wke, wv, vones, inv_lane, bs_full = _odd_weights(w_in, w_uq, w_ukv, b_s)
    tm = 512
    row_spec = lambda w: pl.BlockSpec((tm, w), lambda t: (t, 0))
    wide = C_HEADS * SLOT
    q, k, v, yd, szc = pl.pallas_call(
        _odd_proj_kernel,
        grid=(n_tok // tm,),
        in_specs=[row_spec(D_MODEL), row_spec(1), _full(w1.shape), _full((1, LANES)),
                  _full((1, C_Q_RANK)), _full(wq.shape), _full(wqs.shape),
                  _full((1, C_KV_RANK)), _full(wkn.shape), _full(wke.shape), _full(wv.shape),
                  _full((1, wide)), _full((1, D_WIDTH)), _full((1, D_WIDTH)),
                  _full((D_GROUPS, D_CHUNK, D_CHUNK)), _full((D_CHUNK, D_WIDTH)),
                  _full((1, D_WIDTH))],
        out_specs=[row_spec(wide), row_spec(wide), row_spec(wide), row_spec(D_WIDTH),
                   row_spec(C_WIDTH)],
        out_shape=[jax.ShapeDtypeStruct((n_tok, wide), BF16),
                   jax.ShapeDtypeStruct((n_tok, wide), BF16),
                   jax.ShapeDtypeStruct((n_tok, wide), BF16),
                   jax.ShapeDtypeStruct((n_tok, D_WIDTH), BF16),
                   jax.ShapeDtypeStruct((n_tok, C_WIDTH), F32)],
        compiler_params=_params(("parallel",)),
        name="odd_proj",
    )(x.reshape(n_tok, D_MODEL), positions.reshape(n_tok, 1), w1, inv_lane,
      q_norm.reshape(1, -1), wq, wqs, kv_norm.reshape(1, -1), wkn, wke, wv, vones,
      v_ln_g.reshape(1, -1), v_ln_b.reshape(1, -1), w_s.astype(BF16), bs_full,
      d_norm.reshape(1, -1))

    tq = 256
    tk = 512
    q = q.reshape(bsz, seq, wide)
    k = k.reshape(bsz, seq, wide)
    v = v.reshape(bsz, seq, wide)
    pair = lambda w: pl.BlockSpec((None, seq, w), lambda b, h: (b, 0, h))
    yc = pl.pallas_call(
        functools.partial(_mla_kernel, seq, tq, tk),
        grid=(bsz, C_HEADS // 2),
        in_specs=[pair(2 * SLOT), pair(2 * SLOT), pair(2 * SLOT)],
        out_specs=pair(2 * C_V),
        out_shape=jax.ShapeDtypeStruct((bsz, seq, C_WIDTH), F32),
        scratch_shapes=[pltpu.VMEM((tq, seq), F32), pltpu.VMEM((tq, seq), F32),
                        pltpu.VMEM((tq, LANES), F32), pltpu.VMEM((tq, LANES), F32)],
        compiler_params=_params(("parallel", "parallel")),
        name="mla_attn",
    )(q, k, v)

    out = pl.pallas_call(
        _odd_out_kernel,
        grid=(n_tok // tm,),
        in_specs=[row_spec(C_WIDTH), row_spec(C_WIDTH), row_spec(D_WIDTH), row_spec(D_MODEL),
                  pl.BlockSpec((None, tm, PLE_DIM), lambda t: (layer, t, 0)),
                  _full((1, C_WIDTH)), _full((D_MODEL, D_MODEL)), _full((1, D_MODEL)),
                  _full((1, D_MODEL)), _full((D_MODEL, D_MODEL)), _full((PLE_DIM, D_MODEL))],
        out_specs=row_spec(D_MODEL),
        out_shape=jax.ShapeDtypeStruct((n_tok, D_MODEL), F32),
        compiler_params=_params(("parallel",)),
        name="odd_out",
    )(yc.reshape(n_tok, C_WIDTH), szc, yd, x.reshape(n_tok, D_MODEL),
      p.reshape(DEPTH, n_tok, PLE_DIM), c_norm.reshape(1, -1), w_out.astype(BF16),
      ln_g.reshape(1, -1), ln_b.reshape(1, -1), ple_gate.astype(BF16), ple_proj.astype(BF16))
    return out.reshape(bsz, seq, D_MODEL)


def kernel(x, p, positions, ev_w_in, ev_conv_w, ev_sink, ev_a_norm, ev_b_norm, ev_w_out, od_w_in, od_q_norm, od_w_uq, od_kv_norm, od_w_ukv, od_v_ln_g, od_v_ln_b, od_w_s, od_b_s, od_c_norm, od_d_norm, od_w_out, post_ln_g, post_ln_b, ple_proj, ple_gate):
    for i in range(DEPTH):
        j = i // 2
        if i % 2 == 0:
            x = _even_layer(x, p, i, positions, ev_w_in[j], ev_conv_w[j], ev_sink[j],
                            ev_a_norm[j], ev_b_norm[j], ev_w_out[j],
                            post_ln_g[i], post_ln_b[i], ple_proj[i], ple_gate[i])
        else:
            x = _odd_layer(x, p, i, positions, od_w_in[j], od_q_norm[j], od_w_uq[j],
                           od_kv_norm[j], od_w_ukv[j], od_v_ln_g[j], od_v_ln_b[j],
                           od_w_s[j], od_b_s[j], od_c_norm[j], od_d_norm[j], od_w_out[j],
                           post_ln_g[i], post_ln_b[i], ple_proj[i], ple_gate[i])
    return x
```

```python
import functools
import math

import jax
import jax.numpy as jnp
from jax import lax
from jax.experimental import pallas as pl
from jax.experimental.pallas import tpu as pltpu

F32 = jnp.float32
BF16 = jnp.bfloat16

D_MODEL = 1024
DEPTH = 2
PLE_DIM = 256
DEEPNORM_ALPHA = (2.0 * DEPTH) ** 0.25
BLOCK = 128
EPS = 1e-6
A_HEAD_DIM = 64
A_HEADS = 8
A_KV_HEADS = 2
A_GROUP = A_HEADS // A_KV_HEADS
A_WIDTH = A_HEADS * A_HEAD_DIM
WINDOW = 128
B_WIDTH = 512
C_HEADS = 8
C_NOPE = 64
C_ROPE = 32
C_V = 64
C_WIDTH = C_HEADS * C_V
C_Q_RANK = 256
C_KV_RANK = 128
ROPE_THETA = 10000.0
D_WIDTH = 512
D_GROUPS = 4
D_GROUP_DIM = D_WIDTH // D_GROUPS
D_CHUNK = 128

LANES = 128
SLOT = LANES
VMEM_LIMIT = 56 * 1024 * 1024

_EV_OFF = (0, 512, 640, 768, 1280, 1792, 2304, 3328)
_OD_OFF = (0, 384, 640, 1152, 1664, 2688)

A_SLOPES = tuple(2.0 ** (-8.0 * (h + 1) / A_HEADS) for h in range(A_HEADS))
LOG2E = math.log2(math.e)
A_SCALE_LOG2E = A_HEAD_DIM ** -0.5 * LOG2E
MLA_SCALE_LOG2E = (C_NOPE + C_ROPE) ** -0.5 * LOG2E


def _dot(a, b):
    return jnp.dot(a, b, preferred_element_type=F32)


def _dot_nt(a, b):
    return lax.dot_general(a, b, (((1,), (1,)), ((), ())), preferred_element_type=F32)


def _rms(x, g):
    return x * lax.rsqrt(jnp.mean(x * x, axis=-1, keepdims=True) + EPS) * g


def _layer_norm(x, g, b):
    mu = jnp.mean(x, axis=-1, keepdims=True)
    xc = x - mu
    var = jnp.mean(xc * xc, axis=-1, keepdims=True)
    return xc * lax.rsqrt(var + EPS) * g + b


def _silu(z):
    return z * jax.nn.sigmoid(z)


def _gelu_tanh(x):
    c = math.sqrt(2.0 / math.pi)
    return 0.5 * x * (1.0 + jnp.tanh(c * (x + 0.044715 * (x * x * x))))


def _post_block(y_bf16_dot, x, p, g_ref, b_ref, gate_ref, proj_ref):
    h = _layer_norm(DEEPNORM_ALPHA * x + y_bf16_dot, g_ref[...], b_ref[...])
    gate = jax.nn.sigmoid(_dot(h.astype(BF16), gate_ref[...]))
    return h + gate * _dot(p.astype(BF16), proj_ref[...])


def _even_kernel(seq_len, tile,
                 xc_ref, xp_ref, p_ref, posc_ref, prc_ref, prp_ref, prn_ref,
                 win_ref, convw_ref, sink_ref, anorm_ref, bnorm_ref, wout_ref,
                 g_ref, b_ref, gate_ref, proj_ref,
                 o_ref,
                 kbuf, vbuf, q_a, q_b, bg_a, bg_b, sz_a, sz_b, cz_a, cz_b,
                 posbuf, ybuf, lg_a, lg_b, m_a, m_b):
    t = pl.program_id(0)

    @pl.when(t == 0)
    def _():
        for ref in (kbuf, vbuf, q_a, q_b, bg_a, bg_b, sz_a, sz_b, cz_a, cz_b):
            ref[...] = jnp.zeros(ref.shape, ref.dtype)

    shared = (seq_len, tile, xc_ref, xp_ref, p_ref, posc_ref, prc_ref, prp_ref, prn_ref,
              win_ref, convw_ref, sink_ref, anorm_ref, bnorm_ref, wout_ref,
              g_ref, b_ref, gate_ref, proj_ref, o_ref,
              kbuf, vbuf, posbuf, ybuf, lg_a, lg_b, m_a, m_b)

    @pl.when(t % 2 == 0)
    def _():
        _even_step(*shared, (q_a, bg_a, sz_a, cz_a), (q_b, bg_b, sz_b, cz_b))

    @pl.when(t % 2 == 1)
    def _():
        _even_step(*shared, (q_b, bg_b, sz_b, cz_b), (q_a, bg_a, sz_a, cz_a))


def _even_step(seq_len, tile, xc_ref, xp_ref, p_ref, posc_ref, prc_ref, prp_ref, prn_ref,
               win_ref, convw_ref, sink_ref, anorm_ref, bnorm_ref, wout_ref,
               g_ref, b_ref, gate_ref, proj_ref, o_ref,
               kbuf, vbuf, posbuf, ybuf, lg_a, lg_b, m_a, m_b, proj_bufs, mix_bufs):
    t = pl.program_id(0)
    tiles_per_seq = seq_len // tile
    sub = tile // BLOCK
    pq, pbg, psz, pcz = proj_bufs
    mq, mbg, msz, mcz = mix_bufs

    before_rows = pcz[tile - 8:tile, :]

    xb = xc_ref[...].astype(BF16)

    def seg(n, lo=0, hi=None):
        hi = _EV_OFF[n + 1] - _EV_OFF[n] if hi is None else hi
        return _dot(xb, win_ref[:, _EV_OFF[n] + lo:_EV_OFF[n] + hi])

    krow_p = pl.multiple_of(BLOCK + (t % tiles_per_seq) * tile, BLOCK)

    lane_t = lax.broadcasted_iota(jnp.int32, (tile, LANES), 1)
    low = lane_t < A_HEAD_DIM
    one = jnp.where(lane_t == A_HEAD_DIM, 1.0, 0.0)

    def slots(v, fill):
        return (jnp.where(low, v, fill),
                jnp.where(low, pltpu.roll(v, A_HEAD_DIM, axis=1), fill))

    def proj_kv():
        lo, hi = slots(seg(1), 0.0)
        kbuf[pl.ds(krow_p, tile), 0:SLOT] = lo.astype(BF16)
        kbuf[pl.ds(krow_p, tile), SLOT:] = hi.astype(BF16)
        lo, hi = slots(seg(2), one)
        vbuf[pl.ds(krow_p, tile), 0:SLOT] = lo.astype(BF16)
        vbuf[pl.ds(krow_p, tile), SLOT:] = hi.astype(BF16)

    piece_w = 2 * LANES

    def proj_q(part):
        q = seg(0, part * piece_w, (part + 1) * piece_w) * A_SCALE_LOG2E
        for n in range(piece_w // LANES):
            lo, hi = slots(q[:, n * LANES:(n + 1) * LANES], 0.0)
            h0 = 2 * (part * (piece_w // LANES) + n)
            pq[:, h0 * SLOT:(h0 + 1) * SLOT] = lo.astype(BF16)
            pq[:, (h0 + 1) * SLOT:(h0 + 2) * SLOT] = hi.astype(BF16)

    def proj_bg(part):
        lo, hi = part * piece_w, (part + 1) * piece_w
        pbg[:, lo:hi] = seg(3, lo, hi)

    def proj_cg(part):
        lo, hi = part * piece_w, (part + 1) * piece_w
        pcz[:, lo:hi] = seg(4, lo, hi)

    def proj_xin(part):
        lo, hi = part * piece_w, (part + 1) * piece_w
        pcz[:, lo:hi] = pcz[:, lo:hi] * seg(5, lo, hi)

    def proj_sz(part):
        lo, hi = part * piece_w, (part + 1) * piece_w
        psz[:, lo:hi] = _silu(seg(6, lo, hi))

    proj_kv()
    pieces = [functools.partial(fn, part)
              for fn, parts in ((proj_q, 2), (proj_bg, 2), (proj_cg, 2), (proj_xin, 2), (proj_sz, 4))
              for part in range(parts)]

    i = jnp.maximum(t - 1, 0) % tiles_per_seq
    krow_m = pl.multiple_of(i * tile, BLOCK)

    posbuf[:, 0:BLOCK] = prp_ref[...]
    posbuf[:, BLOCK:BLOCK + tile] = prc_ref[...]
    posbuf[:, BLOCK + tile:] = prn_ref[...]

    q_idx = lax.broadcasted_iota(jnp.int32, (BLOCK, 3 * BLOCK), 0)
    w_idx = lax.broadcasted_iota(jnp.int32, (BLOCK, 3 * BLOCK), 1)
    in_band = jnp.abs(w_idx - BLOCK - q_idx) <= WINDOW
    lane = lax.broadcasted_iota(jnp.int32, (BLOCK, LANES), 1)
    dists = {}

    def masked_dist(j):
        if j not in dists:
            k_glob = (i * sub + j) * BLOCK + w_idx - BLOCK
            valid = in_band & (k_glob >= 0) & (k_glob < seq_len)
            d = jnp.abs(posc_ref[j * BLOCK:(j + 1) * BLOCK, :]
                        - posbuf[:, j * BLOCK:(j + 3) * BLOCK]).astype(F32)
            dists[j] = jnp.where(valid, d, jnp.inf)
        return dists[j]

    def scores(j, g, lg_ref, m_ref):
        qrow = slice(j * BLOCK, (j + 1) * BLOCK)
        qg = jnp.concatenate([mq[qrow, (A_GROUP * g + hh) * SLOT:(A_GROUP * g + hh + 1) * SLOT]
                              for hh in range(A_GROUP)], axis=0)
        s = _dot_nt(qg, kbuf[pl.ds(krow_m + j * BLOCK, 3 * BLOCK), g * SLOT:(g + 1) * SLOT])
        d = masked_dist(j)
        for hh in range(A_GROUP):
            h = A_GROUP * g + hh
            r = slice(hh * BLOCK, (hh + 1) * BLOCK)
            lg = s[r, :] - (A_SLOPES[h] * LOG2E) * d
            lg_ref[r, :] = lg
            mx = jnp.maximum(jnp.max(lg, axis=-1, keepdims=True), sink_ref[h])
            m_ref[r, :] = jnp.broadcast_to(mx, (BLOCK, LANES))

    def attend(j, g, lg_ref, m_ref):
        rows = slice(j * BLOCK, (j + 1) * BLOCK)
        m = m_ref[...]
        e = jnp.concatenate([jnp.exp2(lg_ref[:, n * LANES:(n + 1) * LANES] - m).astype(BF16)
                             for n in range(3)], axis=1)
        pv = _dot(e, vbuf[pl.ds(krow_m + j * BLOCK, 3 * BLOCK), g * SLOT:(g + 1) * SLOT])
        outs = []
        for hh in range(A_GROUP):
            r = slice(hh * BLOCK, (hh + 1) * BLOCK)
            denom = (jnp.broadcast_to(pv[r, A_HEAD_DIM:A_HEAD_DIM + 1], (BLOCK, LANES))
                     + jnp.exp2(sink_ref[A_GROUP * g + hh] - m[r, :]))
            outs.append(pv[r, :] / denom)
        for pr in range(A_GROUP // 2):
            c0 = (A_GROUP // 2 * g + pr) * LANES
            ybuf[rows, c0:c0 + LANES] = jnp.where(
                lane < A_HEAD_DIM, outs[2 * pr], pltpu.roll(outs[2 * pr + 1], A_HEAD_DIM, axis=1))

    chains = [(j, g) for j in range(sub) for g in range(A_KV_HEADS)]
    bufs = [(lg_a, m_a), (lg_b, m_b)]
    scores(*chains[0], *bufs[0])
    late = [pieces.pop() for _ in range(4)][::-1]
    for c, (j, g) in enumerate(chains):
        if c + 1 < len(chains):
            scores(*chains[c + 1], *bufs[(c + 1) % 2])
        attend(j, g, *bufs[c % 2])
        pieces.pop(0)()
    assert not pieces

    late[0]()
    cz = mcz[...]
    row = lax.broadcasted_iota(jnp.int32, cz.shape, 0)
    prev_row = jnp.where(i > 0, before_rows[7:8, :], 0.0)
    next_row = jnp.where(i < tiles_per_seq - 1, pcz[0:1, :], 0.0)
    z_prev = jnp.where(row == 0, prev_row, pltpu.roll(cz, 1, axis=0))
    z_next = jnp.where(row == tile - 1, next_row, pltpu.roll(cz, tile - 1, axis=0))
    conv = (convw_ref[0:1, :] * z_prev + convw_ref[1:2, :] * cz
            + convw_ref[2:3, :] * z_next)
    yb = mbg[...] * conv

    late[1]()
    ya_n = _rms(ybuf[...], anorm_ref[...]) * msz[:, 0:A_WIDTH]
    yb_n = _rms(yb, bnorm_ref[...]) * msz[:, A_WIDTH:]
    mix = (_dot(ya_n.astype(BF16), wout_ref[0:A_WIDTH, :])
           + _dot(yb_n.astype(BF16), wout_ref[A_WIDTH:, :]))
    late[2]()
    h = _layer_norm(DEEPNORM_ALPHA * xp_ref[...] + mix, g_ref[...], b_ref[...])
    emb = _dot(p_ref[...].astype(BF16), proj_ref[...])
    late[3]()
    gate = jax.nn.sigmoid(_dot(h.astype(BF16), gate_ref[...]))
    o_ref[...] = h + gate * emb


def _odd_proj_kernel(x_ref, pos_ref, w_ref, inv_ref, qn_ref, wq_ref, wqs_ref,
                     kvn_ref, wkv_ref, vones_ref,
                     lng_ref, lnb_ref, ws_ref, bs_ref, dnorm_ref,
                     q_ref, k_ref, v_ref, yd_ref, szc_ref):
    tile = x_ref.shape[0]
    wide = C_HEADS * SLOT
    xb = x_ref[...].astype(BF16)

    def seg(i):
        return _dot(xb, w_ref[:, _OD_OFF[i]:_OD_OFF[i + 1]])

    packs = LANES // C_ROPE
    rb = tile // packs
    lane = lax.broadcasted_iota(jnp.int32, (rb, LANES), 1)
    pos_f = pos_ref[...].astype(F32)
    packed = jnp.broadcast_to(pos_f[0:rb, :], (rb, LANES))
    for u in range(1, packs):
        packed = jnp.where(lane >= u * C_ROPE,
                           jnp.broadcast_to(pos_f[u * rb:(u + 1) * rb, :], (rb, LANES)), packed)
    ang = packed * inv_ref[...]
    cos_p = jnp.cos(ang)
    sin_p = jnp.sin(ang)
    rope_lane = (lane >= C_NOPE) & (lane < C_NOPE + C_ROPE)
    cos_blocks, sin_blocks = [], []
    for u in range(packs):
        shift = (C_NOPE - u * C_ROPE) % LANES
        cu = pltpu.roll(cos_p, shift, axis=1) if shift else cos_p
        su = pltpu.roll(sin_p, shift, axis=1) if shift else sin_p
        cos_blocks.append(jnp.where(rope_lane, cu, 1.0))
        sin_blocks.append(jnp.where(rope_lane, su, 0.0))
    cos_t = jnp.concatenate(cos_blocks, axis=0)
    sin_t = jnp.concatenate(sin_blocks, axis=0)

    a = seg(0)
    cqn = _rms(a[:, 0:C_Q_RANK], qn_ref[...]).astype(BF16)
    ckvn = _rms(a[:, C_Q_RANK:], kvn_ref[...]).astype(BF16)

    qa = _dot(cqn, wq_ref[...])
    qs = _dot(cqn, wqs_ref[...])
    for h in range(C_HEADS):
        sl = slice(h * SLOT, (h + 1) * SLOT)
        q_ref[:, sl] = ((qa[:, sl] * cos_t + qs[:, sl] * sin_t) * MLA_SCALE_LOG2E).astype(BF16)

    kr = seg(1)
    kr_rot = (kr[:, 0:LANES] * cos_t + kr[:, LANES:] * sin_t).astype(BF16)
    kv = _dot(jnp.concatenate([ckvn, kr_rot], axis=1), wkv_ref[...])
    k_ref[...] = kv[:, 0:wide].astype(BF16)
    v_ref[...] = (kv[:, wide:] + vones_ref[...]).astype(BF16)

    gu = _gelu_tanh(seg(2))
    vn = _layer_norm(_gelu_tanh(seg(3)), lng_ref[...], lnb_ref[...]).astype(BF16)
    n_ch = tile // D_CHUNK
    mixed = []
    for g in range(D_GROUPS):
        cols = slice(g * D_GROUP_DIM, (g + 1) * D_GROUP_DIM)
        rhs = jnp.concatenate([vn[c * D_CHUNK:(c + 1) * D_CHUNK, cols] for c in range(n_ch)], axis=1)
        mixed.append(_dot(ws_ref[g], rhs))
    chunks = [jnp.concatenate([mixed[g][:, c * D_GROUP_DIM:(c + 1) * D_GROUP_DIM]
                               for g in range(D_GROUPS)], axis=1) + bs_ref[...]
              for c in range(n_ch)]
    yd = gu * jnp.concatenate(chunks, axis=0)
    sz = _silu(seg(4))
    yd_ref[...] = (_rms(yd, dnorm_ref[...]) * sz[:, C_WIDTH:]).astype(BF16)
    szc_ref[...] = sz[:, 0:C_WIDTH]


def _mla_kernel(seq_len, tq, tk, unroll, q_ref, k_ref, v_ref, o_ref, s_a, s_b, m_a, m_b):
    n_qt = seq_len // tq
    n_c = seq_len // tk

    def scores(qt, hh, s_ref, m_ref):
        sl = slice(hh * SLOT, (hh + 1) * SLOT)
        q = q_ref[pl.ds(pl.multiple_of(qt * tq, tq), tq), sl]
        part = None
        for c in range(n_c):
            s = _dot_nt(q, k_ref[c * tk:(c + 1) * tk, sl])
            s_ref[:, c * tk:(c + 1) * tk] = s
            tiles = [s[:, j * LANES:(j + 1) * LANES] for j in range(tk // LANES)]
            while len(tiles) > 1:
                tiles = [jnp.maximum(a, b) for a, b in zip(tiles[0::2], tiles[1::2])]
            part = tiles[0] if part is None else jnp.maximum(part, tiles[0])
        m_ref[...] = jnp.broadcast_to(jnp.max(part, axis=-1, keepdims=True), m_ref.shape)

    def attend(hh, s_ref, m_ref):
        sl = slice(hh * SLOT, (hh + 1) * SLOT)
        m = m_ref[...]
        acc = None
        for c in range(n_c):
            p = jnp.concatenate(
                [jnp.exp2(s_ref[:, c * tk + j * LANES:c * tk + (j + 1) * LANES] - m).astype(BF16)
                 for j in range(tk // LANES)], axis=1)
            d = _dot(p, v_ref[c * tk:(c + 1) * tk, sl])
            acc = d if acc is None else acc + d
        return acc / acc[:, C_V:C_V + 1]

    scores(0, 0, s_a, m_a)

    def one_tile(qt):
        scores(qt, 1, s_b, m_b)
        out0 = attend(0, s_a, m_a)
        scores(jnp.minimum(qt + 1, n_qt - 1), 0, s_a, m_a)
        out1 = attend(1, s_b, m_b)
        lane = lax.broadcasted_iota(jnp.int32, out0.shape, 1)
        o_ref[pl.ds(pl.multiple_of(qt * tq, tq), tq), :] = jnp.where(
            lane < C_V, out0, pltpu.roll(out1, C_V, axis=1))

    def body(it, carry):
        for u in range(unroll):
            one_tile(it * unroll + u)
        return carry

    lax.fori_loop(0, n_qt // unroll, body, 0)


def _odd_out_kernel(yc_ref, szc_ref, yd_ref, x_ref, p_ref, cnorm_ref, wout_ref,
                    g_ref, b_ref, gate_ref, proj_ref, o_ref):
    yc_n = _rms(yc_ref[...], cnorm_ref[...]) * szc_ref[...]
    mix = (_dot(yc_n.astype(BF16), wout_ref[0:C_WIDTH, :])
           + _dot(yd_ref[...], wout_ref[C_WIDTH:, :]))
    o_ref[...] = _post_block(mix, x_ref[...], p_ref[...], g_ref, b_ref, gate_ref, proj_ref)


def _full(shape):
    return pl.BlockSpec(shape, lambda *_: (0,) * len(shape))


def _once(shape):
    return pl.BlockSpec(shape, lambda *_: (0,) * len(shape), pipeline_mode=pl.Buffered(1))


def _params(sem):
    return pltpu.CompilerParams(dimension_semantics=sem, vmem_limit_bytes=VMEM_LIMIT)


def _even_layer(x, p, layer, positions, w_in, conv_w, sink, a_norm, b_norm, w_out,
                ln_g, ln_b, ple_proj, ple_gate):
    bsz, seq, _ = x.shape
    n_tok = bsz * seq
    tile = 512
    n_tiles = n_tok // tile
    tiles_per_seq = seq // tile
    sub = tile // BLOCK
    nb = seq // BLOCK
    q_w = A_HEADS * SLOT
    kv_w = A_KV_HEADS * SLOT
    x2 = x.reshape(n_tok, D_MODEL)
    pos_col = positions.reshape(n_tok, 1)
    pos_row = positions.reshape(bsz, 1, seq)

    cur = lambda t: jnp.minimum(t, n_tiles - 1)
    prv = lambda t: jnp.maximum(t - 1, 0)
    seq_of = lambda t: prv(t) // tiles_per_seq
    blk_of = lambda t: (prv(t) % tiles_per_seq) * sub
    in_specs = [
        pl.BlockSpec((tile, D_MODEL), lambda t: (cur(t), 0)),
        pl.BlockSpec((tile, D_MODEL), lambda t: (prv(t), 0)),
        pl.BlockSpec((None, tile, PLE_DIM), lambda t: (layer, prv(t), 0)),
        pl.BlockSpec((tile, 1), lambda t: (prv(t), 0)),
        pl.BlockSpec((None, 1, tile), lambda t: (seq_of(t), 0, prv(t) % tiles_per_seq)),
        pl.BlockSpec((None, 1, BLOCK), lambda t: (seq_of(t), 0, jnp.maximum(blk_of(t) - 1, 0))),
        pl.BlockSpec((None, 1, BLOCK), lambda t: (seq_of(t), 0, jnp.minimum(blk_of(t) + sub, nb - 1))),
        _once((D_MODEL, _EV_OFF[-1])),
        _once((3, B_WIDTH)),
        pl.BlockSpec(memory_space=pltpu.SMEM),
        _once((1, A_WIDTH)), _once((1, B_WIDTH)), _once((D_MODEL, D_MODEL)),
        _once((1, D_MODEL)), _once((1, D_MODEL)), _once((D_MODEL, D_MODEL)),
        _once((PLE_DIM, D_MODEL)),
    ]
    out = pl.pallas_call(
        functools.partial(_even_kernel, seq, tile),
        grid=(n_tiles + 1,),
        in_specs=in_specs,
        out_specs=pl.BlockSpec((tile, D_MODEL), lambda t: (prv(t), 0)),
        out_shape=jax.ShapeDtypeStruct((n_tok, D_MODEL), F32),
        scratch_shapes=[pltpu.VMEM((seq + 2 * BLOCK, kv_w), BF16),
                        pltpu.VMEM((seq + 2 * BLOCK, kv_w), BF16),
                        pltpu.VMEM((tile, q_w), BF16), pltpu.VMEM((tile, q_w), BF16),
                        pltpu.VMEM((tile, B_WIDTH), F32), pltpu.VMEM((tile, B_WIDTH), F32),
                        pltpu.VMEM((tile, A_WIDTH + B_WIDTH), F32),
                        pltpu.VMEM((tile, A_WIDTH + B_WIDTH), F32),
                        pltpu.VMEM((tile, B_WIDTH), F32), pltpu.VMEM((tile, B_WIDTH), F32),
                        pltpu.VMEM((1, tile + 2 * BLOCK), jnp.int32),
                        pltpu.VMEM((tile, A_WIDTH), F32),
                        pltpu.VMEM((A_GROUP * BLOCK, 3 * BLOCK), F32),
                        pltpu.VMEM((A_GROUP * BLOCK, 3 * BLOCK), F32),
                        pltpu.VMEM((A_GROUP * BLOCK, LANES), F32),
                        pltpu.VMEM((A_GROUP * BLOCK, LANES), F32)],
        compiler_params=_params(("arbitrary",)),
        name="even_layer",
    )(x2, x2, p.reshape(DEPTH, n_tok, PLE_DIM), pos_col, pos_row, pos_row, pos_row,
      w_in.astype(BF16), conv_w, sink * LOG2E, a_norm.reshape(1, -1), b_norm.reshape(1, -1),
      w_out.astype(BF16), ln_g.reshape(1, -1), ln_b.reshape(1, -1),
      ple_gate.astype(BF16), ple_proj.astype(BF16))
    return out.reshape(bsz, seq, D_MODEL)


def _swap_halves(r):
    half = r.shape[-1] // 2
    return jnp.concatenate([-r[..., half:], r[..., :half]], axis=-1)


def _odd_weights(w_in, w_uq, w_ukv, b_s):
    d = w_in.shape[0]
    o_kr = C_Q_RANK + C_KV_RANK
    o_du = o_kr + C_ROPE
    w_kr = w_in[:, o_kr:o_du]
    kr_pair = jnp.zeros((d, 2 * LANES), F32)
    kr_pair = kr_pair.at[:, C_NOPE:C_NOPE + C_ROPE].set(w_kr)
    kr_pair = kr_pair.at[:, LANES + C_NOPE:LANES + C_NOPE + C_ROPE].set(_swap_halves(w_kr))
    w1 = jnp.concatenate([w_in[:, :o_kr], kr_pair, w_in[:, o_du:]], axis=1).astype(BF16)

    uq = w_uq.reshape(C_Q_RANK, C_HEADS, C_NOPE + C_ROPE)
    pad = SLOT - C_NOPE - C_ROPE
    wq = jnp.pad(uq, ((0, 0), (0, 0), (0, pad)))
    wqs = jnp.pad(_swap_halves(uq[..., C_NOPE:]), ((0, 0), (0, 0), (C_NOPE, pad)))
    ukv = w_ukv.reshape(C_KV_RANK, C_HEADS, C_NOPE + C_V)
    wkn = jnp.pad(ukv[..., :C_NOPE], ((0, 0), (0, 0), (0, SLOT - C_NOPE)))
    wv = jnp.pad(ukv[..., C_NOPE:], ((0, 0), (0, 0), (0, SLOT - C_V)))
    flat = lambda w: w.reshape(w.shape[0], C_HEADS * SLOT).astype(BF16)

    lane = jnp.arange(LANES)
    rope_lane = (lane >= C_NOPE) & (lane < C_NOPE + C_ROPE)
    wke = jnp.tile(jnp.where(rope_lane[:, None] & (lane[:, None] == lane[None, :]), 1.0, 0.0),
                   (1, C_HEADS)).astype(BF16)
    vones = jnp.tile(jnp.where(lane == C_V, 1.0, 0.0), C_HEADS).reshape(1, -1).astype(F32)
    half = C_ROPE // 2
    inv = ROPE_THETA ** (-jnp.arange(half, dtype=F32) / half)
    inv_lane = inv[(lane % C_ROPE) % half].reshape(1, LANES).astype(F32)
    bs_full = jnp.repeat(b_s.T, D_GROUP_DIM, axis=1)
    wkv = jnp.concatenate([jnp.concatenate([flat(wkn), flat(wv)], axis=1),
                           jnp.concatenate([wke, jnp.zeros_like(wke)], axis=1)], axis=0)
    return w1, flat(wq), flat(wqs), wkv, vones, inv_lane, bs_full


def _odd_layer(x, p, layer, positions, w_in, q_norm, w_uq, kv_norm, w_ukv, v_ln_g, v_ln_b,
               w_s, b_s, c_norm, d_norm, w_out, ln_g, ln_b, ple_proj, ple_gate):
    bsz, seq, _ = x.shape
    n_tok = bsz * seq
    w1, wq, wqs, wkv, vones, inv_lane, bs_full = _odd_weights(w_in, w_uq, w_ukv, b_s)
    tm = 512
    row_spec = lambda w: pl.BlockSpec((tm, w), lambda t: (t, 0))
    wide = C_HEADS * SLOT
    q, k, v, yd, szc = pl.pallas_call(
        _odd_proj_kernel,
        grid=(n_tok // tm,),
        in_specs=[row_spec(D_MODEL), row_spec(1), _full(w1.shape), _full((1, LANES)),
                  _full((1, C_Q_RANK)), _full(wq.shape), _full(wqs.shape),
                  _full((1, C_KV_RANK)), _full(wkv.shape),
                  _full((1, wide)), _full((1, D_WIDTH)), _full((1, D_WIDTH)),
                  _full((D_GROUPS, D_CHUNK, D_CHUNK)), _full((D_CHUNK, D_WIDTH)),
                  _full((1, D_WIDTH))],
        out_specs=[row_spec(wide), row_spec(wide), row_spec(wide), row_spec(D_WIDTH),
                   row_spec(C_WIDTH)],
        out_shape=[jax.ShapeDtypeStruct((n_tok, wide), BF16),
                   jax.ShapeDtypeStruct((n_tok, wide), BF16),
                   jax.ShapeDtypeStruct((n_tok, wide), BF16),
                   jax.ShapeDtypeStruct((n_tok, D_WIDTH), BF16),
                   jax.ShapeDtypeStruct((n_tok, C_WIDTH), F32)],
        compiler_params=_params(("parallel",)),
        name="odd_proj",
    )(x.reshape(n_tok, D_MODEL), positions.reshape(n_tok, 1), w1, inv_lane,
      q_norm.reshape(1, -1), wq, wqs, kv_norm.reshape(1, -1), wkv, vones,
      v_ln_g.reshape(1, -1), v_ln_b.reshape(1, -1), w_s.astype(BF16), bs_full,
      d_norm.reshape(1, -1))

    tq = 256
    tk = 512
    q = q.reshape(bsz, seq, wide)
    k = k.reshape(bsz, seq, wide)
    v = v.reshape(bsz, seq, wide)
    pair = lambda w: pl.BlockSpec((None, seq, w), lambda b, h: (b, 0, h))
    yc = pl.pallas_call(
        functools.partial(_mla_kernel, seq, tq, tk, 4),
        grid=(bsz, C_HEADS // 2),
        in_specs=[pair(2 * SLOT), pair(2 * SLOT), pair(2 * SLOT)],
        out_specs=pair(2 * C_V),
        out_shape=jax.ShapeDtypeStruct((bsz, seq, C_WIDTH), F32),
        scratch_shapes=[pltpu.VMEM((tq, seq), F32), pltpu.VMEM((tq, seq), F32),
                        pltpu.VMEM((tq, LANES), F32), pltpu.VMEM((tq, LANES), F32)],
        compiler_params=_params(("parallel", "parallel")),
        name="mla_attn",
    )(q, k, v)

    out = pl.pallas_call(
        _odd_out_kernel,
        grid=(n_tok // tm,),
        in_specs=[row_spec(C_WIDTH), row_spec(C_WIDTH), row_spec(D_WIDTH), row_spec(D_MODEL),
                  pl.BlockSpec((None, tm, PLE_DIM), lambda t: (layer, t, 0)),
                  _full((1, C_WIDTH)), _full((D_MODEL, D_MODEL)), _full((1, D_MODEL)),
                  _full((1, D_MODEL)), _full((D_MODEL, D_MODEL)), _full((PLE_DIM, D_MODEL))],
        out_specs=row_spec(D_MODEL),
        out_shape=jax.ShapeDtypeStruct((n_tok, D_MODEL), F32),
        compiler_params=_params(("parallel",)),
        name="odd_out",
    )(yc.reshape(n_tok, C_WIDTH), szc, yd, x.reshape(n_tok, D_MODEL),
      p.reshape(DEPTH, n_tok, PLE_DIM), c_norm.reshape(1, -1), w_out.astype(BF16),
      ln_g.reshape(1, -1), ln_b.reshape(1, -1), ple_gate.astype(BF16), ple_proj.astype(BF16))
    return out.reshape(bsz, seq, D_MODEL)


def kernel(x, p, positions, ev_w_in, ev_conv_w, ev_sink, ev_a_norm, ev_b_norm, ev_w_out, od_w_in, od_q_norm, od_w_uq, od_kv_norm, od_w_ukv, od_v_ln_g, od_v_ln_b, od_w_s, od_b_s, od_c_norm, od_d_norm, od_w_out, post_ln_g, post_ln_b, ple_proj, ple_gate):
    for i in range(DEPTH):
        j = i // 2
        if i % 2 == 0:
            x = _even_layer(x, p, i, positions, ev_w_in[j], ev_conv_w[j], ev_sink[j],
                            ev_a_norm[j], ev_b_norm[j], ev_w_out[j],
                            post_ln_g[i], post_ln_b[i], ple_proj[i], ple_gate[i])
        else:
            x = _odd_layer(x, p, i, positions, od_w_in[j], od_q_norm[j], od_w_uq[j],
                           od_kv_norm[j], od_w_ukv[j], od_v_ln_g[j], od_v_ln_b[j],
                           od_w_s[j], od_b_s[j], od_c_norm[j], od_d_norm[j], od_w_out[j],
                           post_ln_g[i], post_ln_b[i], ple_proj[i], ple_gate[i])
    return x
```

```python
import functools
import math

import jax
import jax.numpy as jnp
from jax import lax
from jax.experimental import pallas as pl
from jax.experimental.pallas import tpu as pltpu

F32 = jnp.float32
BF16 = jnp.bfloat16

D_MODEL = 1024
DEPTH = 2
PLE_DIM = 256
DEEPNORM_ALPHA = (2.0 * DEPTH) ** 0.25
BLOCK = 128
EPS = 1e-6
A_HEAD_DIM = 64
A_HEADS = 8
A_KV_HEADS = 2
A_GROUP = A_HEADS // A_KV_HEADS
A_WIDTH = A_HEADS * A_HEAD_DIM
WINDOW = 128
B_WIDTH = 512
C_HEADS = 8
C_NOPE = 64
C_ROPE = 32
C_V = 64
C_WIDTH = C_HEADS * C_V
C_Q_RANK = 256
C_KV_RANK = 128
ROPE_THETA = 10000.0
D_WIDTH = 512
D_GROUPS = 4
D_GROUP_DIM = D_WIDTH // D_GROUPS
D_CHUNK = 128

LANES = 128
SLOT = LANES
VMEM_LIMIT = 56 * 1024 * 1024
OUT_SPLIT = 4

_EV_OFF = (0, 512, 640, 768, 1280, 1792, 2304, 3328)
_OD_OFF = (0, 384, 640, 1152, 1664, 2688)

A_SLOPES = tuple(2.0 ** (-8.0 * (h + 1) / A_HEADS) for h in range(A_HEADS))
LOG2E = math.log2(math.e)
A_SCALE_LOG2E = A_HEAD_DIM ** -0.5 * LOG2E
MLA_SCALE_LOG2E = (C_NOPE + C_ROPE) ** -0.5 * LOG2E


def _dot(a, b):
    return jnp.dot(a, b, preferred_element_type=F32)


def _dot_nt(a, b):
    return lax.dot_general(a, b, (((1,), (1,)), ((), ())), preferred_element_type=F32)


def _rms(x, g):
    return x * lax.rsqrt(jnp.mean(x * x, axis=-1, keepdims=True) + EPS) * g


def _layer_norm(x, g, b):
    mu = jnp.mean(x, axis=-1, keepdims=True)
    xc = x - mu
    var = jnp.mean(xc * xc, axis=-1, keepdims=True)
    return xc * lax.rsqrt(var + EPS) * g + b


def _silu(z):
    return z * jax.nn.sigmoid(z)


def _gelu_tanh(x):
    c = math.sqrt(2.0 / math.pi)
    return 0.5 * x * (1.0 + jnp.tanh(c * (x + 0.044715 * (x * x * x))))


def _post_block(y_bf16_dot, x, p, g_ref, b_ref, gate_ref, proj_ref):
    h = _layer_norm(DEEPNORM_ALPHA * x + y_bf16_dot, g_ref[...], b_ref[...])
    gate = jax.nn.sigmoid(_dot(h.astype(BF16), gate_ref[...]))
    return h + gate * _dot(p.astype(BF16), proj_ref[...])


def _even_kernel(seq_len, tile,
                 xc_ref, xp_ref, p_ref, posc_ref, prc_ref, prp_ref, prn_ref,
                 win_ref, convw_ref, sink_ref, anorm_ref, bnorm_ref, wout_ref,
                 g_ref, b_ref, gate_ref, proj_ref,
                 o_ref,
                 kbuf, vbuf, q_a, q_b, bg_a, bg_b, sz_a, sz_b, cz_a, cz_b,
                 posbuf, ybuf, lg_a, lg_b, m_a, m_b):
    t = pl.program_id(0)

    @pl.when(t == 0)
    def _():
        for ref in (kbuf, vbuf, q_a, q_b, bg_a, bg_b, sz_a, sz_b, cz_a, cz_b):
            ref[...] = jnp.zeros(ref.shape, ref.dtype)

    shared = (seq_len, tile, xc_ref, xp_ref, p_ref, posc_ref, prc_ref, prp_ref, prn_ref,
              win_ref, convw_ref, sink_ref, anorm_ref, bnorm_ref, wout_ref,
              g_ref, b_ref, gate_ref, proj_ref, o_ref,
              kbuf, vbuf, posbuf, ybuf, lg_a, lg_b, m_a, m_b)

    @pl.when(t % 2 == 0)
    def _():
        _even_step(*shared, (q_a, bg_a, sz_a, cz_a), (q_b, bg_b, sz_b, cz_b))

    @pl.when(t % 2 == 1)
    def _():
        _even_step(*shared, (q_b, bg_b, sz_b, cz_b), (q_a, bg_a, sz_a, cz_a))


def _even_step(seq_len, tile, xc_ref, xp_ref, p_ref, posc_ref, prc_ref, prp_ref, prn_ref,
               win_ref, convw_ref, sink_ref, anorm_ref, bnorm_ref, wout_ref,
               g_ref, b_ref, gate_ref, proj_ref, o_ref,
               kbuf, vbuf, posbuf, ybuf, lg_a, lg_b, m_a, m_b, proj_bufs, mix_bufs):
    t = pl.program_id(0)
    tiles_per_seq = seq_len // tile
    sub = tile // BLOCK
    pq, pbg, psz, pcz = proj_bufs
    mq, mbg, msz, mcz = mix_bufs

    before_rows = pcz[tile - 8:tile, :]

    xb = xc_ref[...].astype(BF16)

    def seg(n, lo=0, hi=None):
        hi = _EV_OFF[n + 1] - _EV_OFF[n] if hi is None else hi
        return _dot(xb, win_ref[:, _EV_OFF[n] + lo:_EV_OFF[n] + hi])

    krow_p = pl.multiple_of(BLOCK + (t % tiles_per_seq) * tile, BLOCK)

    lane_t = lax.broadcasted_iota(jnp.int32, (tile, LANES), 1)
    low = lane_t < A_HEAD_DIM
    one = jnp.where(lane_t == A_HEAD_DIM, 1.0, 0.0)

    def slots(v, fill):
        return (jnp.where(low, v, fill),
                jnp.where(low, pltpu.roll(v, A_HEAD_DIM, axis=1), fill))

    def proj_kv():
        lo, hi = slots(seg(1), 0.0)
        kbuf[pl.ds(krow_p, tile), 0:SLOT] = lo.astype(BF16)
        kbuf[pl.ds(krow_p, tile), SLOT:] = hi.astype(BF16)
        lo, hi = slots(seg(2), one)
        vbuf[pl.ds(krow_p, tile), 0:SLOT] = lo.astype(BF16)
        vbuf[pl.ds(krow_p, tile), SLOT:] = hi.astype(BF16)

    piece_w = 2 * LANES

    def proj_q(part):
        q = seg(0, part * piece_w, (part + 1) * piece_w) * A_SCALE_LOG2E
        for n in range(piece_w // LANES):
            lo, hi = slots(q[:, n * LANES:(n + 1) * LANES], 0.0)
            h0 = 2 * (part * (piece_w // LANES) + n)
            pq[:, h0 * SLOT:(h0 + 1) * SLOT] = lo.astype(BF16)
            pq[:, (h0 + 1) * SLOT:(h0 + 2) * SLOT] = hi.astype(BF16)

    def proj_bg(part):
        lo, hi = part * piece_w, (part + 1) * piece_w
        pbg[:, lo:hi] = seg(3, lo, hi)

    def proj_cg(part):
        lo, hi = part * piece_w, (part + 1) * piece_w
        pcz[:, lo:hi] = seg(4, lo, hi)

    def proj_xin(part):
        lo, hi = part * piece_w, (part + 1) * piece_w
        pcz[:, lo:hi] = pcz[:, lo:hi] * seg(5, lo, hi)

    def proj_sz(part):
        lo, hi = part * piece_w, (part + 1) * piece_w
        psz[:, lo:hi] = _silu(seg(6, lo, hi))

    proj_kv()
    pieces = [functools.partial(fn, part)
              for fn, parts in ((proj_q, 2), (proj_bg, 2), (proj_cg, 2), (proj_xin, 2), (proj_sz, 4))
              for part in range(parts)]

    i = jnp.maximum(t - 1, 0) % tiles_per_seq
    krow_m = pl.multiple_of(i * tile, BLOCK)

    posbuf[:, 0:BLOCK] = prp_ref[...]
    posbuf[:, BLOCK:BLOCK + tile] = prc_ref[...]
    posbuf[:, BLOCK + tile:] = prn_ref[...]

    q_idx = lax.broadcasted_iota(jnp.int32, (BLOCK, 3 * BLOCK), 0)
    w_idx = lax.broadcasted_iota(jnp.int32, (BLOCK, 3 * BLOCK), 1)
    in_band = jnp.abs(w_idx - BLOCK - q_idx) <= WINDOW
    lane = lax.broadcasted_iota(jnp.int32, (BLOCK, LANES), 1)
    dists = {}

    def masked_dist(j):
        if j not in dists:
            k_glob = (i * sub + j) * BLOCK + w_idx - BLOCK
            valid = in_band & (k_glob >= 0) & (k_glob < seq_len)
            d = jnp.abs(posc_ref[j * BLOCK:(j + 1) * BLOCK, :]
                        - posbuf[:, j * BLOCK:(j + 3) * BLOCK]).astype(F32)
            dists[j] = jnp.where(valid, d, jnp.inf)
        return dists[j]

    def scores(j, g, lg_ref, m_ref):
        qrow = slice(j * BLOCK, (j + 1) * BLOCK)
        qg = jnp.concatenate([mq[qrow, (A_GROUP * g + hh) * SLOT:(A_GROUP * g + hh + 1) * SLOT]
                              for hh in range(A_GROUP)], axis=0)
        s = _dot_nt(qg, kbuf[pl.ds(krow_m + j * BLOCK, 3 * BLOCK), g * SLOT:(g + 1) * SLOT])
        d = masked_dist(j)
        for hh in range(A_GROUP):
            h = A_GROUP * g + hh
            r = slice(hh * BLOCK, (hh + 1) * BLOCK)
            lg = s[r, :] - (A_SLOPES[h] * LOG2E) * d
            lg_ref[r, :] = lg
            mx = jnp.maximum(jnp.max(lg, axis=-1, keepdims=True), sink_ref[h])
            m_ref[r, :] = jnp.broadcast_to(mx, (BLOCK, LANES))

    def attend(j, g, lg_ref, m_ref):
        rows = slice(j * BLOCK, (j + 1) * BLOCK)
        m = m_ref[...]
        e = jnp.concatenate([jnp.exp2(lg_ref[:, n * LANES:(n + 1) * LANES] - m).astype(BF16)
                             for n in range(3)], axis=1)
        pv = _dot(e, vbuf[pl.ds(krow_m + j * BLOCK, 3 * BLOCK), g * SLOT:(g + 1) * SLOT])
        outs = []
        for hh in range(A_GROUP):
            r = slice(hh * BLOCK, (hh + 1) * BLOCK)
            denom = (jnp.broadcast_to(pv[r, A_HEAD_DIM:A_HEAD_DIM + 1], (BLOCK, LANES))
                     + jnp.exp2(sink_ref[A_GROUP * g + hh] - m[r, :]))
            outs.append(pv[r, :] / denom)
        for pr in range(A_GROUP // 2):
            c0 = (A_GROUP // 2 * g + pr) * LANES
            ybuf[rows, c0:c0 + LANES] = jnp.where(
                lane < A_HEAD_DIM, outs[2 * pr], pltpu.roll(outs[2 * pr + 1], A_HEAD_DIM, axis=1))

    chains = [(j, g) for j in range(sub) for g in range(A_KV_HEADS)]
    bufs = [(lg_a, m_a), (lg_b, m_b)]
    scores(*chains[0], *bufs[0])
    late = [pieces.pop() for _ in range(4)][::-1]
    for c, (j, g) in enumerate(chains):
        if c + 1 < len(chains):
            scores(*chains[c + 1], *bufs[(c + 1) % 2])
        attend(j, g, *bufs[c % 2])
        pieces.pop(0)()
    assert not pieces

    late[0]()
    cz = mcz[...]
    row = lax.broadcasted_iota(jnp.int32, cz.shape, 0)
    prev_row = jnp.where(i > 0, before_rows[7:8, :], 0.0)
    next_row = jnp.where(i < tiles_per_seq - 1, pcz[0:1, :], 0.0)
    z_prev = jnp.where(row == 0, prev_row, pltpu.roll(cz, 1, axis=0))
    z_next = jnp.where(row == tile - 1, next_row, pltpu.roll(cz, tile - 1, axis=0))
    conv = (convw_ref[0:1, :] * z_prev + convw_ref[1:2, :] * cz
            + convw_ref[2:3, :] * z_next)
    yb = mbg[...] * conv

    late[1]()
    groups = [slice(r, r + tile // 2) for r in range(0, tile, tile // 2)]
    ya_n = [(_rms(ybuf[r, :], anorm_ref[...]) * msz[r, 0:A_WIDTH]).astype(BF16) for r in groups]
    yb_n = [(_rms(yb[r, :], bnorm_ref[...]) * msz[r, A_WIDTH:]).astype(BF16) for r in groups]
    emb = [_dot(p_ref[r, :].astype(BF16), proj_ref[...]) for r in groups]
    mix = [_dot(a, wout_ref[0:A_WIDTH, :]) + _dot(b, wout_ref[A_WIDTH:, :])
           for a, b in zip(ya_n, yb_n)]
    late[2]()
    h = [_layer_norm(DEEPNORM_ALPHA * xp_ref[r, :] + m, g_ref[...], b_ref[...])
         for m, r in zip(mix, groups)]
    late[3]()
    gate = [jax.nn.sigmoid(_dot(hh.astype(BF16), gate_ref[...])) for hh in h]
    for r, hh, gt, e in zip(groups, h, gate, emb):
        o_ref[r, :] = hh + gt * e


def _odd_proj_kernel(x_ref, pos_ref, w_ref, inv_ref, qn_ref, wq_ref, wqs_ref,
                     kvn_ref, wkv_ref, vones_ref,
                     lng_ref, lnb_ref, ws_ref, bs_ref, dnorm_ref,
                     q_ref, k_ref, v_ref, yd_ref, szc_ref):
    tile = x_ref.shape[0]
    wide = C_HEADS * SLOT
    xb = x_ref[...].astype(BF16)

    def seg(i):
        return _dot(xb, w_ref[:, _OD_OFF[i]:_OD_OFF[i + 1]])

    packs = LANES // C_ROPE
    rb = tile // packs
    lane = lax.broadcasted_iota(jnp.int32, (rb, LANES), 1)
    pos_f = pos_ref[...].astype(F32)
    packed = jnp.broadcast_to(pos_f[0:rb, :], (rb, LANES))
    for u in range(1, packs):
        packed = jnp.where(lane >= u * C_ROPE,
                           jnp.broadcast_to(pos_f[u * rb:(u + 1) * rb, :], (rb, LANES)), packed)
    ang = packed * inv_ref[...]
    cos_p = jnp.cos(ang)
    sin_p = jnp.sin(ang)
    rope_lane = (lane >= C_NOPE) & (lane < C_NOPE + C_ROPE)
    cos_blocks, sin_blocks = [], []
    for u in range(packs):
        shift = (C_NOPE - u * C_ROPE) % LANES
        cu = pltpu.roll(cos_p, shift, axis=1) if shift else cos_p
        su = pltpu.roll(sin_p, shift, axis=1) if shift else sin_p
        cos_blocks.append(jnp.where(rope_lane, cu, 1.0))
        sin_blocks.append(jnp.where(rope_lane, su, 0.0))
    cos_t = jnp.concatenate(cos_blocks, axis=0)
    sin_t = jnp.concatenate(sin_blocks, axis=0)

    a = seg(0)
    kr = seg(1)
    cqn = _rms(a[:, 0:C_Q_RANK], qn_ref[...]).astype(BF16)
    ckvn = _rms(a[:, C_Q_RANK:], kvn_ref[...]).astype(BF16)
    du = seg(2)
    qa = _dot(cqn, wq_ref[...])
    qs = _dot(cqn, wqs_ref[...])
    gu = _gelu_tanh(du)
    dv = seg(3)
    for h in range(C_HEADS):
        sl = slice(h * SLOT, (h + 1) * SLOT)
        q_ref[:, sl] = ((qa[:, sl] * cos_t + qs[:, sl] * sin_t) * MLA_SCALE_LOG2E).astype(BF16)

    kr_rot = (kr[:, 0:LANES] * cos_t + kr[:, LANES:] * sin_t).astype(BF16)
    kv = _dot(jnp.concatenate([ckvn, kr_rot], axis=1), wkv_ref[...])
    vn = _layer_norm(_gelu_tanh(dv), lng_ref[...], lnb_ref[...]).astype(BF16)
    z = seg(4)
    k_ref[...] = kv[:, 0:wide].astype(BF16)
    v_ref[...] = (kv[:, wide:] + vones_ref[...]).astype(BF16)

    n_ch = tile // D_CHUNK
    mixed = []
    for g in range(D_GROUPS):
        cols = slice(g * D_GROUP_DIM, (g + 1) * D_GROUP_DIM)
        rhs = jnp.concatenate([vn[c * D_CHUNK:(c + 1) * D_CHUNK, cols] for c in range(n_ch)], axis=1)
        mixed.append(_dot(ws_ref[g], rhs))
    sz = _silu(z)
    szc_ref[...] = sz[:, 0:C_WIDTH]
    chunks = [jnp.concatenate([mixed[g][:, c * D_GROUP_DIM:(c + 1) * D_GROUP_DIM]
                               for g in range(D_GROUPS)], axis=1) + bs_ref[...]
              for c in range(n_ch)]
    yd = gu * jnp.concatenate(chunks, axis=0)
    yd_ref[...] = (_rms(yd, dnorm_ref[...]) * sz[:, C_WIDTH:]).astype(BF16)


def _mla_kernel(seq_len, tq, tk, unroll, q_ref, k_ref, v_ref, o_ref, s_a, s_b, m_a, m_b):
    n_qt = seq_len // tq
    n_c = seq_len // tk

    def scores(qt, hh, s_ref, m_ref):
        sl = slice(hh * SLOT, (hh + 1) * SLOT)
        q = q_ref[pl.ds(pl.multiple_of(qt * tq, tq), tq), sl]
        part = None
        for c in range(n_c):
            s = _dot_nt(q, k_ref[c * tk:(c + 1) * tk, sl])
            s_ref[:, c * tk:(c + 1) * tk] = s
            tiles = [s[:, j * LANES:(j + 1) * LANES] for j in range(tk // LANES)]
            while len(tiles) > 1:
                tiles = [jnp.maximum(a, b) for a, b in zip(tiles[0::2], tiles[1::2])]
            part = tiles[0] if part is None else jnp.maximum(part, tiles[0])
        m_ref[...] = jnp.broadcast_to(jnp.max(part, axis=-1, keepdims=True), m_ref.shape)

    def attend(hh, s_ref, m_ref):
        sl = slice(hh * SLOT, (hh + 1) * SLOT)
        m = m_ref[...]
        acc = None
        for c in range(n_c):
            p = jnp.concatenate(
                [jnp.exp2(s_ref[:, c * tk + j * LANES:c * tk + (j + 1) * LANES] - m).astype(BF16)
                 for j in range(tk // LANES)], axis=1)
            d = _dot(p, v_ref[c * tk:(c + 1) * tk, sl])
            acc = d if acc is None else acc + d
        return acc / acc[:, C_V:C_V + 1]

    scores(0, 0, s_a, m_a)

    def one_tile(qt):
        scores(qt, 1, s_b, m_b)
        out0 = attend(0, s_a, m_a)
        scores(jnp.minimum(qt + 1, n_qt - 1), 0, s_a, m_a)
        out1 = attend(1, s_b, m_b)
        lane = lax.broadcasted_iota(jnp.int32, out0.shape, 1)
        o_ref[pl.ds(pl.multiple_of(qt * tq, tq), tq), :] = jnp.where(
            lane < C_V, out0, pltpu.roll(out1, C_V, axis=1))

    def body(it, carry):
        for u in range(unroll):
            one_tile(it * unroll + u)
        return carry

    lax.fori_loop(0, n_qt // unroll, body, 0)


def _odd_out_kernel(yc_ref, szc_ref, yd_ref, x_ref, p_ref, cnorm_ref, wout_ref,
                    g_ref, b_ref, gate_ref, proj_ref, o_ref):
    tile = x_ref.shape[0]
    groups = [slice(r, r + tile // OUT_SPLIT) for r in range(0, tile, tile // OUT_SPLIT)]
    yc_n = [(_rms(yc_ref[r, :], cnorm_ref[...]) * szc_ref[r, :]).astype(BF16) for r in groups]
    emb = [_dot(p_ref[r, :].astype(BF16), proj_ref[...]) for r in groups]
    mix = [_dot(y, wout_ref[0:C_WIDTH, :]) + _dot(yd_ref[r, :], wout_ref[C_WIDTH:, :])
           for y, r in zip(yc_n, groups)]
    h = [_layer_norm(DEEPNORM_ALPHA * x_ref[r, :] + m, g_ref[...], b_ref[...])
         for m, r in zip(mix, groups)]
    gate = [jax.nn.sigmoid(_dot(hh.astype(BF16), gate_ref[...])) for hh in h]
    for r, hh, gt, e in zip(groups, h, gate, emb):
        o_ref[r, :] = hh + gt * e


def _full(shape):
    return pl.BlockSpec(shape, lambda *_: (0,) * len(shape))


def _once(shape):
    return pl.BlockSpec(shape, lambda *_: (0,) * len(shape), pipeline_mode=pl.Buffered(1))


def _params(sem):
    return pltpu.CompilerParams(dimension_semantics=sem, vmem_limit_bytes=VMEM_LIMIT)


def _even_layer(x, p, layer, positions, w_in, conv_w, sink, a_norm, b_norm, w_out,
                ln_g, ln_b, ple_proj, ple_gate):
    bsz, seq, _ = x.shape
    n_tok = bsz * seq
    tile = 512
    n_tiles = n_tok // tile
    tiles_per_seq = seq // tile
    sub = tile // BLOCK
    nb = seq // BLOCK
    q_w = A_HEADS * SLOT
    kv_w = A_KV_HEADS * SLOT
    x2 = x.reshape(n_tok, D_MODEL)
    pos_col = positions.reshape(n_tok, 1)
    pos_row = positions.reshape(bsz, 1, seq)

    cur = lambda t: jnp.minimum(t, n_tiles - 1)
    prv = lambda t: jnp.maximum(t - 1, 0)
    seq_of = lambda t: prv(t) // tiles_per_seq
    blk_of = lambda t: (prv(t) % tiles_per_seq) * sub
    in_specs = [
        pl.BlockSpec((tile, D_MODEL), lambda t: (cur(t), 0)),
        pl.BlockSpec((tile, D_MODEL), lambda t: (prv(t), 0)),
        pl.BlockSpec((None, tile, PLE_DIM), lambda t: (layer, prv(t), 0)),
        pl.BlockSpec((tile, 1), lambda t: (prv(t), 0)),
        pl.BlockSpec((None, 1, tile), lambda t: (seq_of(t), 0, prv(t) % tiles_per_seq)),
        pl.BlockSpec((None, 1, BLOCK), lambda t: (seq_of(t), 0, jnp.maximum(blk_of(t) - 1, 0))),
        pl.BlockSpec((None, 1, BLOCK), lambda t: (seq_of(t), 0, jnp.minimum(blk_of(t) + sub, nb - 1))),
        _once((D_MODEL, _EV_OFF[-1])),
        _once((3, B_WIDTH)),
        pl.BlockSpec(memory_space=pltpu.SMEM),
        _once((1, A_WIDTH)), _once((1, B_WIDTH)), _once((D_MODEL, D_MODEL)),
        _once((1, D_MODEL)), _once((1, D_MODEL)), _once((D_MODEL, D_MODEL)),
        _once((PLE_DIM, D_MODEL)),
    ]
    out = pl.pallas_call(
        functools.partial(_even_kernel, seq, tile),
        grid=(n_tiles + 1,),
        in_specs=in_specs,
        out_specs=pl.BlockSpec((tile, D_MODEL), lambda t: (prv(t), 0)),
        out_shape=jax.ShapeDtypeStruct((n_tok, D_MODEL), F32),
        scratch_shapes=[pltpu.VMEM((seq + 2 * BLOCK, kv_w), BF16),
                        pltpu.VMEM((seq + 2 * BLOCK, kv_w), BF16),
                        pltpu.VMEM((tile, q_w), BF16), pltpu.VMEM((tile, q_w), BF16),
                        pltpu.VMEM((tile, B_WIDTH), F32), pltpu.VMEM((tile, B_WIDTH), F32),
                        pltpu.VMEM((tile, A_WIDTH + B_WIDTH), F32),
                        pltpu.VMEM((tile, A_WIDTH + B_WIDTH), F32),
                        pltpu.VMEM((tile, B_WIDTH), F32), pltpu.VMEM((tile, B_WIDTH), F32),
                        pltpu.VMEM((1, tile + 2 * BLOCK), jnp.int32),
                        pltpu.VMEM((tile, A_WIDTH), F32),
                        pltpu.VMEM((A_GROUP * BLOCK, 3 * BLOCK), F32),
                        pltpu.VMEM((A_GROUP * BLOCK, 3 * BLOCK), F32),
                        pltpu.VMEM((A_GROUP * BLOCK, LANES), F32),
                        pltpu.VMEM((A_GROUP * BLOCK, LANES), F32)],
        compiler_params=_params(("arbitrary",)),
        name="even_layer",
    )(x2, x2, p.reshape(DEPTH, n_tok, PLE_DIM), pos_col, pos_row, pos_row, pos_row,
      w_in.astype(BF16), conv_w, sink * LOG2E, a_norm.reshape(1, -1), b_norm.reshape(1, -1),
      w_out.astype(BF16), ln_g.reshape(1, -1), ln_b.reshape(1, -1),
      ple_gate.astype(BF16), ple_proj.astype(BF16))
    return out.reshape(bsz, seq, D_MODEL)


def _swap_halves(r):
    half = r.shape[-1] // 2
    return jnp.concatenate([-r[..., half:], r[..., :half]], axis=-1)


def _odd_weights(w_in, w_uq, w_ukv, b_s):
    d = w_in.shape[0]
    o_kr = C_Q_RANK + C_KV_RANK
    o_du = o_kr + C_ROPE
    w_kr = w_in[:, o_kr:o_du]
    kr_pair = jnp.zeros((d, 2 * LANES), F32)
    kr_pair = kr_pair.at[:, C_NOPE:C_NOPE + C_ROPE].set(w_kr)
    kr_pair = kr_pair.at[:, LANES + C_NOPE:LANES + C_NOPE + C_ROPE].set(_swap_halves(w_kr))
    w1 = jnp.concatenate([w_in[:, :o_kr], kr_pair, w_in[:, o_du:]], axis=1).astype(BF16)

    uq = w_uq.reshape(C_Q_RANK, C_HEADS, C_NOPE + C_ROPE)
    pad = SLOT - C_NOPE - C_ROPE
    wq = jnp.pad(uq, ((0, 0), (0, 0), (0, pad)))
    wqs = jnp.pad(_swap_halves(uq[..., C_NOPE:]), ((0, 0), (0, 0), (C_NOPE, pad)))
    ukv = w_ukv.reshape(C_KV_RANK, C_HEADS, C_NOPE + C_V)
    wkn = jnp.pad(ukv[..., :C_NOPE], ((0, 0), (0, 0), (0, SLOT - C_NOPE)))
    wv = jnp.pad(ukv[..., C_NOPE:], ((0, 0), (0, 0), (0, SLOT - C_V)))
    flat = lambda w: w.reshape(w.shape[0], C_HEADS * SLOT).astype(BF16)

    lane = jnp.arange(LANES)
    rope_lane = (lane >= C_NOPE) & (lane < C_NOPE + C_ROPE)
    wke = jnp.tile(jnp.where(rope_lane[:, None] & (lane[:, None] == lane[None, :]), 1.0, 0.0),
                   (1, C_HEADS)).astype(BF16)
    vones = jnp.tile(jnp.where(lane == C_V, 1.0, 0.0), C_HEADS).reshape(1, -1).astype(F32)
    half = C_ROPE // 2
    inv = ROPE_THETA ** (-jnp.arange(half, dtype=F32) / half)
    inv_lane = inv[(lane % C_ROPE) % half].reshape(1, LANES).astype(F32)
    bs_full = jnp.repeat(b_s.T, D_GROUP_DIM, axis=1)
    wkv = jnp.concatenate([jnp.concatenate([flat(wkn), flat(wv)], axis=1),
                           jnp.concatenate([wke, jnp.zeros_like(wke)], axis=1)], axis=0)
    return w1, flat(wq), flat(wqs), wkv, vones, inv_lane, bs_full


def _odd_layer(x, p, layer, positions, w_in, q_norm, w_uq, kv_norm, w_ukv, v_ln_g, v_ln_b,
               w_s, b_s, c_norm, d_norm, w_out, ln_g, ln_b, ple_proj, ple_gate):
    bsz, seq, _ = x.shape
    n_tok = bsz * seq
    w1, wq, wqs, wkv, vones, inv_lane, bs_full = _odd_weights(w_in, w_uq, w_ukv, b_s)
    tm = 512
    row_spec = lambda w: pl.BlockSpec((tm, w), lambda t: (t, 0))
    wide = C_HEADS * SLOT
    q, k, v, yd, szc = pl.pallas_call(
        _odd_proj_kernel,
        grid=(n_tok // tm,),
        in_specs=[row_spec(D_MODEL), row_spec(1), _full(w1.shape), _full((1, LANES)),
                  _full((1, C_Q_RANK)), _full(wq.shape), _full(wqs.shape),
                  _full((1, C_KV_RANK)), _full(wkv.shape),
                  _full((1, wide)), _full((1, D_WIDTH)), _full((1, D_WIDTH)),
                  _full((D_GROUPS, D_CHUNK, D_CHUNK)), _full((D_CHUNK, D_WIDTH)),
                  _full((1, D_WIDTH))],
        out_specs=[row_spec(wide), row_spec(wide), row_spec(wide), row_spec(D_WIDTH),
                   row_spec(C_WIDTH)],
        out_shape=[jax.ShapeDtypeStruct((n_tok, wide), BF16),
                   jax.ShapeDtypeStruct((n_tok, wide), BF16),
                   jax.ShapeDtypeStruct((n_tok, wide), BF16),
                   jax.ShapeDtypeStruct((n_tok, D_WIDTH), BF16),
                   jax.ShapeDtypeStruct((n_tok, C_WIDTH), F32)],
        compiler_params=_params(("parallel",)),
        name="odd_proj",
    )(x.reshape(n_tok, D_MODEL), positions.reshape(n_tok, 1), w1, inv_lane,
      q_norm.reshape(1, -1), wq, wqs, kv_norm.reshape(1, -1), wkv, vones,
      v_ln_g.reshape(1, -1), v_ln_b.reshape(1, -1), w_s.astype(BF16), bs_full,
      d_norm.reshape(1, -1))

    tq = 256
    tk = 512
    q = q.reshape(bsz, seq, wide)
    k = k.reshape(bsz, seq, wide)
    v = v.reshape(bsz, seq, wide)
    pair = lambda w: pl.BlockSpec((None, seq, w), lambda b, h: (b, 0, h))
    yc = pl.pallas_call(
        functools.partial(_mla_kernel, seq, tq, tk, 4),
        grid=(bsz, C_HEADS // 2),
        in_specs=[pair(2 * SLOT), pair(2 * SLOT), pair(2 * SLOT)],
        out_specs=pair(2 * C_V),
        out_shape=jax.ShapeDtypeStruct((bsz, seq, C_WIDTH), F32),
        scratch_shapes=[pltpu.VMEM((tq, seq), F32), pltpu.VMEM((tq, seq), F32),
                        pltpu.VMEM((tq, LANES), F32), pltpu.VMEM((tq, LANES), F32)],
        compiler_params=_params(("parallel", "parallel")),
        name="mla_attn",
    )(q, k, v)

    to = 1024
    out_spec = lambda w: pl.BlockSpec((to, w), lambda t: (t, 0))
    out = pl.pallas_call(
        _odd_out_kernel,
        grid=(n_tok // to,),
        in_specs=[out_spec(C_WIDTH), out_spec(C_WIDTH), out_spec(D_WIDTH), out_spec(D_MODEL),
                  pl.BlockSpec((None, to, PLE_DIM), lambda t: (layer, t, 0)),
                  _once((1, C_WIDTH)), _once((D_MODEL, D_MODEL)), _once((1, D_MODEL)),
                  _once((1, D_MODEL)), _once((D_MODEL, D_MODEL)), _once((PLE_DIM, D_MODEL))],
        out_specs=out_spec(D_MODEL),
        out_shape=jax.ShapeDtypeStruct((n_tok, D_MODEL), F32),
        compiler_params=_params(("parallel",)),
        name="odd_out",
    )(yc.reshape(n_tok, C_WIDTH), szc, yd, x.reshape(n_tok, D_MODEL),
      p.reshape(DEPTH, n_tok, PLE_DIM), c_norm.reshape(1, -1), w_out.astype(BF16),
      ln_g.reshape(1, -1), ln_b.reshape(1, -1), ple_gate.astype(BF16), ple_proj.astype(BF16))
    return out.reshape(bsz, seq, D_MODEL)


def kernel(x, p, positions, ev_w_in, ev_conv_w, ev_sink, ev_a_norm, ev_b_norm, ev_w_out, od_w_in, od_q_norm, od_w_uq, od_kv_norm, od_w_ukv, od_v_ln_g, od_v_ln_b, od_w_s, od_b_s, od_c_norm, od_d_norm, od_w_out, post_ln_g, post_ln_b, ple_proj, ple_gate):
    for i in range(DEPTH):
        j = i // 2
        if i % 2 == 0:
            x = _even_layer(x, p, i, positions, ev_w_in[j], ev_conv_w[j], ev_sink[j],
                            ev_a_norm[j], ev_b_norm[j], ev_w_out[j],
                            post_ln_g[i], post_ln_b[i], ple_proj[i], ple_gate[i])
        else:
            x = _odd_layer(x, p, i, positions, od_w_in[j], od_q_norm[j], od_w_uq[j],
                           od_kv_norm[j], od_w_ukv[j], od_v_ln_g[j], od_v_ln_b[j],
                           od_w_s[j], od_b_s[j], od_c_norm[j], od_d_norm[j], od_w_out[j],
                           post_ln_g[i], post_ln_b[i], ple_proj[i], ple_gate[i])
    return x
```

```python
import functools
import math

import jax
import jax.numpy as jnp
from jax import lax
from jax.experimental import pallas as pl
from jax.experimental.pallas import tpu as pltpu

F32 = jnp.float32
BF16 = jnp.bfloat16

D_MODEL = 1024
DEPTH = 2
PLE_DIM = 256
DEEPNORM_ALPHA = (2.0 * DEPTH) ** 0.25
BLOCK = 128
EPS = 1e-6
A_HEAD_DIM = 64
A_HEADS = 8
A_KV_HEADS = 2
A_GROUP = A_HEADS // A_KV_HEADS
A_WIDTH = A_HEADS * A_HEAD_DIM
WINDOW = 128
B_WIDTH = 512
C_HEADS = 8
C_NOPE = 64
C_ROPE = 32
C_V = 64
C_WIDTH = C_HEADS * C_V
C_Q_RANK = 256
C_KV_RANK = 128
ROPE_THETA = 10000.0
D_WIDTH = 512
D_GROUPS = 4
D_GROUP_DIM = D_WIDTH // D_GROUPS
D_CHUNK = 128

LANES = 128
SLOT = LANES
VMEM_LIMIT = 56 * 1024 * 1024
OUT_SPLIT = 4

_EV_OFF = (0, 512, 640, 768, 1280, 1792, 2304, 3328)
_OD_OFF = (0, 384, 640, 1152, 1664, 2688)

A_SLOPES = tuple(2.0 ** (-8.0 * (h + 1) / A_HEADS) for h in range(A_HEADS))
LOG2E = math.log2(math.e)
A_SCALE_LOG2E = A_HEAD_DIM ** -0.5 * LOG2E
MLA_SCALE_LOG2E = (C_NOPE + C_ROPE) ** -0.5 * LOG2E


def _dot(a, b):
    return jnp.dot(a, b, preferred_element_type=F32)


def _dot_nt(a, b):
    return lax.dot_general(a, b, (((1,), (1,)), ((), ())), preferred_element_type=F32)


def _column(row):
    return jnp.transpose(jnp.broadcast_to(row, (LANES, LANES)))


def _rms(x, g):
    return x * lax.rsqrt(jnp.mean(x * x, axis=-1, keepdims=True) + EPS) * g


def _layer_norm(x, g, b):
    mu = jnp.mean(x, axis=-1, keepdims=True)
    xc = x - mu
    var = jnp.mean(xc * xc, axis=-1, keepdims=True)
    return xc * lax.rsqrt(var + EPS) * g + b


def _silu(z):
    return z * jax.nn.sigmoid(z)


def _gelu_tanh(x):
    c = math.sqrt(2.0 / math.pi)
    return 0.5 * x * (1.0 + jnp.tanh(c * (x + 0.044715 * (x * x * x))))


def _post_block(y_bf16_dot, x, p, g_ref, b_ref, gate_ref, proj_ref):
    h = _layer_norm(DEEPNORM_ALPHA * x + y_bf16_dot, g_ref[...], b_ref[...])
    gate = jax.nn.sigmoid(_dot(h.astype(BF16), gate_ref[...]))
    return h + gate * _dot(p.astype(BF16), proj_ref[...])


def _even_kernel(seq_len, tile,
                 xc_ref, xp_ref, p_ref, prc_ref, prp_ref, prn_ref,
                 win_ref, convw_ref, sink_ref, anorm_ref, bnorm_ref, wout_ref,
                 g_ref, b_ref, gate_ref, proj_ref,
                 o_ref,
                 kbuf, vbuf, q_a, q_b, bg_a, bg_b, sz_a, sz_b, cz_a, cz_b,
                 posbuf, ybuf, lg_a, lg_b, m_a, m_b):
    t = pl.program_id(0)

    @pl.when(t == 0)
    def _():
        for ref in (kbuf, vbuf, q_a, q_b, bg_a, bg_b, sz_a, sz_b, cz_a, cz_b):
            ref[...] = jnp.zeros(ref.shape, ref.dtype)

    shared = (seq_len, tile, xc_ref, xp_ref, p_ref, prc_ref, prp_ref, prn_ref,
              win_ref, convw_ref, sink_ref, anorm_ref, bnorm_ref, wout_ref,
              g_ref, b_ref, gate_ref, proj_ref, o_ref,
              kbuf, vbuf, posbuf, ybuf, lg_a, lg_b, m_a, m_b)

    @pl.when(t % 2 == 0)
    def _():
        _even_step(*shared, (q_a, bg_a, sz_a, cz_a), (q_b, bg_b, sz_b, cz_b))

    @pl.when(t % 2 == 1)
    def _():
        _even_step(*shared, (q_b, bg_b, sz_b, cz_b), (q_a, bg_a, sz_a, cz_a))


def _even_step(seq_len, tile, xc_ref, xp_ref, p_ref, prc_ref, prp_ref, prn_ref,
               win_ref, convw_ref, sink_ref, anorm_ref, bnorm_ref, wout_ref,
               g_ref, b_ref, gate_ref, proj_ref, o_ref,
               kbuf, vbuf, posbuf, ybuf, lg_a, lg_b, m_a, m_b, proj_bufs, mix_bufs):
    t = pl.program_id(0)
    tiles_per_seq = seq_len // tile
    sub = tile // BLOCK
    pq, pbg, psz, pcz = proj_bufs
    mq, mbg, msz, mcz = mix_bufs

    before_rows = pcz[tile - 8:tile, :]

    xb = xc_ref[...].astype(BF16)

    def seg(n, lo=0, hi=None):
        hi = _EV_OFF[n + 1] - _EV_OFF[n] if hi is None else hi
        return _dot(xb, win_ref[:, _EV_OFF[n] + lo:_EV_OFF[n] + hi])

    krow_p = pl.multiple_of(BLOCK + (t % tiles_per_seq) * tile, BLOCK)

    lane_t = lax.broadcasted_iota(jnp.int32, (tile, LANES), 1)
    low = lane_t < A_HEAD_DIM
    one = jnp.where(lane_t == A_HEAD_DIM, 1.0, 0.0)

    def slots(v, fill):
        return (jnp.where(low, v, fill),
                jnp.where(low, pltpu.roll(v, A_HEAD_DIM, axis=1), fill))

    def proj_kv():
        lo, hi = slots(seg(1), 0.0)
        kbuf[pl.ds(krow_p, tile), 0:SLOT] = lo.astype(BF16)
        kbuf[pl.ds(krow_p, tile), SLOT:] = hi.astype(BF16)
        lo, hi = slots(seg(2), one)
        vbuf[pl.ds(krow_p, tile), 0:SLOT] = lo.astype(BF16)
        vbuf[pl.ds(krow_p, tile), SLOT:] = hi.astype(BF16)

    piece_w = 2 * LANES

    def proj_q(part):
        q = seg(0, part * piece_w, (part + 1) * piece_w) * A_SCALE_LOG2E
        for n in range(piece_w // LANES):
            lo, hi = slots(q[:, n * LANES:(n + 1) * LANES], 0.0)
            h0 = 2 * (part * (piece_w // LANES) + n)
            pq[:, h0 * SLOT:(h0 + 1) * SLOT] = lo.astype(BF16)
            pq[:, (h0 + 1) * SLOT:(h0 + 2) * SLOT] = hi.astype(BF16)

    def proj_bg(part):
        lo, hi = part * piece_w, (part + 1) * piece_w
        pbg[:, lo:hi] = seg(3, lo, hi)

    def proj_cg(part):
        lo, hi = part * piece_w, (part + 1) * piece_w
        pcz[:, lo:hi] = seg(4, lo, hi)

    def proj_xin(part):
        lo, hi = part * piece_w, (part + 1) * piece_w
        pcz[:, lo:hi] = pcz[:, lo:hi] * seg(5, lo, hi)

    def proj_sz(part):
        lo, hi = part * piece_w, (part + 1) * piece_w
        psz[:, lo:hi] = _silu(seg(6, lo, hi))

    proj_kv()
    pieces = [functools.partial(fn, part)
              for fn, parts in ((proj_q, 2), (proj_bg, 2), (proj_cg, 2), (proj_xin, 2), (proj_sz, 4))
              for part in range(parts)]

    i = jnp.maximum(t - 1, 0) % tiles_per_seq
    krow_m = pl.multiple_of(i * tile, BLOCK)

    posbuf[:, 0:BLOCK] = prp_ref[...]
    posbuf[:, BLOCK:BLOCK + tile] = prc_ref[...]
    posbuf[:, BLOCK + tile:] = prn_ref[...]

    q_idx = lax.broadcasted_iota(jnp.int32, (BLOCK, 3 * BLOCK), 0)
    w_idx = lax.broadcasted_iota(jnp.int32, (BLOCK, 3 * BLOCK), 1)
    in_band = jnp.abs(w_idx - BLOCK - q_idx) <= WINDOW
    lane = lax.broadcasted_iota(jnp.int32, (BLOCK, LANES), 1)
    dists = {}

    def masked_dist(j):
        if j not in dists:
            k_glob = (i * sub + j) * BLOCK + w_idx - BLOCK
            valid = in_band & (k_glob >= 0) & (k_glob < seq_len)
            pos_q = _column(posbuf[:, (j + 1) * BLOCK:(j + 2) * BLOCK])
            d = jnp.concatenate([jnp.abs(pos_q - posbuf[:, (j + n) * BLOCK:(j + n + 1) * BLOCK])
                                 for n in range(3)], axis=1).astype(F32)
            dists[j] = jnp.where(valid, d, jnp.inf)
        return dists[j]

    def scores(j, g, lg_ref, m_ref):
        qrow = slice(j * BLOCK, (j + 1) * BLOCK)
        qg = jnp.concatenate([mq[qrow, (A_GROUP * g + hh) * SLOT:(A_GROUP * g + hh + 1) * SLOT]
                              for hh in range(A_GROUP)], axis=0)
        s = _dot_nt(qg, kbuf[pl.ds(krow_m + j * BLOCK, 3 * BLOCK), g * SLOT:(g + 1) * SLOT])
        d = masked_dist(j)
        for hh in range(A_GROUP):
            h = A_GROUP * g + hh
            r = slice(hh * BLOCK, (hh + 1) * BLOCK)
            lg = s[r, :] - (A_SLOPES[h] * LOG2E) * d
            lg_ref[r, :] = lg
            mx = jnp.maximum(jnp.max(lg, axis=-1, keepdims=True), sink_ref[h])
            m_ref[r, :] = jnp.broadcast_to(mx, (BLOCK, LANES))

    def attend(j, g, lg_ref, m_ref):
        rows = slice(j * BLOCK, (j + 1) * BLOCK)
        m = m_ref[...]
        e = jnp.concatenate([jnp.exp2(lg_ref[:, n * LANES:(n + 1) * LANES] - m).astype(BF16)
                             for n in range(3)], axis=1)
        pv = _dot(e, vbuf[pl.ds(krow_m + j * BLOCK, 3 * BLOCK), g * SLOT:(g + 1) * SLOT])
        outs = []
        for hh in range(A_GROUP):
            r = slice(hh * BLOCK, (hh + 1) * BLOCK)
            denom = (jnp.broadcast_to(pv[r, A_HEAD_DIM:A_HEAD_DIM + 1], (BLOCK, LANES))
                     + jnp.exp2(sink_ref[A_GROUP * g + hh] - m[r, :]))
            outs.append(pv[r, :] / denom)
        for pr in range(A_GROUP // 2):
            c0 = (A_GROUP // 2 * g + pr) * LANES
            ybuf[rows, c0:c0 + LANES] = jnp.where(
                lane < A_HEAD_DIM, outs[2 * pr], pltpu.roll(outs[2 * pr + 1], A_HEAD_DIM, axis=1))

    chains = [(j, g) for j in range(sub) for g in range(A_KV_HEADS)]
    bufs = [(lg_a, m_a), (lg_b, m_b)]
    scores(*chains[0], *bufs[0])
    late = [pieces.pop() for _ in range(4)][::-1]
    for c, (j, g) in enumerate(chains):
        if c + 1 < len(chains):
            scores(*chains[c + 1], *bufs[(c + 1) % 2])
        attend(j, g, *bufs[c % 2])
        pieces.pop(0)()
    assert not pieces

    late[0]()
    cz = mcz[...]
    row = lax.broadcasted_iota(jnp.int32, cz.shape, 0)
    prev_row = jnp.where(i > 0, before_rows[7:8, :], 0.0)
    next_row = jnp.where(i < tiles_per_seq - 1, pcz[0:1, :], 0.0)
    z_prev = jnp.where(row == 0, prev_row, pltpu.roll(cz, 1, axis=0))
    z_next = jnp.where(row == tile - 1, next_row, pltpu.roll(cz, tile - 1, axis=0))
    conv = (convw_ref[0:1, :] * z_prev + convw_ref[1:2, :] * cz
            + convw_ref[2:3, :] * z_next)
    yb = mbg[...] * conv

    late[1]()
    groups = [slice(r, r + tile // 2) for r in range(0, tile, tile // 2)]
    ya_n = [(_rms(ybuf[r, :], anorm_ref[...]) * msz[r, 0:A_WIDTH]).astype(BF16) for r in groups]
    yb_n = [(_rms(yb[r, :], bnorm_ref[...]) * msz[r, A_WIDTH:]).astype(BF16) for r in groups]
    emb = [_dot(p_ref[r, :].astype(BF16), proj_ref[...]) for r in groups]
    mix = [_dot(a, wout_ref[0:A_WIDTH, :]) + _dot(b, wout_ref[A_WIDTH:, :])
           for a, b in zip(ya_n, yb_n)]
    late[2]()
    h = [_layer_norm(DEEPNORM_ALPHA * xp_ref[r, :] + m, g_ref[...], b_ref[...])
         for m, r in zip(mix, groups)]
    late[3]()
    gate = [jax.nn.sigmoid(_dot(hh.astype(BF16), gate_ref[...])) for hh in h]
    for r, hh, gt, e in zip(groups, h, gate, emb):
        o_ref[r, :] = hh + gt * e


def _odd_proj_kernel(x_ref, pos_ref, w_ref, inv_ref, qn_ref, wq_ref, wqs_ref,
                     kvn_ref, wkv_ref, vones_ref,
                     lng_ref, lnb_ref, ws_ref, bs_ref, dnorm_ref,
                     q_ref, k_ref, v_ref, yd_ref, szc_ref):
    tile = x_ref.shape[0]
    wide = C_HEADS * SLOT
    xb = x_ref[...].astype(BF16)

    def seg(i):
        return _dot(xb, w_ref[:, _OD_OFF[i]:_OD_OFF[i + 1]])

    packs = LANES // C_ROPE
    rb = tile // packs
    lane = lax.broadcasted_iota(jnp.int32, (rb, LANES), 1)
    packed = _column(pos_ref[0]).astype(F32)
    for u in range(1, packs):
        packed = jnp.where(lane >= u * C_ROPE, _column(pos_ref[u]).astype(F32), packed)
    ang = packed * inv_ref[...]
    cos_p = jnp.cos(ang)
    sin_p = jnp.sin(ang)
    rope_lane = (lane >= C_NOPE) & (lane < C_NOPE + C_ROPE)
    cos_blocks, sin_blocks = [], []
    for u in range(packs):
        shift = (C_NOPE - u * C_ROPE) % LANES
        cu = pltpu.roll(cos_p, shift, axis=1) if shift else cos_p
        su = pltpu.roll(sin_p, shift, axis=1) if shift else sin_p
        cos_blocks.append(jnp.where(rope_lane, cu, 1.0))
        sin_blocks.append(jnp.where(rope_lane, su, 0.0))
    cos_t = jnp.concatenate(cos_blocks, axis=0)
    sin_t = jnp.concatenate(sin_blocks, axis=0)

    a = seg(0)
    kr = seg(1)
    cqn = _rms(a[:, 0:C_Q_RANK], qn_ref[...]).astype(BF16)
    ckvn = _rms(a[:, C_Q_RANK:], kvn_ref[...]).astype(BF16)
    du = seg(2)
    qa = _dot(cqn, wq_ref[...])
    qs = _dot(cqn, wqs_ref[...])
    gu = _gelu_tanh(du)
    dv = seg(3)
    for h in range(C_HEADS):
        sl = slice(h * SLOT, (h + 1) * SLOT)
        q_ref[:, sl] = ((qa[:, sl] * cos_t + qs[:, sl] * sin_t) * MLA_SCALE_LOG2E).astype(BF16)

    kr_rot = (kr[:, 0:LANES] * cos_t + kr[:, LANES:] * sin_t).astype(BF16)
    kv = _dot(jnp.concatenate([ckvn, kr_rot], axis=1), wkv_ref[...])
    vn = _layer_norm(_gelu_tanh(dv), lng_ref[...], lnb_ref[...]).astype(BF16)
    z = seg(4)
    k_ref[...] = kv[:, 0:wide].astype(BF16)
    v_ref[...] = (kv[:, wide:] + vones_ref[...]).astype(BF16)

    n_ch = tile // D_CHUNK
    mixed = []
    for g in range(D_GROUPS):
        cols = slice(g * D_GROUP_DIM, (g + 1) * D_GROUP_DIM)
        rhs = jnp.concatenate([vn[c * D_CHUNK:(c + 1) * D_CHUNK, cols] for c in range(n_ch)], axis=1)
        mixed.append(_dot(ws_ref[g], rhs))
    sz = _silu(z)
    szc_ref[...] = sz[:, 0:C_WIDTH]
    chunks = [jnp.concatenate([mixed[g][:, c * D_GROUP_DIM:(c + 1) * D_GROUP_DIM]
                               for g in range(D_GROUPS)], axis=1) + bs_ref[...]
              for c in range(n_ch)]
    yd = gu * jnp.concatenate(chunks, axis=0)
    yd_ref[...] = (_rms(yd, dnorm_ref[...]) * sz[:, C_WIDTH:]).astype(BF16)


def _mla_kernel(seq_len, tq, tk, unroll, q_ref, k_ref, v_ref, o_ref, s_a, s_b, m_a, m_b):
    n_qt = seq_len // tq
    n_c = seq_len // tk

    def scores(qt, hh, s_ref, m_ref):
        sl = slice(hh * SLOT, (hh + 1) * SLOT)
        q = q_ref[pl.ds(pl.multiple_of(qt * tq, tq), tq), sl]
        part = None
        for c in range(n_c):
            s = _dot_nt(q, k_ref[c * tk:(c + 1) * tk, sl])
            s_ref[:, c * tk:(c + 1) * tk] = s
            tiles = [s[:, j * LANES:(j + 1) * LANES] for j in range(tk // LANES)]
            while len(tiles) > 1:
                tiles = [jnp.maximum(a, b) for a, b in zip(tiles[0::2], tiles[1::2])]
            part = tiles[0] if part is None else jnp.maximum(part, tiles[0])
        m_ref[...] = jnp.broadcast_to(jnp.max(part, axis=-1, keepdims=True), m_ref.shape)

    def attend(hh, s_ref, m_ref):
        sl = slice(hh * SLOT, (hh + 1) * SLOT)
        m = m_ref[...]
        acc = None
        for c in range(n_c):
            p = jnp.concatenate(
                [jnp.exp2(s_ref[:, c * tk + j * LANES:c * tk + (j + 1) * LANES] - m).astype(BF16)
                 for j in range(tk // LANES)], axis=1)
            d = _dot(p, v_ref[c * tk:(c + 1) * tk, sl])
            acc = d if acc is None else acc + d
        return acc / acc[:, C_V:C_V + 1]

    scores(0, 0, s_a, m_a)

    def one_tile(qt):
        scores(qt, 1, s_b, m_b)
        out0 = attend(0, s_a, m_a)
        scores(jnp.minimum(qt + 1, n_qt - 1), 0, s_a, m_a)
        out1 = attend(1, s_b, m_b)
        lane = lax.broadcasted_iota(jnp.int32, out0.shape, 1)
        o_ref[pl.ds(pl.multiple_of(qt * tq, tq), tq), :] = jnp.where(
            lane < C_V, out0, pltpu.roll(out1, C_V, axis=1))

    def body(it, carry):
        for u in range(unroll):
            one_tile(it * unroll + u)
        return carry

    lax.fori_loop(0, n_qt // unroll, body, 0)


def _odd_out_kernel(yc_ref, szc_ref, yd_ref, x_ref, p_ref, cnorm_ref, wout_ref,
                    g_ref, b_ref, gate_ref, proj_ref, o_ref):
    tile = x_ref.shape[0]
    groups = [slice(r, r + tile // OUT_SPLIT) for r in range(0, tile, tile // OUT_SPLIT)]
    yc_n = [(_rms(yc_ref[r, :], cnorm_ref[...]) * szc_ref[r, :]).astype(BF16) for r in groups]
    emb = [_dot(p_ref[r, :].astype(BF16), proj_ref[...]) for r in groups]
    mix = [_dot(y, wout_ref[0:C_WIDTH, :]) + _dot(yd_ref[r, :], wout_ref[C_WIDTH:, :])
           for y, r in zip(yc_n, groups)]
    h = [_layer_norm(DEEPNORM_ALPHA * x_ref[r, :] + m, g_ref[...], b_ref[...])
         for m, r in zip(mix, groups)]
    gate = [jax.nn.sigmoid(_dot(hh.astype(BF16), gate_ref[...])) for hh in h]
    for r, hh, gt, e in zip(groups, h, gate, emb):
        o_ref[r, :] = hh + gt * e


def _full(shape):
    return pl.BlockSpec(shape, lambda *_: (0,) * len(shape))


def _once(shape):
    return pl.BlockSpec(shape, lambda *_: (0,) * len(shape), pipeline_mode=pl.Buffered(1))


def _params(sem):
    return pltpu.CompilerParams(dimension_semantics=sem, vmem_limit_bytes=VMEM_LIMIT)


def _even_layer(x, p, layer, positions, w_in, conv_w, sink, a_norm, b_norm, w_out,
                ln_g, ln_b, ple_proj, ple_gate):
    bsz, seq, _ = x.shape
    n_tok = bsz * seq
    tile = 512
    n_tiles = n_tok // tile
    tiles_per_seq = seq // tile
    sub = tile // BLOCK
    nb = seq // BLOCK
    q_w = A_HEADS * SLOT
    kv_w = A_KV_HEADS * SLOT
    x2 = x.reshape(n_tok, D_MODEL)
    pos_row = positions.reshape(bsz, 1, seq)

    cur = lambda t: jnp.minimum(t, n_tiles - 1)
    prv = lambda t: jnp.maximum(t - 1, 0)
    seq_of = lambda t: prv(t) // tiles_per_seq
    blk_of = lambda t: (prv(t) % tiles_per_seq) * sub
    in_specs = [
        pl.BlockSpec((tile, D_MODEL), lambda t: (cur(t), 0)),
        pl.BlockSpec((tile, D_MODEL), lambda t: (prv(t), 0)),
        pl.BlockSpec((None, tile, PLE_DIM), lambda t: (layer, prv(t), 0)),
        pl.BlockSpec((None, 1, tile), lambda t: (seq_of(t), 0, prv(t) % tiles_per_seq)),
        pl.BlockSpec((None, 1, BLOCK), lambda t: (seq_of(t), 0, jnp.maximum(blk_of(t) - 1, 0))),
        pl.BlockSpec((None, 1, BLOCK), lambda t: (seq_of(t), 0, jnp.minimum(blk_of(t) + sub, nb - 1))),
        _once((D_MODEL, _EV_OFF[-1])),
        _once((3, B_WIDTH)),
        pl.BlockSpec(memory_space=pltpu.SMEM),
        _once((1, A_WIDTH)), _once((1, B_WIDTH)), _once((D_MODEL, D_MODEL)),
        _once((1, D_MODEL)), _once((1, D_MODEL)), _once((D_MODEL, D_MODEL)),
        _once((PLE_DIM, D_MODEL)),
    ]
    out = pl.pallas_call(
        functools.partial(_even_kernel, seq, tile),
        grid=(n_tiles + 1,),
        in_specs=in_specs,
        out_specs=pl.BlockSpec((tile, D_MODEL), lambda t: (prv(t), 0)),
        out_shape=jax.ShapeDtypeStruct((n_tok, D_MODEL), F32),
        scratch_shapes=[pltpu.VMEM((seq + 2 * BLOCK, kv_w), BF16),
                        pltpu.VMEM((seq + 2 * BLOCK, kv_w), BF16),
                        pltpu.VMEM((tile, q_w), BF16), pltpu.VMEM((tile, q_w), BF16),
                        pltpu.VMEM((tile, B_WIDTH), F32), pltpu.VMEM((tile, B_WIDTH), F32),
                        pltpu.VMEM((tile, A_WIDTH + B_WIDTH), F32),
                        pltpu.VMEM((tile, A_WIDTH + B_WIDTH), F32),
                        pltpu.VMEM((tile, B_WIDTH), F32), pltpu.VMEM((tile, B_WIDTH), F32),
                        pltpu.VMEM((1, tile + 2 * BLOCK), jnp.int32),
                        pltpu.VMEM((tile, A_WIDTH), F32),
                        pltpu.VMEM((A_GROUP * BLOCK, 3 * BLOCK), F32),
                        pltpu.VMEM((A_GROUP * BLOCK, 3 * BLOCK), F32),
                        pltpu.VMEM((A_GROUP * BLOCK, LANES), F32),
                        pltpu.VMEM((A_GROUP * BLOCK, LANES), F32)],
        compiler_params=_params(("arbitrary",)),
        name="even_layer",
    )(x2, x2, p.reshape(DEPTH, n_tok, PLE_DIM), pos_row, pos_row, pos_row,
      w_in.astype(BF16), conv_w, sink * LOG2E, a_norm.reshape(1, -1), b_norm.reshape(1, -1),
      w_out.astype(BF16), ln_g.reshape(1, -1), ln_b.reshape(1, -1),
      ple_gate.astype(BF16), ple_proj.astype(BF16))
    return out.reshape(bsz, seq, D_MODEL)


def _swap_halves(r):
    half = r.shape[-1] // 2
    return jnp.concatenate([-r[..., half:], r[..., :half]], axis=-1)


def _odd_weights(w_in, w_uq, w_ukv, b_s):
    d = w_in.shape[0]
    o_kr = C_Q_RANK + C_KV_RANK
    o_du = o_kr + C_ROPE
    w_kr = w_in[:, o_kr:o_du]
    kr_pair = jnp.zeros((d, 2 * LANES), F32)
    kr_pair = kr_pair.at[:, C_NOPE:C_NOPE + C_ROPE].set(w_kr)
    kr_pair = kr_pair.at[:, LANES + C_NOPE:LANES + C_NOPE + C_ROPE].set(_swap_halves(w_kr))
    w1 = jnp.concatenate([w_in[:, :o_kr], kr_pair, w_in[:, o_du:]], axis=1).astype(BF16)

    uq = w_uq.reshape(C_Q_RANK, C_HEADS, C_NOPE + C_ROPE)
    pad = SLOT - C_NOPE - C_ROPE
    wq = jnp.pad(uq, ((0, 0), (0, 0), (0, pad)))
    wqs = jnp.pad(_swap_halves(uq[..., C_NOPE:]), ((0, 0), (0, 0), (C_NOPE, pad)))
    ukv = w_ukv.reshape(C_KV_RANK, C_HEADS, C_NOPE + C_V)
    wkn = jnp.pad(ukv[..., :C_NOPE], ((0, 0), (0, 0), (0, SLOT - C_NOPE)))
    wv = jnp.pad(ukv[..., C_NOPE:], ((0, 0), (0, 0), (0, SLOT - C_V)))
    flat = lambda w: w.reshape(w.shape[0], C_HEADS * SLOT).astype(BF16)

    lane = jnp.arange(LANES)
    rope_lane = (lane >= C_NOPE) & (lane < C_NOPE + C_ROPE)
    wke = jnp.tile(jnp.where(rope_lane[:, None] & (lane[:, None] == lane[None, :]), 1.0, 0.0),
                   (1, C_HEADS)).astype(BF16)
    vones = jnp.tile(jnp.where(lane == C_V, 1.0, 0.0), C_HEADS).reshape(1, -1).astype(F32)
    half = C_ROPE // 2
    inv = ROPE_THETA ** (-jnp.arange(half, dtype=F32) / half)
    inv_lane = jnp.tile(inv, LANES // half).reshape(1, LANES)
    bs_full = jnp.repeat(b_s.T, D_GROUP_DIM, axis=1)
    wkv = jnp.concatenate([jnp.concatenate([flat(wkn), flat(wv)], axis=1),
                           jnp.concatenate([wke, jnp.zeros_like(wke)], axis=1)], axis=0)
    return w1, flat(wq), flat(wqs), wkv, vones, inv_lane, bs_full


def _odd_layer(x, p, layer, positions, w_in, q_norm, w_uq, kv_norm, w_ukv, v_ln_g, v_ln_b,
               w_s, b_s, c_norm, d_norm, w_out, ln_g, ln_b, ple_proj, ple_gate):
    bsz, seq, _ = x.shape
    n_tok = bsz * seq
    w1, wq, wqs, wkv, vones, inv_lane, bs_full = _odd_weights(w_in, w_uq, w_ukv, b_s)
    tm = 512
    row_spec = lambda w: pl.BlockSpec((tm, w), lambda t: (t, 0))
    wide = C_HEADS * SLOT
    q, k, v, yd, szc = pl.pallas_call(
        _odd_proj_kernel,
        grid=(n_tok // tm,),
        in_specs=[row_spec(D_MODEL), pl.BlockSpec((tm // LANES, 1, LANES), lambda t: (t, 0, 0)),
                  _full(w1.shape), _full((1, LANES)),
                  _full((1, C_Q_RANK)), _full(wq.shape), _full(wqs.shape),
                  _full((1, C_KV_RANK)), _full(wkv.shape),
                  _full((1, wide)), _full((1, D_WIDTH)), _full((1, D_WIDTH)),
                  _full((D_GROUPS, D_CHUNK, D_CHUNK)), _full((D_CHUNK, D_WIDTH)),
                  _full((1, D_WIDTH))],
        out_specs=[row_spec(wide), row_spec(wide), row_spec(wide), row_spec(D_WIDTH),
                   row_spec(C_WIDTH)],
        out_shape=[jax.ShapeDtypeStruct((n_tok, wide), BF16),
                   jax.ShapeDtypeStruct((n_tok, wide), BF16),
                   jax.ShapeDtypeStruct((n_tok, wide), BF16),
                   jax.ShapeDtypeStruct((n_tok, D_WIDTH), BF16),
                   jax.ShapeDtypeStruct((n_tok, C_WIDTH), F32)],
        compiler_params=_params(("parallel",)),
        name="odd_proj",
    )(x.reshape(n_tok, D_MODEL), positions.reshape(n_tok // LANES, 1, LANES), w1, inv_lane,
      q_norm.reshape(1, -1), wq, wqs, kv_norm.reshape(1, -1), wkv, vones,
      v_ln_g.reshape(1, -1), v_ln_b.reshape(1, -1), w_s.astype(BF16), bs_full,
      d_norm.reshape(1, -1))

    tq = 256
    tk = 512
    q = q.reshape(bsz, seq, wide)
    k = k.reshape(bsz, seq, wide)
    v = v.reshape(bsz, seq, wide)
    pair = lambda w: pl.BlockSpec((None, seq, w), lambda b, h: (b, 0, h))
    yc = pl.pallas_call(
        functools.partial(_mla_kernel, seq, tq, tk, 4),
        grid=(bsz, C_HEADS // 2),
        in_specs=[pair(2 * SLOT), pair(2 * SLOT), pair(2 * SLOT)],
        out_specs=pair(2 * C_V),
        out_shape=jax.ShapeDtypeStruct((bsz, seq, C_WIDTH), F32),
        scratch_shapes=[pltpu.VMEM((tq, seq), F32), pltpu.VMEM((tq, seq), F32),
                        pltpu.VMEM((tq, LANES), F32), pltpu.VMEM((tq, LANES), F32)],
        compiler_params=_params(("parallel", "parallel")),
        name="mla_attn",
    )(q, k, v)

    to = 1024
    out_spec = lambda w: pl.BlockSpec((to, w), lambda t: (t, 0))
    out = pl.pallas_call(
        _odd_out_kernel,
        grid=(n_tok // to,),
        in_specs=[out_spec(C_WIDTH), out_spec(C_WIDTH), out_spec(D_WIDTH), out_spec(D_MODEL),
                  pl.BlockSpec((None, to, PLE_DIM), lambda t: (layer, t, 0)),
                  _once((1, C_WIDTH)), _once((D_MODEL, D_MODEL)), _once((1, D_MODEL)),
                  _once((1, D_MODEL)), _once((D_MODEL, D_MODEL)), _once((PLE_DIM, D_MODEL))],
        out_specs=out_spec(D_MODEL),
        out_shape=jax.ShapeDtypeStruct((n_tok, D_MODEL), F32),
        compiler_params=_params(("parallel",)),
        name="odd_out",
    )(yc.reshape(n_tok, C_WIDTH), szc, yd, x.reshape(n_tok, D_MODEL),
      p.reshape(DEPTH, n_tok, PLE_DIM), c_norm.reshape(1, -1), w_out.astype(BF16),
      ln_g.reshape(1, -1), ln_b.reshape(1, -1), ple_gate.astype(BF16), ple_proj.astype(BF16))
    return out.reshape(bsz, seq, D_MODEL)


def kernel(x, p, positions, ev_w_in, ev_conv_w, ev_sink, ev_a_norm, ev_b_norm, ev_w_out, od_w_in, od_q_norm, od_w_uq, od_kv_norm, od_w_ukv, od_v_ln_g, od_v_ln_b, od_w_s, od_b_s, od_c_norm, od_d_norm, od_w_out, post_ln_g, post_ln_b, ple_proj, ple_gate):
    for i in range(DEPTH):
        j = i // 2
        if i % 2 == 0:
            x = _even_layer(x, p, i, positions, ev_w_in[j], ev_conv_w[j], ev_sink[j],
                            ev_a_norm[j], ev_b_norm[j], ev_w_out[j],
                            post_ln_g[i], post_ln_b[i], ple_proj[i], ple_gate[i])
        else:
            x = _odd_layer(x, p, i, positions, od_w_in[j], od_q_norm[j], od_w_uq[j],
                           od_kv_norm[j], od_w_ukv[j], od_v_ln_g[j], od_v_ln_b[j],
                           od_w_s[j], od_b_s[j], od_c_norm[j], od_d_norm[j], od_w_out[j],
                           post_ln_g[i], post_ln_b[i], ple_proj[i], ple_gate[i])
    return x
```

```python
import functools
import math

import jax
import jax.numpy as jnp
from jax import lax
from jax.experimental import pallas as pl
from jax.experimental.pallas import tpu as pltpu

F32 = jnp.float32
BF16 = jnp.bfloat16

D_MODEL = 1024
DEPTH = 2
PLE_DIM = 256
DEEPNORM_ALPHA = (2.0 * DEPTH) ** 0.25
BLOCK = 128
EPS = 1e-6
A_HEAD_DIM = 64
A_HEADS = 8
A_KV_HEADS = 2
A_GROUP = A_HEADS // A_KV_HEADS
A_WIDTH = A_HEADS * A_HEAD_DIM
WINDOW = 128
B_WIDTH = 512
C_HEADS = 8
C_NOPE = 64
C_ROPE = 32
C_V = 64
C_WIDTH = C_HEADS * C_V
C_Q_RANK = 256
C_KV_RANK = 128
ROPE_THETA = 10000.0
D_WIDTH = 512
D_GROUPS = 4
D_GROUP_DIM = D_WIDTH // D_GROUPS
D_CHUNK = 128

LANES = 128
SLOT = LANES
VMEM_LIMIT = 56 * 1024 * 1024
OUT_SPLIT = 4

_EV_OFF = (0, 512, 640, 768, 1280, 1792, 2304, 3328)
_OD_OFF = (0, 384, 640, 1152, 1664, 2688)

A_SLOPES = tuple(2.0 ** (-8.0 * (h + 1) / A_HEADS) for h in range(A_HEADS))
LOG2E = math.log2(math.e)
A_SCALE_LOG2E = A_HEAD_DIM ** -0.5 * LOG2E
MLA_SCALE_LOG2E = (C_NOPE + C_ROPE) ** -0.5 * LOG2E


def _dot(a, b):
    return jnp.dot(a, b, preferred_element_type=F32)


def _dot_nt(a, b):
    return lax.dot_general(a, b, (((1,), (1,)), ((), ())), preferred_element_type=F32)


def _column(row):
    return jnp.transpose(jnp.broadcast_to(row, (LANES, LANES)))


def _rms(x, g):
    return x * lax.rsqrt(jnp.mean(x * x, axis=-1, keepdims=True) + EPS) * g


def _layer_norm(x, g, b):
    mu = jnp.mean(x, axis=-1, keepdims=True)
    xc = x - mu
    var = jnp.mean(xc * xc, axis=-1, keepdims=True)
    return xc * lax.rsqrt(var + EPS) * g + b


def _silu(z):
    return z * jax.nn.sigmoid(z)


def _gelu_tanh(x):
    c = math.sqrt(2.0 / math.pi)
    return 0.5 * x * (1.0 + jnp.tanh(c * (x + 0.044715 * (x * x * x))))


def _post_block(y_bf16_dot, x, p, g_ref, b_ref, gate_ref, proj_ref):
    h = _layer_norm(DEEPNORM_ALPHA * x + y_bf16_dot, g_ref[...], b_ref[...])
    gate = jax.nn.sigmoid(_dot(h.astype(BF16), gate_ref[...]))
    return h + gate * _dot(p.astype(BF16), proj_ref[...])


def _even_kernel(seq_len, tile,
                 xc_ref, xp_ref, p_ref, prc_ref, prp_ref, prn_ref,
                 win_ref, convw_ref, sink_ref, anorm_ref, bnorm_ref, wout_ref,
                 g_ref, b_ref, gate_ref, proj_ref,
                 o_ref,
                 kbuf, vbuf, q_a, q_b, bg_a, bg_b, sz_a, sz_b, cz_a, cz_b,
                 posbuf, ybuf, lg_a, lg_b, m_a, m_b):
    t = pl.program_id(0)

    @pl.when(t == 0)
    def _():
        for ref in (kbuf, vbuf, q_a, q_b, bg_a, bg_b, sz_a, sz_b, cz_a, cz_b):
            ref[...] = jnp.zeros(ref.shape, ref.dtype)

    shared = (seq_len, tile, xc_ref, xp_ref, p_ref, prc_ref, prp_ref, prn_ref,
              win_ref, convw_ref, sink_ref, anorm_ref, bnorm_ref, wout_ref,
              g_ref, b_ref, gate_ref, proj_ref, o_ref,
              kbuf, vbuf, posbuf, ybuf, lg_a, lg_b, m_a, m_b)

    @pl.when(t % 2 == 0)
    def _():
        _even_step(*shared, (q_a, bg_a, sz_a, cz_a), (q_b, bg_b, sz_b, cz_b))

    @pl.when(t % 2 == 1)
    def _():
        _even_step(*shared, (q_b, bg_b, sz_b, cz_b), (q_a, bg_a, sz_a, cz_a))


def _even_step(seq_len, tile, xc_ref, xp_ref, p_ref, prc_ref, prp_ref, prn_ref,
               win_ref, convw_ref, sink_ref, anorm_ref, bnorm_ref, wout_ref,
               g_ref, b_ref, gate_ref, proj_ref, o_ref,
               kbuf, vbuf, posbuf, ybuf, lg_a, lg_b, m_a, m_b, proj_bufs, mix_bufs):
    t = pl.program_id(0)
    tiles_per_seq = seq_len // tile
    sub = tile // BLOCK
    pq, pbg, psz, pcz = proj_bufs
    mq, mbg, msz, mcz = mix_bufs

    before_rows = pcz[tile - 8:tile, :]

    xb = xc_ref[...].astype(BF16)

    def seg(n, lo=0, hi=None):
        hi = _EV_OFF[n + 1] - _EV_OFF[n] if hi is None else hi
        return _dot(xb, win_ref[:, _EV_OFF[n] + lo:_EV_OFF[n] + hi])

    krow_p = pl.multiple_of(BLOCK + (t % tiles_per_seq) * tile, BLOCK)

    lane_t = lax.broadcasted_iota(jnp.int32, (tile, LANES), 1)
    low = lane_t < A_HEAD_DIM
    one = jnp.where(lane_t == A_HEAD_DIM, 1.0, 0.0)

    def slots(v, fill):
        return (jnp.where(low, v, fill),
                jnp.where(low, pltpu.roll(v, A_HEAD_DIM, axis=1), fill))

    def proj_kv():
        lo, hi = slots(seg(1), 0.0)
        kbuf[pl.ds(krow_p, tile), 0:SLOT] = lo.astype(BF16)
        kbuf[pl.ds(krow_p, tile), SLOT:] = hi.astype(BF16)
        lo, hi = slots(seg(2), one)
        vbuf[pl.ds(krow_p, tile), 0:SLOT] = lo.astype(BF16)
        vbuf[pl.ds(krow_p, tile), SLOT:] = hi.astype(BF16)

    piece_w = 2 * LANES

    def proj_q(part):
        q = seg(0, part * piece_w, (part + 1) * piece_w) * A_SCALE_LOG2E
        for n in range(piece_w // LANES):
            lo, hi = slots(q[:, n * LANES:(n + 1) * LANES], 0.0)
            h0 = 2 * (part * (piece_w // LANES) + n)
            pq[:, h0 * SLOT:(h0 + 1) * SLOT] = lo.astype(BF16)
            pq[:, (h0 + 1) * SLOT:(h0 + 2) * SLOT] = hi.astype(BF16)

    def proj_bg(part):
        lo, hi = part * piece_w, (part + 1) * piece_w
        pbg[:, lo:hi] = seg(3, lo, hi)

    def proj_cg(part):
        lo, hi = part * piece_w, (part + 1) * piece_w
        pcz[:, lo:hi] = seg(4, lo, hi)

    def proj_xin(part):
        lo, hi = part * piece_w, (part + 1) * piece_w
        pcz[:, lo:hi] = pcz[:, lo:hi] * seg(5, lo, hi)

    def proj_sz(part):
        lo, hi = part * piece_w, (part + 1) * piece_w
        psz[:, lo:hi] = _silu(seg(6, lo, hi))

    proj_kv()
    pieces = [functools.partial(fn, part)
              for fn, parts in ((proj_q, 2), (proj_bg, 2), (proj_cg, 2), (proj_xin, 2), (proj_sz, 4))
              for part in range(parts)]

    i = jnp.maximum(t - 1, 0) % tiles_per_seq
    krow_m = pl.multiple_of(i * tile, BLOCK)

    posbuf[:, 0:BLOCK] = prp_ref[...]
    posbuf[:, BLOCK:BLOCK + tile] = prc_ref[...]
    posbuf[:, BLOCK + tile:] = prn_ref[...]

    q_idx = lax.broadcasted_iota(jnp.int32, (BLOCK, 3 * BLOCK), 0)
    w_idx = lax.broadcasted_iota(jnp.int32, (BLOCK, 3 * BLOCK), 1)
    in_band = jnp.abs(w_idx - BLOCK - q_idx) <= WINDOW
    lane = lax.broadcasted_iota(jnp.int32, (BLOCK, LANES), 1)
    dists = {}

    def masked_dist(j):
        if j not in dists:
            k_glob = (i * sub + j) * BLOCK + w_idx - BLOCK
            valid = in_band & (k_glob >= 0) & (k_glob < seq_len)
            pos_q = _column(posbuf[:, (j + 1) * BLOCK:(j + 2) * BLOCK])
            d = jnp.concatenate([jnp.abs(pos_q - posbuf[:, (j + n) * BLOCK:(j + n + 1) * BLOCK])
                                 for n in range(3)], axis=1).astype(F32)
            dists[j] = jnp.where(valid, d, jnp.inf)
        return dists[j]

    def scores(j, g, lg_ref, m_ref):
        qrow = slice(j * BLOCK, (j + 1) * BLOCK)
        qg = jnp.concatenate([mq[qrow, (A_GROUP * g + hh) * SLOT:(A_GROUP * g + hh + 1) * SLOT]
                              for hh in range(A_GROUP)], axis=0)
        s = _dot_nt(qg, kbuf[pl.ds(krow_m + j * BLOCK, 3 * BLOCK), g * SLOT:(g + 1) * SLOT])
        d = masked_dist(j)
        for hh in range(A_GROUP):
            h = A_GROUP * g + hh
            r = slice(hh * BLOCK, (hh + 1) * BLOCK)
            lg = s[r, :] - (A_SLOPES[h] * LOG2E) * d
            lg_ref[r, :] = lg
            mx = jnp.maximum(jnp.max(lg, axis=-1, keepdims=True), sink_ref[h])
            m_ref[r, :] = jnp.broadcast_to(mx, (BLOCK, LANES))

    def attend(j, g, lg_ref, m_ref):
        rows = slice(j * BLOCK, (j + 1) * BLOCK)
        m = m_ref[...]
        e = jnp.concatenate([jnp.exp2(lg_ref[:, n * LANES:(n + 1) * LANES] - m).astype(BF16)
                             for n in range(3)], axis=1)
        pv = _dot(e, vbuf[pl.ds(krow_m + j * BLOCK, 3 * BLOCK), g * SLOT:(g + 1) * SLOT])
        outs = []
        for hh in range(A_GROUP):
            r = slice(hh * BLOCK, (hh + 1) * BLOCK)
            denom = (jnp.broadcast_to(pv[r, A_HEAD_DIM:A_HEAD_DIM + 1], (BLOCK, LANES))
                     + jnp.exp2(sink_ref[A_GROUP * g + hh] - m[r, :]))
            outs.append(pv[r, :] / denom)
        for pr in range(A_GROUP // 2):
            c0 = (A_GROUP // 2 * g + pr) * LANES
            ybuf[rows, c0:c0 + LANES] = jnp.where(
                lane < A_HEAD_DIM, outs[2 * pr], pltpu.roll(outs[2 * pr + 1], A_HEAD_DIM, axis=1))

    chains = [(j, g) for j in range(sub) for g in range(A_KV_HEADS)]
    bufs = [(lg_a, m_a), (lg_b, m_b)]
    scores(*chains[0], *bufs[0])
    late = [pieces.pop() for _ in range(4)][::-1]
    for c, (j, g) in enumerate(chains):
        if c + 1 < len(chains):
            scores(*chains[c + 1], *bufs[(c + 1) % 2])
        attend(j, g, *bufs[c % 2])
        pieces.pop(0)()
    assert not pieces

    late[0]()
    cz = mcz[...]
    row = lax.broadcasted_iota(jnp.int32, cz.shape, 0)
    prev_row = jnp.where(i > 0, before_rows[7:8, :], 0.0)
    next_row = jnp.where(i < tiles_per_seq - 1, pcz[0:1, :], 0.0)
    z_prev = jnp.where(row == 0, prev_row, pltpu.roll(cz, 1, axis=0))
    z_next = jnp.where(row == tile - 1, next_row, pltpu.roll(cz, tile - 1, axis=0))
    conv = (convw_ref[0:1, :] * z_prev + convw_ref[1:2, :] * cz
            + convw_ref[2:3, :] * z_next)
    yb = mbg[...] * conv

    late[1]()
    groups = [slice(r, r + tile // 2) for r in range(0, tile, tile // 2)]
    ya_n = [(_rms(ybuf[r, :], anorm_ref[...]) * msz[r, 0:A_WIDTH]).astype(BF16) for r in groups]
    yb_n = [(_rms(yb[r, :], bnorm_ref[...]) * msz[r, A_WIDTH:]).astype(BF16) for r in groups]
    emb = [_dot(p_ref[r, :].astype(BF16), proj_ref[...]) for r in groups]
    mix = [_dot(a, wout_ref[0:A_WIDTH, :]) + _dot(b, wout_ref[A_WIDTH:, :])
           for a, b in zip(ya_n, yb_n)]
    late[2]()
    h = [_layer_norm(DEEPNORM_ALPHA * xp_ref[r, :] + m, g_ref[...], b_ref[...])
         for m, r in zip(mix, groups)]
    late[3]()
    gate = [jax.nn.sigmoid(_dot(hh.astype(BF16), gate_ref[...])) for hh in h]
    for r, hh, gt, e in zip(groups, h, gate, emb):
        o_ref[r, :] = hh + gt * e


def _odd_proj_kernel(x_ref, pos_ref, w_ref, inv_ref, qn_ref, wq_ref, wqs_ref,
                     kvn_ref, wkv_ref, vones_ref,
                     lng_ref, lnb_ref, ws_ref, bs_ref, dnorm_ref,
                     q_ref, k_ref, v_ref, yd_ref, szc_ref):
    tile = x_ref.shape[0]
    wide = C_HEADS * SLOT
    xb = x_ref[...].astype(BF16)

    def seg(i):
        return _dot(xb, w_ref[:, _OD_OFF[i]:_OD_OFF[i + 1]])

    packs = LANES // C_ROPE
    rb = tile // packs
    lane = lax.broadcasted_iota(jnp.int32, (rb, LANES), 1)
    packed = _column(pos_ref[0]).astype(F32)
    for u in range(1, packs):
        packed = jnp.where(lane >= u * C_ROPE, _column(pos_ref[u]).astype(F32), packed)
    ang = packed * inv_ref[...]
    cos_p = jnp.cos(ang)
    sin_p = jnp.sin(ang)
    rope_lane = (lane >= C_NOPE) & (lane < C_NOPE + C_ROPE)
    cos_blocks, sin_blocks = [], []
    for u in range(packs):
        shift = (C_NOPE - u * C_ROPE) % LANES
        cu = pltpu.roll(cos_p, shift, axis=1) if shift else cos_p
        su = pltpu.roll(sin_p, shift, axis=1) if shift else sin_p
        cos_blocks.append(jnp.where(rope_lane, cu, 1.0))
        sin_blocks.append(jnp.where(rope_lane, su, 0.0))
    cos_t = jnp.concatenate(cos_blocks, axis=0)
    sin_t = jnp.concatenate(sin_blocks, axis=0)

    a = seg(0)
    kr = seg(1)
    cqn = _rms(a[:, 0:C_Q_RANK], qn_ref[...]).astype(BF16)
    ckvn = _rms(a[:, C_Q_RANK:], kvn_ref[...]).astype(BF16)
    du = seg(2)
    qa = _dot(cqn, wq_ref[...])
    qs = _dot(cqn, wqs_ref[...])
    gu = _gelu_tanh(du)
    dv = seg(3)
    for h in range(C_HEADS):
        sl = slice(h * SLOT, (h + 1) * SLOT)
        q_ref[:, sl] = ((qa[:, sl] * cos_t + qs[:, sl] * sin_t) * MLA_SCALE_LOG2E).astype(BF16)

    kr_rot = (kr[:, 0:LANES] * cos_t + kr[:, LANES:] * sin_t).astype(BF16)
    kv = _dot(jnp.concatenate([ckvn, kr_rot], axis=1), wkv_ref[...])
    vn = _layer_norm(_gelu_tanh(dv), lng_ref[...], lnb_ref[...]).astype(BF16)
    z = seg(4)
    k_ref[...] = kv[:, 0:wide].astype(BF16)
    v_ref[...] = jnp.transpose(kv[:, wide:] + vones_ref[...]).astype(BF16)

    n_ch = tile // D_CHUNK
    mixed = []
    for g in range(D_GROUPS):
        cols = slice(g * D_GROUP_DIM, (g + 1) * D_GROUP_DIM)
        rhs = jnp.concatenate([vn[c * D_CHUNK:(c + 1) * D_CHUNK, cols] for c in range(n_ch)], axis=1)
        mixed.append(_dot(ws_ref[g], rhs))
    sz = _silu(z)
    szc_ref[...] = sz[:, 0:C_WIDTH]
    chunks = [jnp.concatenate([mixed[g][:, c * D_GROUP_DIM:(c + 1) * D_GROUP_DIM]
                               for g in range(D_GROUPS)], axis=1) + bs_ref[...]
              for c in range(n_ch)]
    yd = gu * jnp.concatenate(chunks, axis=0)
    yd_ref[...] = (_rms(yd, dnorm_ref[...]) * sz[:, C_WIDTH:]).astype(BF16)


def _mla_kernel(seq_len, tq, tk, unroll, q_ref, k_ref, vt_ref, o_ref, s_a, s_b, m_a, m_b):
    n_qt = seq_len // tq
    n_c = seq_len // tk

    def scores(qt, hh, s_ref, m_ref):
        sl = slice(hh * SLOT, (hh + 1) * SLOT)
        q = q_ref[pl.ds(pl.multiple_of(qt * tq, tq), tq), sl]
        part = None
        for c in range(n_c):
            st = _dot_nt(k_ref[c * tk:(c + 1) * tk, sl], q)
            s_ref[c * tk:(c + 1) * tk, :] = st
            cm = jnp.max(st.reshape(tk // 8, 8, tq), axis=0)
            part = cm if part is None else jnp.maximum(part, cm)
        m_ref[...] = jnp.broadcast_to(jnp.max(part, axis=0, keepdims=True), m_ref.shape)

    def attend(hh, s_ref, m_ref):
        sl = slice(hh * SLOT, (hh + 1) * SLOT)
        m = m_ref[0:1, :]
        acc = None
        for c in range(n_c):
            pt = jnp.exp2(s_ref[c * tk:(c + 1) * tk, :] - m).astype(BF16)
            d = _dot(vt_ref[sl, c * tk:(c + 1) * tk], pt)
            acc = d if acc is None else acc + d
        return acc[0:C_V, :] / acc[C_V:C_V + 1, :]

    scores(0, 0, s_a, m_a)

    def one_tile(qt):
        scores(qt, 1, s_b, m_b)
        out0 = attend(0, s_a, m_a)
        scores(jnp.minimum(qt + 1, n_qt - 1), 0, s_a, m_a)
        out1 = attend(1, s_b, m_b)
        both = jnp.concatenate([out0, out1], axis=0)
        o_ref[pl.ds(pl.multiple_of(qt * tq, tq), tq), :] = jnp.transpose(both)

    def body(it, carry):
        for u in range(unroll):
            one_tile(it * unroll + u)
        return carry

    lax.fori_loop(0, n_qt // unroll, body, 0)


def _odd_out_kernel(yc_ref, szc_ref, yd_ref, x_ref, p_ref, cnorm_ref, wout_ref,
                    g_ref, b_ref, gate_ref, proj_ref, o_ref):
    tile = x_ref.shape[0]
    groups = [slice(r, r + tile // OUT_SPLIT) for r in range(0, tile, tile // OUT_SPLIT)]
    yc_n = [(_rms(yc_ref[r, :], cnorm_ref[...]) * szc_ref[r, :]).astype(BF16) for r in groups]
    emb = [_dot(p_ref[r, :].astype(BF16), proj_ref[...]) for r in groups]
    mix = [_dot(y, wout_ref[0:C_WIDTH, :]) + _dot(yd_ref[r, :], wout_ref[C_WIDTH:, :])
           for y, r in zip(yc_n, groups)]
    h = [_layer_norm(DEEPNORM_ALPHA * x_ref[r, :] + m, g_ref[...], b_ref[...])
         for m, r in zip(mix, groups)]
    gate = [jax.nn.sigmoid(_dot(hh.astype(BF16), gate_ref[...])) for hh in h]
    for r, hh, gt, e in zip(groups, h, gate, emb):
        o_ref[r, :] = hh + gt * e


def _full(shape):
    return pl.BlockSpec(shape, lambda *_: (0,) * len(shape))


def _once(shape):
    return pl.BlockSpec(shape, lambda *_: (0,) * len(shape), pipeline_mode=pl.Buffered(1))


def _params(sem):
    return pltpu.CompilerParams(dimension_semantics=sem, vmem_limit_bytes=VMEM_LIMIT)


def _even_layer(x, p, layer, positions, w_in, conv_w, sink, a_norm, b_norm, w_out,
                ln_g, ln_b, ple_proj, ple_gate):
    bsz, seq, _ = x.shape
    n_tok = bsz * seq
    tile = 512
    n_tiles = n_tok // tile
    tiles_per_seq = seq // tile
    sub = tile // BLOCK
    nb = seq // BLOCK
    q_w = A_HEADS * SLOT
    kv_w = A_KV_HEADS * SLOT
    x2 = x.reshape(n_tok, D_MODEL)
    pos_row = positions.reshape(bsz, 1, seq)

    cur = lambda t: jnp.minimum(t, n_tiles - 1)
    prv = lambda t: jnp.maximum(t - 1, 0)
    seq_of = lambda t: prv(t) // tiles_per_seq
    blk_of = lambda t: (prv(t) % tiles_per_seq) * sub
    in_specs = [
        pl.BlockSpec((tile, D_MODEL), lambda t: (cur(t), 0)),
        pl.BlockSpec((tile, D_MODEL), lambda t: (prv(t), 0)),
        pl.BlockSpec((None, tile, PLE_DIM), lambda t: (layer, prv(t), 0)),
        pl.BlockSpec((None, 1, tile), lambda t: (seq_of(t), 0, prv(t) % tiles_per_seq)),
        pl.BlockSpec((None, 1, BLOCK), lambda t: (seq_of(t), 0, jnp.maximum(blk_of(t) - 1, 0))),
        pl.BlockSpec((None, 1, BLOCK), lambda t: (seq_of(t), 0, jnp.minimum(blk_of(t) + sub, nb - 1))),
        _once((D_MODEL, _EV_OFF[-1])),
        _once((3, B_WIDTH)),
        pl.BlockSpec(memory_space=pltpu.SMEM),
        _once((1, A_WIDTH)), _once((1, B_WIDTH)), _once((D_MODEL, D_MODEL)),
        _once((1, D_MODEL)), _once((1, D_MODEL)), _once((D_MODEL, D_MODEL)),
        _once((PLE_DIM, D_MODEL)),
    ]
    out = pl.pallas_call(
        functools.partial(_even_kernel, seq, tile),
        grid=(n_tiles + 1,),
        in_specs=in_specs,
        out_specs=pl.BlockSpec((tile, D_MODEL), lambda t: (prv(t), 0)),
        out_shape=jax.ShapeDtypeStruct((n_tok, D_MODEL), F32),
        scratch_shapes=[pltpu.VMEM((seq + 2 * BLOCK, kv_w), BF16),
                        pltpu.VMEM((seq + 2 * BLOCK, kv_w), BF16),
                        pltpu.VMEM((tile, q_w), BF16), pltpu.VMEM((tile, q_w), BF16),
                        pltpu.VMEM((tile, B_WIDTH), F32), pltpu.VMEM((tile, B_WIDTH), F32),
                        pltpu.VMEM((tile, A_WIDTH + B_WIDTH), F32),
                        pltpu.VMEM((tile, A_WIDTH + B_WIDTH), F32),
                        pltpu.VMEM((tile, B_WIDTH), F32), pltpu.VMEM((tile, B_WIDTH), F32),
                        pltpu.VMEM((1, tile + 2 * BLOCK), jnp.int32),
                        pltpu.VMEM((tile, A_WIDTH), F32),
                        pltpu.VMEM((A_GROUP * BLOCK, 3 * BLOCK), F32),
                        pltpu.VMEM((A_GROUP * BLOCK, 3 * BLOCK), F32),
                        pltpu.VMEM((A_GROUP * BLOCK, LANES), F32),
                        pltpu.VMEM((A_GROUP * BLOCK, LANES), F32)],
        compiler_params=_params(("arbitrary",)),
        name="even_layer",
    )(x2, x2, p.reshape(DEPTH, n_tok, PLE_DIM), pos_row, pos_row, pos_row,
      w_in.astype(BF16), conv_w, sink * LOG2E, a_norm.reshape(1, -1), b_norm.reshape(1, -1),
      w_out.astype(BF16), ln_g.reshape(1, -1), ln_b.reshape(1, -1),
      ple_gate.astype(BF16), ple_proj.astype(BF16))
    return out.reshape(bsz, seq, D_MODEL)


def _swap_halves(r):
    half = r.shape[-1] // 2
    return jnp.concatenate([-r[..., half:], r[..., :half]], axis=-1)


def _odd_weights(w_in, w_uq, w_ukv, b_s):
    d = w_in.shape[0]
    o_kr = C_Q_RANK + C_KV_RANK
    o_du = o_kr + C_ROPE
    w_kr = w_in[:, o_kr:o_du]
    kr_pair = jnp.zeros((d, 2 * LANES), F32)
    kr_pair = kr_pair.at[:, C_NOPE:C_NOPE + C_ROPE].set(w_kr)
    kr_pair = kr_pair.at[:, LANES + C_NOPE:LANES + C_NOPE + C_ROPE].set(_swap_halves(w_kr))
    w1 = jnp.concatenate([w_in[:, :o_kr], kr_pair, w_in[:, o_du:]], axis=1).astype(BF16)

    uq = w_uq.reshape(C_Q_RANK, C_HEADS, C_NOPE + C_ROPE)
    pad = SLOT - C_NOPE - C_ROPE
    wq = jnp.pad(uq, ((0, 0), (0, 0), (0, pad)))
    wqs = jnp.pad(_swap_halves(uq[..., C_NOPE:]), ((0, 0), (0, 0), (C_NOPE, pad)))
    ukv = w_ukv.reshape(C_KV_RANK, C_HEADS, C_NOPE + C_V)
    wkn = jnp.pad(ukv[..., :C_NOPE], ((0, 0), (0, 0), (0, SLOT - C_NOPE)))
    wv = jnp.pad(ukv[..., C_NOPE:], ((0, 0), (0, 0), (0, SLOT - C_V)))
    flat = lambda w: w.reshape(w.shape[0], C_HEADS * SLOT).astype(BF16)

    lane = jnp.arange(LANES)
    rope_lane = (lane >= C_NOPE) & (lane < C_NOPE + C_ROPE)
    wke = jnp.tile(jnp.where(rope_lane[:, None] & (lane[:, None] == lane[None, :]), 1.0, 0.0),
                   (1, C_HEADS)).astype(BF16)
    vones = jnp.tile(jnp.where(lane == C_V, 1.0, 0.0), C_HEADS).reshape(1, -1).astype(F32)
    half = C_ROPE // 2
    inv = ROPE_THETA ** (-jnp.arange(half, dtype=F32) / half)
    inv_lane = jnp.tile(inv, LANES // half).reshape(1, LANES)
    bs_full = jnp.repeat(b_s.T, D_GROUP_DIM, axis=1)
    wkv = jnp.concatenate([jnp.concatenate([flat(wkn), flat(wv)], axis=1),
                           jnp.concatenate([wke, jnp.zeros_like(wke)], axis=1)], axis=0)
    return w1, flat(wq), flat(wqs), wkv, vones, inv_lane, bs_full


def _odd_layer(x, p, layer, positions, w_in, q_norm, w_uq, kv_norm, w_ukv, v_ln_g, v_ln_b,
               w_s, b_s, c_norm, d_norm, w_out, ln_g, ln_b, ple_proj, ple_gate):
    bsz, seq, _ = x.shape
    n_tok = bsz * seq
    w1, wq, wqs, wkv, vones, inv_lane, bs_full = _odd_weights(w_in, w_uq, w_ukv, b_s)
    tm = 512
    row_spec = lambda w: pl.BlockSpec((tm, w), lambda t: (t, 0))
    wide = C_HEADS * SLOT
    q, k, v, yd, szc = pl.pallas_call(
        _odd_proj_kernel,
        grid=(n_tok // tm,),
        in_specs=[row_spec(D_MODEL), pl.BlockSpec((tm // LANES, 1, LANES), lambda t: (t, 0, 0)),
                  _full(w1.shape), _full((1, LANES)),
                  _full((1, C_Q_RANK)), _full(wq.shape), _full(wqs.shape),
                  _full((1, C_KV_RANK)), _full(wkv.shape),
                  _full((1, wide)), _full((1, D_WIDTH)), _full((1, D_WIDTH)),
                  _full((D_GROUPS, D_CHUNK, D_CHUNK)), _full((D_CHUNK, D_WIDTH)),
                  _full((1, D_WIDTH))],
        out_specs=[row_spec(wide), row_spec(wide), pl.BlockSpec((wide, tm), lambda t: (0, t)),
                   row_spec(D_WIDTH),
                   row_spec(C_WIDTH)],
        out_shape=[jax.ShapeDtypeStruct((n_tok, wide), BF16),
                   jax.ShapeDtypeStruct((n_tok, wide), BF16),
                   jax.ShapeDtypeStruct((wide, n_tok), BF16),
                   jax.ShapeDtypeStruct((n_tok, D_WIDTH), BF16),
                   jax.ShapeDtypeStruct((n_tok, C_WIDTH), F32)],
        compiler_params=_params(("parallel",)),
        name="odd_proj",
    )(x.reshape(n_tok, D_MODEL), positions.reshape(n_tok // LANES, 1, LANES), w1, inv_lane,
      q_norm.reshape(1, -1), wq, wqs, kv_norm.reshape(1, -1), wkv, vones,
      v_ln_g.reshape(1, -1), v_ln_b.reshape(1, -1), w_s.astype(BF16), bs_full,
      d_norm.reshape(1, -1))

    tq = 256
    tk = 512
    q = q.reshape(bsz, seq, wide)
    k = k.reshape(bsz, seq, wide)
    pair = lambda w: pl.BlockSpec((None, seq, w), lambda b, h: (b, 0, h))
    yc = pl.pallas_call(
        functools.partial(_mla_kernel, seq, tq, tk, 4),
        grid=(bsz, C_HEADS // 2),
        in_specs=[pair(2 * SLOT), pair(2 * SLOT),
                  pl.BlockSpec((2 * SLOT, seq), lambda b, h: (h, b))],
        out_specs=pair(2 * C_V),
        out_shape=jax.ShapeDtypeStruct((bsz, seq, C_WIDTH), F32),
        scratch_shapes=[pltpu.VMEM((seq, tq), F32), pltpu.VMEM((seq, tq), F32),
                        pltpu.VMEM((8, tq), F32), pltpu.VMEM((8, tq), F32)],
        compiler_params=_params(("parallel", "parallel")),
        name="mla_attn",
    )(q, k, v)

    to = 1024
    out_spec = lambda w: pl.BlockSpec((to, w), lambda t: (t, 0))
    out = pl.pallas_call(
        _odd_out_kernel,
        grid=(n_tok // to,),
        in_specs=[out_spec(C_WIDTH), out_spec(C_WIDTH), out_spec(D_WIDTH), out_spec(D_MODEL),
                  pl.BlockSpec((None, to, PLE_DIM), lambda t: (layer, t, 0)),
                  _once((1, C_WIDTH)), _once((D_MODEL, D_MODEL)), _once((1, D_MODEL)),
                  _once((1, D_MODEL)), _once((D_MODEL, D_MODEL)), _once((PLE_DIM, D_MODEL))],
        out_specs=out_spec(D_MODEL),
        out_shape=jax.ShapeDtypeStruct((n_tok, D_MODEL), F32),
        compiler_params=_params(("parallel",)),
        name="odd_out",
    )(yc.reshape(n_tok, C_WIDTH), szc, yd, x.reshape(n_tok, D_MODEL),
      p.reshape(DEPTH, n_tok, PLE_DIM), c_norm.reshape(1, -1), w_out.astype(BF16),
      ln_g.reshape(1, -1), ln_b.reshape(1, -1), ple_gate.astype(BF16), ple_proj.astype(BF16))
    return out.reshape(bsz, seq, D_MODEL)


def kernel(x, p, positions, ev_w_in, ev_conv_w, ev_sink, ev_a_norm, ev_b_norm, ev_w_out, od_w_in, od_q_norm, od_w_uq, od_kv_norm, od_w_ukv, od_v_ln_g, od_v_ln_b, od_w_s, od_b_s, od_c_norm, od_d_norm, od_w_out, post_ln_g, post_ln_b, ple_proj, ple_gate):
    for i in range(DEPTH):
        j = i // 2
        if i % 2 == 0:
            x = _even_layer(x, p, i, positions, ev_w_in[j], ev_conv_w[j], ev_sink[j],
                            ev_a_norm[j], ev_b_norm[j], ev_w_out[j],
                            post_ln_g[i], post_ln_b[i], ple_proj[i], ple_gate[i])
        else:
            x = _odd_layer(x, p, i, positions, od_w_in[j], od_q_norm[j], od_w_uq[j],
                           od_kv_norm[j], od_w_ukv[j], od_v_ln_g[j], od_v_ln_b[j],
                           od_w_s[j], od_b_s[j], od_c_norm[j], od_d_norm[j], od_w_out[j],
                           post_ln_g[i], post_ln_b[i], ple_proj[i], ple_gate[i])
    return x
```

```python
import functools
import math

import jax
import jax.numpy as jnp
from jax import lax
from jax.experimental import pallas as pl
from jax.experimental.pallas import tpu as pltpu

F32 = jnp.float32
BF16 = jnp.bfloat16

D_MODEL = 1024
DEPTH = 2
PLE_DIM = 256
DEEPNORM_ALPHA = (2.0 * DEPTH) ** 0.25
BLOCK = 128
EPS = 1e-6
A_HEAD_DIM = 64
A_HEADS = 8
A_KV_HEADS = 2
A_GROUP = A_HEADS // A_KV_HEADS
A_WIDTH = A_HEADS * A_HEAD_DIM
WINDOW = 128
B_WIDTH = 512
C_HEADS = 8
C_NOPE = 64
C_ROPE = 32
C_V = 64
C_WIDTH = C_HEADS * C_V
C_Q_RANK = 256
C_KV_RANK = 128
ROPE_THETA = 10000.0
D_WIDTH = 512
D_GROUPS = 4
D_GROUP_DIM = D_WIDTH // D_GROUPS
D_CHUNK = 128

LANES = 128
SLOT = LANES
VMEM_LIMIT = 56 * 1024 * 1024
V_ROWS = 80
OUT_SPLIT = 4

_EV_OFF = (0, 512, 640, 768, 1280, 1792, 2304, 3328)
_OD_OFF = (0, 384, 512, 1024, 1536, 2560)

A_SLOPES = tuple(2.0 ** (-8.0 * (h + 1) / A_HEADS) for h in range(A_HEADS))
LOG2E = math.log2(math.e)
A_SCALE_LOG2E = A_HEAD_DIM ** -0.5 * LOG2E
MLA_SCALE_LOG2E = (C_NOPE + C_ROPE) ** -0.5 * LOG2E


def _dot(a, b):
    return jnp.dot(a, b, preferred_element_type=F32)


def _dot_nt(a, b):
    return lax.dot_general(a, b, (((1,), (1,)), ((), ())), preferred_element_type=F32)


def _column(row):
    return jnp.transpose(jnp.broadcast_to(row, (LANES, LANES)))


def _rms(x, g):
    return x * lax.rsqrt(jnp.mean(x * x, axis=-1, keepdims=True) + EPS) * g


def _layer_norm(x, g, b):
    mu = jnp.mean(x, axis=-1, keepdims=True)
    xc = x - mu
    var = jnp.mean(xc * xc, axis=-1, keepdims=True)
    return xc * lax.rsqrt(var + EPS) * g + b


def _silu(z):
    return z * jax.nn.sigmoid(z)


def _gelu_tanh(x):
    c = math.sqrt(2.0 / math.pi)
    return 0.5 * x * (1.0 + jnp.tanh(c * (x + 0.044715 * (x * x * x))))


def _post_block(y_bf16_dot, x, p, g_ref, b_ref, gate_ref, proj_ref):
    h = _layer_norm(DEEPNORM_ALPHA * x + y_bf16_dot, g_ref[...], b_ref[...])
    gate = jax.nn.sigmoid(_dot(h.astype(BF16), gate_ref[...]))
    return h + gate * _dot(p.astype(BF16), proj_ref[...])


def _even_kernel(seq_len, tile,
                 xc_ref, xp_ref, p_ref, prc_ref, prp_ref, prn_ref,
                 win_ref, convw_ref, sink_ref, anorm_ref, bnorm_ref, wout_ref,
                 g_ref, b_ref, gate_ref, proj_ref,
                 o_ref,
                 kbuf, vbuf, q_a, q_b, bg_a, bg_b, sz_a, sz_b, cz_a, cz_b,
                 posbuf, ybuf, lg_a, lg_b, m_a, m_b):
    t = pl.program_id(0)

    @pl.when(t == 0)
    def _():
        for ref in (kbuf, vbuf, q_a, q_b, bg_a, bg_b, sz_a, sz_b, cz_a, cz_b):
            ref[...] = jnp.zeros(ref.shape, ref.dtype)

    shared = (seq_len, tile, xc_ref, xp_ref, p_ref, prc_ref, prp_ref, prn_ref,
              win_ref, convw_ref, sink_ref, anorm_ref, bnorm_ref, wout_ref,
              g_ref, b_ref, gate_ref, proj_ref, o_ref,
              kbuf, vbuf, posbuf, ybuf, lg_a, lg_b, m_a, m_b)

    @pl.when(t % 2 == 0)
    def _():
        _even_step(*shared, (q_a, bg_a, sz_a, cz_a), (q_b, bg_b, sz_b, cz_b))

    @pl.when(t % 2 == 1)
    def _():
        _even_step(*shared, (q_b, bg_b, sz_b, cz_b), (q_a, bg_a, sz_a, cz_a))


def _even_step(seq_len, tile, xc_ref, xp_ref, p_ref, prc_ref, prp_ref, prn_ref,
               win_ref, convw_ref, sink_ref, anorm_ref, bnorm_ref, wout_ref,
               g_ref, b_ref, gate_ref, proj_ref, o_ref,
               kbuf, vbuf, posbuf, ybuf, lg_a, lg_b, m_a, m_b, proj_bufs, mix_bufs):
    t = pl.program_id(0)
    tiles_per_seq = seq_len // tile
    sub = tile // BLOCK
    pq, pbg, psz, pcz = proj_bufs
    mq, mbg, msz, mcz = mix_bufs

    before_rows = pcz[tile - 8:tile, :]

    xb = xc_ref[...].astype(BF16)

    def seg(n, lo=0, hi=None):
        hi = _EV_OFF[n + 1] - _EV_OFF[n] if hi is None else hi
        return _dot(xb, win_ref[:, _EV_OFF[n] + lo:_EV_OFF[n] + hi])

    krow_p = pl.multiple_of(BLOCK + (t % tiles_per_seq) * tile, BLOCK)

    lane_t = lax.broadcasted_iota(jnp.int32, (tile, LANES), 1)
    low = lane_t < A_HEAD_DIM
    one = jnp.where(lane_t == A_HEAD_DIM, 1.0, 0.0)

    def slots(v, fill):
        return (jnp.where(low, v, fill),
                jnp.where(low, pltpu.roll(v, A_HEAD_DIM, axis=1), fill))

    def proj_kv():
        lo, hi = slots(seg(1), 0.0)
        kbuf[pl.ds(krow_p, tile), 0:SLOT] = lo.astype(BF16)
        kbuf[pl.ds(krow_p, tile), SLOT:] = hi.astype(BF16)
        lo, hi = slots(seg(2), one)
        vbuf[pl.ds(krow_p, tile), 0:SLOT] = lo.astype(BF16)
        vbuf[pl.ds(krow_p, tile), SLOT:] = hi.astype(BF16)

    piece_w = 2 * LANES

    def proj_q(part):
        q = seg(0, part * piece_w, (part + 1) * piece_w) * A_SCALE_LOG2E
        for n in range(piece_w // LANES):
            lo, hi = slots(q[:, n * LANES:(n + 1) * LANES], 0.0)
            h0 = 2 * (part * (piece_w // LANES) + n)
            pq[:, h0 * SLOT:(h0 + 1) * SLOT] = lo.astype(BF16)
            pq[:, (h0 + 1) * SLOT:(h0 + 2) * SLOT] = hi.astype(BF16)

    def proj_bg(part):
        lo, hi = part * piece_w, (part + 1) * piece_w
        pbg[:, lo:hi] = seg(3, lo, hi)

    def proj_cg(part):
        lo, hi = part * piece_w, (part + 1) * piece_w
        pcz[:, lo:hi] = seg(4, lo, hi)

    def proj_xin(part):
        lo, hi = part * piece_w, (part + 1) * piece_w
        pcz[:, lo:hi] = pcz[:, lo:hi] * seg(5, lo, hi)

    def proj_sz(part):
        lo, hi = part * piece_w, (part + 1) * piece_w
        psz[:, lo:hi] = _silu(seg(6, lo, hi))

    proj_kv()
    pieces = [functools.partial(fn, part)
              for fn, parts in ((proj_q, 2), (proj_bg, 2), (proj_cg, 2), (proj_xin, 2), (proj_sz, 4))
              for part in range(parts)]

    i = jnp.maximum(t - 1, 0) % tiles_per_seq
    krow_m = pl.multiple_of(i * tile, BLOCK)

    posbuf[:, 0:BLOCK] = prp_ref[...]
    posbuf[:, BLOCK:BLOCK + tile] = prc_ref[...]
    posbuf[:, BLOCK + tile:] = prn_ref[...]

    q_idx = lax.broadcasted_iota(jnp.int32, (BLOCK, 3 * BLOCK), 0)
    w_idx = lax.broadcasted_iota(jnp.int32, (BLOCK, 3 * BLOCK), 1)
    in_band = jnp.abs(w_idx - BLOCK - q_idx) <= WINDOW
    lane = lax.broadcasted_iota(jnp.int32, (BLOCK, LANES), 1)
    dists = {}

    def masked_dist(j):
        if j not in dists:
            k_glob = (i * sub + j) * BLOCK + w_idx - BLOCK
            valid = in_band & (k_glob >= 0) & (k_glob < seq_len)
            pos_q = _column(posbuf[:, (j + 1) * BLOCK:(j + 2) * BLOCK])
            d = jnp.concatenate([jnp.abs(pos_q - posbuf[:, (j + n) * BLOCK:(j + n + 1) * BLOCK])
                                 for n in range(3)], axis=1).astype(F32)
            dists[j] = jnp.where(valid, d, jnp.inf)
        return dists[j]

    def scores(j, g, lg_ref, m_ref):
        qrow = slice(j * BLOCK, (j + 1) * BLOCK)
        qg = jnp.concatenate([mq[qrow, (A_GROUP * g + hh) * SLOT:(A_GROUP * g + hh + 1) * SLOT]
                              for hh in range(A_GROUP)], axis=0)
        s = _dot_nt(qg, kbuf[pl.ds(krow_m + j * BLOCK, 3 * BLOCK), g * SLOT:(g + 1) * SLOT])
        d = masked_dist(j)
        for hh in range(A_GROUP):
            h = A_GROUP * g + hh
            r = slice(hh * BLOCK, (hh + 1) * BLOCK)
            lg = s[r, :] - (A_SLOPES[h] * LOG2E) * d
            lg_ref[r, :] = lg
            mx = jnp.maximum(jnp.max(lg, axis=-1, keepdims=True), sink_ref[h])
            m_ref[r, :] = jnp.broadcast_to(mx, (BLOCK, LANES))

    def attend(j, g, lg_ref, m_ref):
        rows = slice(j * BLOCK, (j + 1) * BLOCK)
        m = m_ref[...]
        e = jnp.concatenate([jnp.exp2(lg_ref[:, n * LANES:(n + 1) * LANES] - m).astype(BF16)
                             for n in range(3)], axis=1)
        pv = _dot(e, vbuf[pl.ds(krow_m + j * BLOCK, 3 * BLOCK), g * SLOT:(g + 1) * SLOT])
        outs = []
        for hh in range(A_GROUP):
            r = slice(hh * BLOCK, (hh + 1) * BLOCK)
            denom = (jnp.broadcast_to(pv[r, A_HEAD_DIM:A_HEAD_DIM + 1], (BLOCK, LANES))
                     + jnp.exp2(sink_ref[A_GROUP * g + hh] - m[r, :]))
            outs.append(pv[r, :] / denom)
        for pr in range(A_GROUP // 2):
            c0 = (A_GROUP // 2 * g + pr) * LANES
            ybuf[rows, c0:c0 + LANES] = jnp.where(
                lane < A_HEAD_DIM, outs[2 * pr], pltpu.roll(outs[2 * pr + 1], A_HEAD_DIM, axis=1))

    chains = [(j, g) for j in range(sub) for g in range(A_KV_HEADS)]
    bufs = [(lg_a, m_a), (lg_b, m_b)]
    scores(*chains[0], *bufs[0])
    late = [pieces.pop() for _ in range(4)][::-1]
    for c, (j, g) in enumerate(chains):
        if c + 1 < len(chains):
            scores(*chains[c + 1], *bufs[(c + 1) % 2])
        attend(j, g, *bufs[c % 2])
        pieces.pop(0)()
    assert not pieces

    late[0]()
    cz = mcz[...]
    row = lax.broadcasted_iota(jnp.int32, cz.shape, 0)
    prev_row = jnp.where(i > 0, before_rows[7:8, :], 0.0)
    next_row = jnp.where(i < tiles_per_seq - 1, pcz[0:1, :], 0.0)
    z_prev = jnp.where(row == 0, prev_row, pltpu.roll(cz, 1, axis=0))
    z_next = jnp.where(row == tile - 1, next_row, pltpu.roll(cz, tile - 1, axis=0))
    conv = (convw_ref[0:1, :] * z_prev + convw_ref[1:2, :] * cz
            + convw_ref[2:3, :] * z_next)
    yb = mbg[...] * conv

    late[1]()
    groups = [slice(r, r + tile // 2) for r in range(0, tile, tile // 2)]
    ya_n = [(_rms(ybuf[r, :], anorm_ref[...]) * msz[r, 0:A_WIDTH]).astype(BF16) for r in groups]
    yb_n = [(_rms(yb[r, :], bnorm_ref[...]) * msz[r, A_WIDTH:]).astype(BF16) for r in groups]
    emb = [_dot(p_ref[r, :].astype(BF16), proj_ref[...]) for r in groups]
    mix = [_dot(a, wout_ref[0:A_WIDTH, :]) + _dot(b, wout_ref[A_WIDTH:, :])
           for a, b in zip(ya_n, yb_n)]
    late[2]()
    h = [_layer_norm(DEEPNORM_ALPHA * xp_ref[r, :] + m, g_ref[...], b_ref[...])
         for m, r in zip(mix, groups)]
    late[3]()
    gate = [jax.nn.sigmoid(_dot(hh.astype(BF16), gate_ref[...])) for hh in h]
    for r, hh, gt, e in zip(groups, h, gate, emb):
        o_ref[r, :] = hh + gt * e


def _odd_proj_kernel(x_ref, pos_ref, w_ref, inv_ref, qn_ref, wq_ref,
                     kvn_ref, wkv_ref, vones_ref,
                     lng_ref, lnb_ref, ws_ref, bs_ref, dnorm_ref,
                     q_ref, k_ref, v_ref, yd_ref, szc_ref):
    tile = x_ref.shape[0]
    wide = C_HEADS * SLOT
    xb = x_ref[...].astype(BF16)

    def seg(i):
        return _dot(xb, w_ref[:, _OD_OFF[i]:_OD_OFF[i + 1]])

    packs = LANES // C_ROPE
    rb = tile // packs
    lane = lax.broadcasted_iota(jnp.int32, (rb, LANES), 1)
    packed = _column(pos_ref[0]).astype(F32)
    for u in range(1, packs):
        packed = jnp.where(lane >= u * C_ROPE, _column(pos_ref[u]).astype(F32), packed)
    ang = packed * inv_ref[...]
    cos_p = jnp.cos(ang)
    sin_p = jnp.sin(ang)
    rope_lane = (lane >= C_NOPE) & (lane < C_NOPE + C_ROPE)
    cos_blocks, sin_blocks = [], []
    for u in range(packs):
        shift = (C_NOPE - u * C_ROPE) % LANES
        cu = pltpu.roll(cos_p, shift, axis=1) if shift else cos_p
        su = pltpu.roll(sin_p, shift, axis=1) if shift else sin_p
        cos_blocks.append(jnp.where(rope_lane, cu, 1.0))
        sin_blocks.append(jnp.where(rope_lane, su, 0.0))
    cos_t = jnp.concatenate(cos_blocks, axis=0)
    sin_t = jnp.concatenate(sin_blocks, axis=0)
    first_half = lax.broadcasted_iota(jnp.int32, (tile, LANES), 1) < C_NOPE + C_ROPE // 2
    sin_s = jnp.where(first_half, -sin_t, sin_t)

    def rotary(t):
        swapped = jnp.where(first_half, pltpu.roll(t, LANES - C_ROPE // 2, axis=1),
                            pltpu.roll(t, C_ROPE // 2, axis=1))
        return t * cos_t + swapped * sin_s

    a = seg(0)
    kr = seg(1)
    cqn = _rms(a[:, 0:C_Q_RANK], qn_ref[...]).astype(BF16)
    ckvn = _rms(a[:, C_Q_RANK:], kvn_ref[...]).astype(BF16)
    du = seg(2)
    qa = _dot(cqn, wq_ref[...])
    gu = _gelu_tanh(du)
    dv = seg(3)
    for h in range(C_HEADS):
        sl = slice(h * SLOT, (h + 1) * SLOT)
        q_ref[:, sl] = (rotary(qa[:, sl]) * MLA_SCALE_LOG2E).astype(BF16)

    kr_rot = rotary(kr).astype(BF16)
    kv = _dot(jnp.concatenate([ckvn, kr_rot], axis=1), wkv_ref[...])
    vn = _layer_norm(_gelu_tanh(dv), lng_ref[...], lnb_ref[...]).astype(BF16)
    z = seg(4)
    k_ref[...] = kv[:, 0:wide].astype(BF16)
    v_ref[...] = jnp.transpose(kv[:, wide:] + vones_ref[...]).astype(BF16)

    n_ch = tile // D_CHUNK
    mixed = []
    for g in range(D_GROUPS):
        cols = slice(g * D_GROUP_DIM, (g + 1) * D_GROUP_DIM)
        rhs = jnp.concatenate([vn[c * D_CHUNK:(c + 1) * D_CHUNK, cols] for c in range(n_ch)], axis=1)
        mixed.append(_dot(ws_ref[g], rhs))
    sz = _silu(z)
    szc_ref[...] = sz[:, 0:C_WIDTH]
    chunks = [jnp.concatenate([mixed[g][:, c * D_GROUP_DIM:(c + 1) * D_GROUP_DIM]
                               for g in range(D_GROUPS)], axis=1) + bs_ref[...]
              for c in range(n_ch)]
    yd = gu * jnp.concatenate(chunks, axis=0)
    yd_ref[...] = (_rms(yd, dnorm_ref[...]) * sz[:, C_WIDTH:]).astype(BF16)


def _mla_kernel(seq_len, tq, tk, unroll, q_ref, k_ref, vt_ref, o_ref, s_a, s_b, m_a, m_b):
    n_qt = seq_len // tq
    n_c = seq_len // tk

    def scores(qt, hh, s_ref, m_ref):
        sl = slice(hh * SLOT, (hh + 1) * SLOT)
        q = q_ref[pl.ds(pl.multiple_of(qt * tq, tq), tq), sl]
        part = None
        for c in range(n_c):
            st = _dot_nt(k_ref[c * tk:(c + 1) * tk, sl], q)
            s_ref[c * tk:(c + 1) * tk, :] = st
            cm = jnp.max(st.reshape(tk // 8, 8, tq), axis=0)
            part = cm if part is None else jnp.maximum(part, cm)
        m_ref[...] = jnp.broadcast_to(jnp.max(part, axis=0, keepdims=True), m_ref.shape)

    def attend(hh, s_ref, m_ref):
        sl = slice(hh * SLOT, hh * SLOT + V_ROWS)
        m = m_ref[0:1, :]
        acc = None
        for c in range(n_c):
            pt = jnp.exp2(s_ref[c * tk:(c + 1) * tk, :] - m).astype(BF16)
            d = _dot(vt_ref[sl, c * tk:(c + 1) * tk], pt)
            acc = d if acc is None else acc + d
        return acc[0:C_V, :] / acc[C_V:C_V + 1, :]

    scores(0, 0, s_a, m_a)

    def one_tile(qt):
        scores(qt, 1, s_b, m_b)
        out0 = attend(0, s_a, m_a)
        out1 = attend(1, s_b, m_b)
        scores(jnp.minimum(qt + 1, n_qt - 1), 0, s_a, m_a)
        both = jnp.concatenate([out0, out1], axis=0)
        o_ref[pl.ds(pl.multiple_of(qt * tq, tq), tq), :] = jnp.transpose(both)

    def body(it, carry):
        for u in range(unroll):
            one_tile(it * unroll + u)
        return carry

    lax.fori_loop(0, n_qt // unroll, body, 0)


def _odd_out_kernel(yc_ref, szc_ref, yd_ref, x_ref, p_ref, cnorm_ref, wout_ref,
                    g_ref, b_ref, gate_ref, proj_ref, o_ref):
    tile = x_ref.shape[0]
    groups = [slice(r, r + tile // OUT_SPLIT) for r in range(0, tile, tile // OUT_SPLIT)]
    yc_n = [(_rms(yc_ref[r, :], cnorm_ref[...]) * szc_ref[r, :]).astype(BF16) for r in groups]
    emb = [_dot(p_ref[r, :].astype(BF16), proj_ref[...]) for r in groups]
    mix = [_dot(y, wout_ref[0:C_WIDTH, :]) + _dot(yd_ref[r, :], wout_ref[C_WIDTH:, :])
           for y, r in zip(yc_n, groups)]
    h = [_layer_norm(DEEPNORM_ALPHA * x_ref[r, :] + m, g_ref[...], b_ref[...])
         for m, r in zip(mix, groups)]
    gate = [jax.nn.sigmoid(_dot(hh.astype(BF16), gate_ref[...])) for hh in h]
    for r, hh, gt, e in zip(groups, h, gate, emb):
        o_ref[r, :] = hh + gt * e


def _full(shape):
    return pl.BlockSpec(shape, lambda *_: (0,) * len(shape))


def _once(shape):
    return pl.BlockSpec(shape, lambda *_: (0,) * len(shape), pipeline_mode=pl.Buffered(1))


def _params(sem):
    return pltpu.CompilerParams(dimension_semantics=sem, vmem_limit_bytes=VMEM_LIMIT)


def _even_layer(x, p, layer, positions, w_in, conv_w, sink, a_norm, b_norm, w_out,
                ln_g, ln_b, ple_proj, ple_gate):
    bsz, seq, _ = x.shape
    n_tok = bsz * seq
    tile = 512
    n_tiles = n_tok // tile
    tiles_per_seq = seq // tile
    sub = tile // BLOCK
    nb = seq // BLOCK
    q_w = A_HEADS * SLOT
    kv_w = A_KV_HEADS * SLOT
    x2 = x.reshape(n_tok, D_MODEL)
    pos_row = positions.reshape(bsz, 1, seq)

    cur = lambda t: jnp.minimum(t, n_tiles - 1)
    prv = lambda t: jnp.maximum(t - 1, 0)
    seq_of = lambda t: prv(t) // tiles_per_seq
    blk_of = lambda t: (prv(t) % tiles_per_seq) * sub
    in_specs = [
        pl.BlockSpec((tile, D_MODEL), lambda t: (cur(t), 0)),
        pl.BlockSpec((tile, D_MODEL), lambda t: (prv(t), 0)),
        pl.BlockSpec((None, tile, PLE_DIM), lambda t: (layer, prv(t), 0)),
        pl.BlockSpec((None, 1, tile), lambda t: (seq_of(t), 0, prv(t) % tiles_per_seq)),
        pl.BlockSpec((None, 1, BLOCK), lambda t: (seq_of(t), 0, jnp.maximum(blk_of(t) - 1, 0))),
        pl.BlockSpec((None, 1, BLOCK), lambda t: (seq_of(t), 0, jnp.minimum(blk_of(t) + sub, nb - 1))),
        _once((D_MODEL, _EV_OFF[-1])),
        _once((3, B_WIDTH)),
        pl.BlockSpec(memory_space=pltpu.SMEM),
        _once((1, A_WIDTH)), _once((1, B_WIDTH)), _once((D_MODEL, D_MODEL)),
        _once((1, D_MODEL)), _once((1, D_MODEL)), _once((D_MODEL, D_MODEL)),
        _once((PLE_DIM, D_MODEL)),
    ]
    out = pl.pallas_call(
        functools.partial(_even_kernel, seq, tile),
        grid=(n_tiles + 1,),
        in_specs=in_specs,
        out_specs=pl.BlockSpec((tile, D_MODEL), lambda t: (prv(t), 0)),
        out_shape=jax.ShapeDtypeStruct((n_tok, D_MODEL), F32),
        scratch_shapes=[pltpu.VMEM((seq + 2 * BLOCK, kv_w), BF16),
                        pltpu.VMEM((seq + 2 * BLOCK, kv_w), BF16),
                        pltpu.VMEM((tile, q_w), BF16), pltpu.VMEM((tile, q_w), BF16),
                        pltpu.VMEM((tile, B_WIDTH), F32), pltpu.VMEM((tile, B_WIDTH), F32),
                        pltpu.VMEM((tile, A_WIDTH + B_WIDTH), F32),
                        pltpu.VMEM((tile, A_WIDTH + B_WIDTH), F32),
                        pltpu.VMEM((tile, B_WIDTH), F32), pltpu.VMEM((tile, B_WIDTH), F32),
                        pltpu.VMEM((1, tile + 2 * BLOCK), jnp.int32),
                        pltpu.VMEM((tile, A_WIDTH), F32),
                        pltpu.VMEM((A_GROUP * BLOCK, 3 * BLOCK), F32),
                        pltpu.VMEM((A_GROUP * BLOCK, 3 * BLOCK), F32),
                        pltpu.VMEM((A_GROUP * BLOCK, LANES), F32),
                        pltpu.VMEM((A_GROUP * BLOCK, LANES), F32)],
        compiler_params=_params(("arbitrary",)),
        name="even_layer",
    )(x2, x2, p.reshape(DEPTH, n_tok, PLE_DIM), pos_row, pos_row, pos_row,
      w_in.astype(BF16), conv_w, sink * LOG2E, a_norm.reshape(1, -1), b_norm.reshape(1, -1),
      w_out.astype(BF16), ln_g.reshape(1, -1), ln_b.reshape(1, -1),
      ple_gate.astype(BF16), ple_proj.astype(BF16))
    return out.reshape(bsz, seq, D_MODEL)


def _odd_weights(w_in, w_uq, w_ukv, b_s):
    d = w_in.shape[0]
    o_kr = C_Q_RANK + C_KV_RANK
    o_du = o_kr + C_ROPE
    w_kr = w_in[:, o_kr:o_du]
    kr_tile = jnp.pad(w_kr, ((0, 0), (C_NOPE, LANES - C_NOPE - C_ROPE)))
    w1 = jnp.concatenate([w_in[:, :o_kr], kr_tile, w_in[:, o_du:]], axis=1).astype(BF16)

    uq = w_uq.reshape(C_Q_RANK, C_HEADS, C_NOPE + C_ROPE)
    pad = SLOT - C_NOPE - C_ROPE
    wq = jnp.pad(uq, ((0, 0), (0, 0), (0, pad)))
    ukv = w_ukv.reshape(C_KV_RANK, C_HEADS, C_NOPE + C_V)
    wkn = jnp.pad(ukv[..., :C_NOPE], ((0, 0), (0, 0), (0, SLOT - C_NOPE)))
    wv = jnp.pad(ukv[..., C_NOPE:], ((0, 0), (0, 0), (0, SLOT - C_V)))
    flat = lambda w: w.reshape(w.shape[0], C_HEADS * SLOT).astype(BF16)

    lane = jnp.arange(LANES)
    rope_lane = (lane >= C_NOPE) & (lane < C_NOPE + C_ROPE)
    wke = jnp.tile(jnp.where(rope_lane[:, None] & (lane[:, None] == lane[None, :]), 1.0, 0.0),
                   (1, C_HEADS)).astype(BF16)
    vones = jnp.tile(jnp.where(lane == C_V, 1.0, 0.0), C_HEADS).reshape(1, -1).astype(F32)
    half = C_ROPE // 2
    inv = ROPE_THETA ** (-jnp.arange(half, dtype=F32) / half)
    inv_lane = jnp.tile(inv, LANES // half).reshape(1, LANES)
    bs_full = jnp.repeat(b_s.T, D_GROUP_DIM, axis=1)
    wkv = jnp.concatenate([jnp.concatenate([flat(wkn), flat(wv)], axis=1),
                           jnp.concatenate([wke, jnp.zeros_like(wke)], axis=1)], axis=0)
    return w1, flat(wq), wkv, vones, inv_lane, bs_full


def _odd_layer(x, p, layer, positions, w_in, q_norm, w_uq, kv_norm, w_ukv, v_ln_g, v_ln_b,
               w_s, b_s, c_norm, d_norm, w_out, ln_g, ln_b, ple_proj, ple_gate):
    bsz, seq, _ = x.shape
    n_tok = bsz * seq
    w1, wq, wkv, vones, inv_lane, bs_full = _odd_weights(w_in, w_uq, w_ukv, b_s)
    tm = 512
    row_spec = lambda w: pl.BlockSpec((tm, w), lambda t: (t, 0))
    wide = C_HEADS * SLOT
    q, k, v, yd, szc = pl.pallas_call(
        _odd_proj_kernel,
        grid=(n_tok // tm,),
        in_specs=[row_spec(D_MODEL), pl.BlockSpec((tm // LANES, 1, LANES), lambda t: (t, 0, 0)),
                  _full(w1.shape), _full((1, LANES)),
                  _full((1, C_Q_RANK)), _full(wq.shape),
                  _full((1, C_KV_RANK)), _full(wkv.shape),
                  _full((1, wide)), _full((1, D_WIDTH)), _full((1, D_WIDTH)),
                  _full((D_GROUPS, D_CHUNK, D_CHUNK)), _full((D_CHUNK, D_WIDTH)),
                  _full((1, D_WIDTH))],
        out_specs=[row_spec(wide), row_spec(wide), pl.BlockSpec((wide, tm), lambda t: (0, t)),
                   row_spec(D_WIDTH),
                   row_spec(C_WIDTH)],
        out_shape=[jax.ShapeDtypeStruct((n_tok, wide), BF16),
                   jax.ShapeDtypeStruct((n_tok, wide), BF16),
                   jax.ShapeDtypeStruct((wide, n_tok), BF16),
                   jax.ShapeDtypeStruct((n_tok, D_WIDTH), BF16),
                   jax.ShapeDtypeStruct((n_tok, C_WIDTH), F32)],
        compiler_params=_params(("parallel",)),
        name="odd_proj",
    )(x.reshape(n_tok, D_MODEL), positions.reshape(n_tok // LANES, 1, LANES), w1, inv_lane,
      q_norm.reshape(1, -1), wq, kv_norm.reshape(1, -1), wkv, vones,
      v_ln_g.reshape(1, -1), v_ln_b.reshape(1, -1), w_s.astype(BF16), bs_full,
      d_norm.reshape(1, -1))

    tq = 256
    tk = 512
    q = q.reshape(bsz, seq, wide)
    k = k.reshape(bsz, seq, wide)
    pair = lambda w: pl.BlockSpec((None, seq, w), lambda b, h: (b, 0, h))
    yc = pl.pallas_call(
        functools.partial(_mla_kernel, seq, tq, tk, 4),
        grid=(bsz, C_HEADS // 2),
        in_specs=[pair(2 * SLOT), pair(2 * SLOT),
                  pl.BlockSpec((2 * SLOT, seq), lambda b, h: (h, b))],
        out_specs=pair(2 * C_V),
        out_shape=jax.ShapeDtypeStruct((bsz, seq, C_WIDTH), F32),
        scratch_shapes=[pltpu.VMEM((seq, tq), F32), pltpu.VMEM((seq, tq), F32),
                        pltpu.VMEM((8, tq), F32), pltpu.VMEM((8, tq), F32)],
        compiler_params=_params(("parallel", "parallel")),
        name="mla_attn",
    )(q, k, v)

    to = 1024
    out_spec = lambda w: pl.BlockSpec((to, w), lambda t: (t, 0))
    out = pl.pallas_call(
        _odd_out_kernel,
        grid=(n_tok // to,),
        in_specs=[out_spec(C_WIDTH), out_spec(C_WIDTH), out_spec(D_WIDTH), out_spec(D_MODEL),
                  pl.BlockSpec((None, to, PLE_DIM), lambda t: (layer, t, 0)),
                  _once((1, C_WIDTH)), _once((D_MODEL, D_MODEL)), _once((1, D_MODEL)),
                  _once((1, D_MODEL)), _once((D_MODEL, D_MODEL)), _once((PLE_DIM, D_MODEL))],
        out_specs=out_spec(D_MODEL),
        out_shape=jax.ShapeDtypeStruct((n_tok, D_MODEL), F32),
        compiler_params=_params(("parallel",)),
        name="odd_out",
    )(yc.reshape(n_tok, C_WIDTH), szc, yd, x.reshape(n_tok, D_MODEL),
      p.reshape(DEPTH, n_tok, PLE_DIM), c_norm.reshape(1, -1), w_out.astype(BF16),
      ln_g.reshape(1, -1), ln_b.reshape(1, -1), ple_gate.astype(BF16), ple_proj.astype(BF16))
    return out.reshape(bsz, seq, D_MODEL)


def kernel(x, p, positions, ev_w_in, ev_conv_w, ev_sink, ev_a_norm, ev_b_norm, ev_w_out, od_w_in, od_q_norm, od_w_uq, od_kv_norm, od_w_ukv, od_v_ln_g, od_v_ln_b, od_w_s, od_b_s, od_c_norm, od_d_norm, od_w_out, post_ln_g, post_ln_b, ple_proj, ple_gate):
    for i in range(DEPTH):
        j = i // 2
        if i % 2 == 0:
            x = _even_layer(x, p, i, positions, ev_w_in[j], ev_conv_w[j], ev_sink[j],
                            ev_a_norm[j], ev_b_norm[j], ev_w_out[j],
                            post_ln_g[i], post_ln_b[i], ple_proj[i], ple_gate[i])
        else:
            x = _odd_layer(x, p, i, positions, od_w_in[j], od_q_norm[j], od_w_uq[j],
                           od_kv_norm[j], od_w_ukv[j], od_v_ln_g[j], od_v_ln_b[j],
                           od_w_s[j], od_b_s[j], od_c_norm[j], od_d_norm[j], od_w_out[j],
                           post_ln_g[i], post_ln_b[i], ple_proj[i], ple_gate[i])
    return x
```

```python
import functools
import math

import jax
import jax.numpy as jnp
from jax import lax
from jax.experimental import pallas as pl
from jax.experimental.pallas import tpu as pltpu

F32 = jnp.float32
BF16 = jnp.bfloat16

D_MODEL = 1024
DEPTH = 2
PLE_DIM = 256
DEEPNORM_ALPHA = (2.0 * DEPTH) ** 0.25
BLOCK = 128
EPS = 1e-6
A_HEAD_DIM = 64
A_HEADS = 8
A_KV_HEADS = 2
A_GROUP = A_HEADS // A_KV_HEADS
A_WIDTH = A_HEADS * A_HEAD_DIM
WINDOW = 128
B_WIDTH = 512
C_HEADS = 8
C_NOPE = 64
C_ROPE = 32
C_V = 64
C_WIDTH = C_HEADS * C_V
C_Q_RANK = 256
C_KV_RANK = 128
ROPE_THETA = 10000.0
D_WIDTH = 512
D_GROUPS = 4
D_GROUP_DIM = D_WIDTH // D_GROUPS
D_CHUNK = 128

LANES = 128
SLOT = LANES
VMEM_LIMIT = 56 * 1024 * 1024
V_ROWS = 80
OUT_SPLIT = 4

_EV_OFF = (0, 512, 640, 768, 1280, 1792, 2304, 3328)
_OD_OFF = (0, 384, 512, 1024, 1536, 2560)

A_SLOPES = tuple(2.0 ** (-8.0 * (h + 1) / A_HEADS) for h in range(A_HEADS))
LOG2E = math.log2(math.e)
A_SCALE_LOG2E = A_HEAD_DIM ** -0.5 * LOG2E
MLA_SCALE_LOG2E = (C_NOPE + C_ROPE) ** -0.5 * LOG2E


def _dot(a, b):
    return jnp.dot(a, b, preferred_element_type=F32)


def _dot_nt(a, b):
    return lax.dot_general(a, b, (((1,), (1,)), ((), ())), preferred_element_type=F32)


def _column(row):
    return jnp.transpose(jnp.broadcast_to(row, (LANES, LANES)))


def _rms(x, g):
    return x * lax.rsqrt(jnp.mean(x * x, axis=-1, keepdims=True) + EPS) * g


def _layer_norm(x, g, b):
    mu = jnp.mean(x, axis=-1, keepdims=True)
    xc = x - mu
    var = jnp.mean(xc * xc, axis=-1, keepdims=True)
    return xc * lax.rsqrt(var + EPS) * g + b


def _silu(z):
    return z * jax.nn.sigmoid(z)


def _gelu_tanh(x):
    c = math.sqrt(2.0 / math.pi)
    return 0.5 * x * (1.0 + jnp.tanh(c * (x + 0.044715 * (x * x * x))))


def _post_block(y_bf16_dot, x, p, g_ref, b_ref, gate_ref, proj_ref):
    h = _layer_norm(DEEPNORM_ALPHA * x + y_bf16_dot, g_ref[...], b_ref[...])
    gate = jax.nn.sigmoid(_dot(h.astype(BF16), gate_ref[...]))
    return h + gate * _dot(p.astype(BF16), proj_ref[...])


def _even_kernel(seq_len, tile,
                 xc_ref, xp_ref, p_ref, prc_ref, prp_ref, prn_ref,
                 win_ref, convw_ref, sink_ref, anorm_ref, bnorm_ref, wout_ref,
                 g_ref, b_ref, gate_ref, proj_ref,
                 o_ref,
                 kbuf, vbuf, q_a, q_b, bg_a, bg_b, sz_a, sz_b, cz_a, cz_b,
                 posbuf, ybuf, lg_a, lg_b, m_a, m_b):
    t = pl.program_id(0)

    @pl.when(t == 0)
    def _():
        for ref in (kbuf, vbuf, q_a, q_b, bg_a, bg_b, sz_a, sz_b, cz_a, cz_b):
            ref[...] = jnp.zeros(ref.shape, ref.dtype)

    shared = (seq_len, tile, xc_ref, xp_ref, p_ref, prc_ref, prp_ref, prn_ref,
              win_ref, convw_ref, sink_ref, anorm_ref, bnorm_ref, wout_ref,
              g_ref, b_ref, gate_ref, proj_ref, o_ref,
              kbuf, vbuf, posbuf, ybuf, lg_a, lg_b, m_a, m_b)

    @pl.when(t % 2 == 0)
    def _():
        _even_step(*shared, (q_a, bg_a, sz_a, cz_a), (q_b, bg_b, sz_b, cz_b))

    @pl.when(t % 2 == 1)
    def _():
        _even_step(*shared, (q_b, bg_b, sz_b, cz_b), (q_a, bg_a, sz_a, cz_a))


def _even_step(seq_len, tile, xc_ref, xp_ref, p_ref, prc_ref, prp_ref, prn_ref,
               win_ref, convw_ref, sink_ref, anorm_ref, bnorm_ref, wout_ref,
               g_ref, b_ref, gate_ref, proj_ref, o_ref,
               kbuf, vbuf, posbuf, ybuf, lg_a, lg_b, m_a, m_b, proj_bufs, mix_bufs):
    t = pl.program_id(0)
    tiles_per_seq = seq_len // tile
    sub = tile // BLOCK
    pq, pbg, psz, pcz = proj_bufs
    mq, mbg, msz, mcz = mix_bufs

    before_rows = pcz[tile - 8:tile, :]

    xb = xc_ref[...].astype(BF16)

    def seg(n, lo=0, hi=None):
        hi = _EV_OFF[n + 1] - _EV_OFF[n] if hi is None else hi
        return _dot(xb, win_ref[:, _EV_OFF[n] + lo:_EV_OFF[n] + hi])

    krow_p = pl.multiple_of(BLOCK + (t % tiles_per_seq) * tile, BLOCK)

    lane_t = lax.broadcasted_iota(jnp.int32, (tile, LANES), 1)
    low = lane_t < A_HEAD_DIM
    one = jnp.where(lane_t == A_HEAD_DIM, 1.0, 0.0)

    def slots(v, fill):
        return (jnp.where(low, v, fill),
                jnp.where(low, pltpu.roll(v, A_HEAD_DIM, axis=1), fill))

    def proj_kv():
        lo, hi = slots(seg(1), 0.0)
        kbuf[pl.ds(krow_p, tile), 0:SLOT] = lo.astype(BF16)
        kbuf[pl.ds(krow_p, tile), SLOT:] = hi.astype(BF16)
        lo, hi = slots(seg(2), one)
        vbuf[pl.ds(krow_p, tile), 0:SLOT] = lo.astype(BF16)
        vbuf[pl.ds(krow_p, tile), SLOT:] = hi.astype(BF16)

    piece_w = 2 * LANES

    def proj_q(part):
        q = seg(0, part * piece_w, (part + 1) * piece_w) * A_SCALE_LOG2E
        for n in range(piece_w // LANES):
            lo, hi = slots(q[:, n * LANES:(n + 1) * LANES], 0.0)
            h0 = 2 * (part * (piece_w // LANES) + n)
            pq[:, h0 * SLOT:(h0 + 1) * SLOT] = lo.astype(BF16)
            pq[:, (h0 + 1) * SLOT:(h0 + 2) * SLOT] = hi.astype(BF16)

    def proj_bg(part):
        lo, hi = part * piece_w, (part + 1) * piece_w
        pbg[:, lo:hi] = seg(3, lo, hi)

    def proj_cg(part):
        lo, hi = part * piece_w, (part + 1) * piece_w
        pcz[:, lo:hi] = seg(4, lo, hi)

    def proj_xin(part):
        lo, hi = part * piece_w, (part + 1) * piece_w
        pcz[:, lo:hi] = pcz[:, lo:hi] * seg(5, lo, hi)

    def proj_sz(part):
        lo, hi = part * piece_w, (part + 1) * piece_w
        psz[:, lo:hi] = _silu(seg(6, lo, hi))

    proj_kv()
    pieces = [functools.partial(fn, part)
              for fn, parts in ((proj_q, 2), (proj_bg, 2), (proj_cg, 2), (proj_xin, 2), (proj_sz, 4))
              for part in range(parts)]

    i = jnp.maximum(t - 1, 0) % tiles_per_seq
    krow_m = pl.multiple_of(i * tile, BLOCK)

    posbuf[:, 0:BLOCK] = prp_ref[...]
    posbuf[:, BLOCK:BLOCK + tile] = prc_ref[...]
    posbuf[:, BLOCK + tile:] = prn_ref[...]

    q_idx = lax.broadcasted_iota(jnp.int32, (BLOCK, 3 * BLOCK), 0)
    w_idx = lax.broadcasted_iota(jnp.int32, (BLOCK, 3 * BLOCK), 1)
    in_band = jnp.abs(w_idx - BLOCK - q_idx) <= WINDOW
    lane = lax.broadcasted_iota(jnp.int32, (BLOCK, LANES), 1)
    dists = {}

    def masked_dist(j):
        if j not in dists:
            k_glob = (i * sub + j) * BLOCK + w_idx - BLOCK
            valid = in_band & (k_glob >= 0) & (k_glob < seq_len)
            pos_q = _column(posbuf[:, (j + 1) * BLOCK:(j + 2) * BLOCK])
            d = jnp.concatenate([jnp.abs(pos_q - posbuf[:, (j + n) * BLOCK:(j + n + 1) * BLOCK])
                                 for n in range(3)], axis=1).astype(F32)
            dists[j] = jnp.where(valid, d, jnp.inf)
        return dists[j]

    def scores(j, g, lg_ref, m_ref):
        qrow = slice(j * BLOCK, (j + 1) * BLOCK)
        qg = jnp.concatenate([mq[qrow, (A_GROUP * g + hh) * SLOT:(A_GROUP * g + hh + 1) * SLOT]
                              for hh in range(A_GROUP)], axis=0)
        s = _dot_nt(qg, kbuf[pl.ds(krow_m + j * BLOCK, 3 * BLOCK), g * SLOT:(g + 1) * SLOT])
        d = masked_dist(j)
        for hh in range(A_GROUP):
            h = A_GROUP * g + hh
            r = slice(hh * BLOCK, (hh + 1) * BLOCK)
            lg = s[r, :] - (A_SLOPES[h] * LOG2E) * d
            lg_ref[r, :] = lg
            mx = jnp.maximum(jnp.max(lg, axis=-1, keepdims=True), sink_ref[h])
            m_ref[r, :] = jnp.broadcast_to(mx, (BLOCK, LANES))

    def attend(j, g, lg_ref, m_ref):
        rows = slice(j * BLOCK, (j + 1) * BLOCK)
        m = m_ref[...]
        e = jnp.concatenate([jnp.exp2(lg_ref[:, n * LANES:(n + 1) * LANES] - m).astype(BF16)
                             for n in range(3)], axis=1)
        pv = _dot(e, vbuf[pl.ds(krow_m + j * BLOCK, 3 * BLOCK), g * SLOT:(g + 1) * SLOT])
        outs = []
        for hh in range(A_GROUP):
            r = slice(hh * BLOCK, (hh + 1) * BLOCK)
            denom = (jnp.broadcast_to(pv[r, A_HEAD_DIM:A_HEAD_DIM + 1], (BLOCK, LANES))
                     + jnp.exp2(sink_ref[A_GROUP * g + hh] - m[r, :]))
            outs.append(pv[r, :] / denom)
        for pr in range(A_GROUP // 2):
            c0 = (A_GROUP // 2 * g + pr) * LANES
            ybuf[rows, c0:c0 + LANES] = jnp.where(
                lane < A_HEAD_DIM, outs[2 * pr], pltpu.roll(outs[2 * pr + 1], A_HEAD_DIM, axis=1))

    chains = [(j, g) for j in range(sub) for g in range(A_KV_HEADS)]
    bufs = [(lg_a, m_a), (lg_b, m_b)]
    scores(*chains[0], *bufs[0])
    late = [pieces.pop() for _ in range(4)][::-1]
    for c, (j, g) in enumerate(chains):
        if c + 1 < len(chains):
            scores(*chains[c + 1], *bufs[(c + 1) % 2])
        attend(j, g, *bufs[c % 2])
        pieces.pop(0)()
    assert not pieces

    late[0]()
    cz = mcz[...]
    row = lax.broadcasted_iota(jnp.int32, cz.shape, 0)
    prev_row = jnp.where(i > 0, before_rows[7:8, :], 0.0)
    next_row = jnp.where(i < tiles_per_seq - 1, pcz[0:1, :], 0.0)
    z_prev = jnp.where(row == 0, prev_row, pltpu.roll(cz, 1, axis=0))
    z_next = jnp.where(row == tile - 1, next_row, pltpu.roll(cz, tile - 1, axis=0))
    conv = (convw_ref[0:1, :] * z_prev + convw_ref[1:2, :] * cz
            + convw_ref[2:3, :] * z_next)
    yb = mbg[...] * conv

    late[1]()
    groups = [slice(r, r + tile // 2) for r in range(0, tile, tile // 2)]
    ya_n = [(_rms(ybuf[r, :], anorm_ref[...]) * msz[r, 0:A_WIDTH]).astype(BF16) for r in groups]
    yb_n = [(_rms(yb[r, :], bnorm_ref[...]) * msz[r, A_WIDTH:]).astype(BF16) for r in groups]
    emb = [_dot(p_ref[r, :].astype(BF16), proj_ref[...]) for r in groups]
    mix = [_dot(a, wout_ref[0:A_WIDTH, :]) + _dot(b, wout_ref[A_WIDTH:, :])
           for a, b in zip(ya_n, yb_n)]
    late[2]()
    h = [_layer_norm(DEEPNORM_ALPHA * xp_ref[r, :] + m, g_ref[...], b_ref[...])
         for m, r in zip(mix, groups)]
    late[3]()
    gate = [jax.nn.sigmoid(_dot(hh.astype(BF16), gate_ref[...])) for hh in h]
    for r, hh, gt, e in zip(groups, h, gate, emb):
        o_ref[r, :] = hh + gt * e


def _odd_proj_kernel(x_ref, pos_ref, w_ref, inv_ref, qn_ref, wq_ref,
                     kvn_ref, wkv_ref, vones_ref,
                     lng_ref, lnb_ref, ws_ref, bs_ref, dnorm_ref,
                     q_ref, k_ref, v_ref, yd_ref, szc_ref):
    tile = x_ref.shape[0]
    wide = C_HEADS * SLOT
    xb = x_ref[...].astype(BF16)

    def seg(i):
        return _dot(xb, w_ref[:, _OD_OFF[i]:_OD_OFF[i + 1]])

    packs = LANES // C_ROPE
    rb = tile // packs
    lane = lax.broadcasted_iota(jnp.int32, (rb, LANES), 1)
    packed = _column(pos_ref[0]).astype(F32)
    for u in range(1, packs):
        packed = jnp.where(lane >= u * C_ROPE, _column(pos_ref[u]).astype(F32), packed)
    ang = packed * inv_ref[...]
    cos_p = jnp.cos(ang)
    sin_p = jnp.sin(ang)
    rope_lane = (lane >= C_NOPE) & (lane < C_NOPE + C_ROPE)
    cos_blocks, sin_blocks = [], []
    for u in range(packs):
        shift = (C_NOPE - u * C_ROPE) % LANES
        cu = pltpu.roll(cos_p, shift, axis=1) if shift else cos_p
        su = pltpu.roll(sin_p, shift, axis=1) if shift else sin_p
        cos_blocks.append(jnp.where(rope_lane, cu, 1.0))
        sin_blocks.append(jnp.where(rope_lane, su, 0.0))
    cos_t = jnp.concatenate(cos_blocks, axis=0)
    sin_t = jnp.concatenate(sin_blocks, axis=0)
    first_half = lax.broadcasted_iota(jnp.int32, (tile, LANES), 1) < C_NOPE + C_ROPE // 2
    sin_s = jnp.where(first_half, -sin_t, sin_t)

    def rotary(t):
        swapped = jnp.where(first_half, pltpu.roll(t, LANES - C_ROPE // 2, axis=1),
                            pltpu.roll(t, C_ROPE // 2, axis=1))
        return t * cos_t + swapped * sin_s

    a = seg(0)
    kr = seg(1)
    cqn = _rms(a[:, 0:C_Q_RANK], qn_ref[...]).astype(BF16)
    ckvn = _rms(a[:, C_Q_RANK:], kvn_ref[...]).astype(BF16)
    du = seg(2)
    qa = _dot(cqn, wq_ref[...])
    gu = _gelu_tanh(du)
    dv = seg(3)
    for h in range(C_HEADS):
        sl = slice(h * SLOT, (h + 1) * SLOT)
        q_ref[:, sl] = (rotary(qa[:, sl]) * MLA_SCALE_LOG2E).astype(BF16)

    kr_rot = rotary(kr).astype(BF16)
    kv = _dot(jnp.concatenate([ckvn, kr_rot], axis=1), wkv_ref[...])
    vn = _layer_norm(_gelu_tanh(dv), lng_ref[...], lnb_ref[...]).astype(BF16)
    z = seg(4)
    k_ref[...] = kv[:, 0:wide].astype(BF16)
    v_ref[...] = jnp.transpose(kv[:, wide:] + vones_ref[...]).astype(BF16)

    n_ch = tile // D_CHUNK
    mixed = []
    for g in range(D_GROUPS):
        cols = slice(g * D_GROUP_DIM, (g + 1) * D_GROUP_DIM)
        rhs = jnp.concatenate([vn[c * D_CHUNK:(c + 1) * D_CHUNK, cols] for c in range(n_ch)], axis=1)
        mixed.append(_dot(ws_ref[g], rhs))
    sz = _silu(z)
    szc_ref[...] = sz[:, 0:C_WIDTH]
    chunks = [jnp.concatenate([mixed[g][:, c * D_GROUP_DIM:(c + 1) * D_GROUP_DIM]
                               for g in range(D_GROUPS)], axis=1) + bs_ref[...]
              for c in range(n_ch)]
    yd = gu * jnp.concatenate(chunks, axis=0)
    yd_ref[...] = (_rms(yd, dnorm_ref[...]) * sz[:, C_WIDTH:]).astype(BF16)


def _mla_kernel(seq_len, tq, tk, unroll, q_ref, k_ref, vt_ref, o_ref, s_a, s_b, m_a, m_b):
    n_qt = seq_len // tq
    n_c = seq_len // tk

    def scores(qt, hh, s_ref, m_ref):
        sl = slice(hh * SLOT, (hh + 1) * SLOT)
        q = q_ref[pl.ds(pl.multiple_of(qt * tq, tq), tq), sl]
        part = None
        for c in range(n_c):
            st = _dot_nt(k_ref[c * tk:(c + 1) * tk, sl], q)
            s_ref[c * tk:(c + 1) * tk, :] = st
            cm = jnp.max(st.reshape(tk // 8, 8, tq), axis=0)
            part = cm if part is None else jnp.maximum(part, cm)
        m_ref[...] = jnp.broadcast_to(jnp.max(part, axis=0, keepdims=True), m_ref.shape)

    def attend(hh, s_ref, m_ref):
        sl = slice(hh * SLOT, hh * SLOT + V_ROWS)
        m = m_ref[0:1, :]
        acc = None
        for c in range(n_c):
            pt = jnp.exp2(s_ref[c * tk:(c + 1) * tk, :] - m).astype(BF16)
            d = _dot(vt_ref[sl, c * tk:(c + 1) * tk], pt)
            acc = d if acc is None else acc + d
        return acc[0:C_V, :] / acc[C_V:C_V + 1, :]

    scores(0, 0, s_a, m_a)

    def one_tile(qt):
        scores(qt, 1, s_b, m_b)
        out0 = attend(0, s_a, m_a)
        out1 = attend(1, s_b, m_b)
        scores(jnp.minimum(qt + 1, n_qt - 1), 0, s_a, m_a)
        both = jnp.concatenate([out0, out1], axis=0)
        o_ref[pl.ds(pl.multiple_of(qt * tq, tq), tq), :] = jnp.transpose(both)

    def body(it, carry):
        for u in range(unroll):
            one_tile(it * unroll + u)
        return carry

    lax.fori_loop(0, n_qt // unroll, body, 0)


def _odd_out_kernel(yc_ref, szc_ref, yd_ref, x_ref, p_ref, cnorm_ref, wout_ref,
                    g_ref, b_ref, gate_ref, proj_ref, o_ref):
    tile = x_ref.shape[0]
    groups = [slice(r, r + tile // OUT_SPLIT) for r in range(0, tile, tile // OUT_SPLIT)]
    yc_n = [(_rms(yc_ref[r, :], cnorm_ref[...]) * szc_ref[r, :]).astype(BF16) for r in groups]
    emb = [_dot(p_ref[r, :].astype(BF16), proj_ref[...]) for r in groups]
    mix = [_dot(y, wout_ref[0:C_WIDTH, :]) + _dot(yd_ref[r, :], wout_ref[C_WIDTH:, :])
           for y, r in zip(yc_n, groups)]
    h = [_layer_norm(DEEPNORM_ALPHA * x_ref[r, :] + m, g_ref[...], b_ref[...])
         for m, r in zip(mix, groups)]
    gate = [jax.nn.sigmoid(_dot(hh.astype(BF16), gate_ref[...])) for hh in h]
    for r, hh, gt, e in zip(groups, h, gate, emb):
        o_ref[r, :] = hh + gt * e


def _full(shape):
    return pl.BlockSpec(shape, lambda *_: (0,) * len(shape))


def _once(shape):
    return pl.BlockSpec(shape, lambda *_: (0,) * len(shape), pipeline_mode=pl.Buffered(1))


def _params(sem):
    return pltpu.CompilerParams(dimension_semantics=sem, vmem_limit_bytes=VMEM_LIMIT)


def _even_layer(x, p, layer, positions, w_in, conv_w, sink, a_norm, b_norm, w_out,
                ln_g, ln_b, ple_proj, ple_gate):
    bsz, seq, _ = x.shape
    n_tok = bsz * seq
    tile = 512
    n_tiles = n_tok // tile
    tiles_per_seq = seq // tile
    sub = tile // BLOCK
    nb = seq // BLOCK
    q_w = A_HEADS * SLOT
    kv_w = A_KV_HEADS * SLOT
    x2 = x.reshape(n_tok, D_MODEL)
    pos_row = positions.reshape(bsz, 1, seq)

    cur = lambda t: jnp.minimum(t, n_tiles - 1)
    prv = lambda t: jnp.maximum(t - 1, 0)
    seq_of = lambda t: prv(t) // tiles_per_seq
    blk_of = lambda t: (prv(t) % tiles_per_seq) * sub
    in_specs = [
        pl.BlockSpec((tile, D_MODEL), lambda t: (cur(t), 0)),
        pl.BlockSpec((tile, D_MODEL), lambda t: (prv(t), 0)),
        pl.BlockSpec((None, tile, PLE_DIM), lambda t: (layer, prv(t), 0)),
        pl.BlockSpec((None, 1, tile), lambda t: (seq_of(t), 0, prv(t) % tiles_per_seq)),
        pl.BlockSpec((None, 1, BLOCK), lambda t: (seq_of(t), 0, jnp.maximum(blk_of(t) - 1, 0))),
        pl.BlockSpec((None, 1, BLOCK), lambda t: (seq_of(t), 0, jnp.minimum(blk_of(t) + sub, nb - 1))),
        _once((D_MODEL, _EV_OFF[-1])),
        _once((3, B_WIDTH)),
        pl.BlockSpec(memory_space=pltpu.SMEM),
        _once((1, A_WIDTH)), _once((1, B_WIDTH)), _once((D_MODEL, D_MODEL)),
        _once((1, D_MODEL)), _once((1, D_MODEL)), _once((D_MODEL, D_MODEL)),
        _once((PLE_DIM, D_MODEL)),
    ]
    out = pl.pallas_call(
        functools.partial(_even_kernel, seq, tile),
        grid=(n_tiles + 1,),
        in_specs=in_specs,
        out_specs=pl.BlockSpec((tile, D_MODEL), lambda t: (prv(t), 0)),
        out_shape=jax.ShapeDtypeStruct((n_tok, D_MODEL), F32),
        scratch_shapes=[pltpu.VMEM((seq + 2 * BLOCK, kv_w), BF16),
                        pltpu.VMEM((seq + 2 * BLOCK, kv_w), BF16),
                        pltpu.VMEM((tile, q_w), BF16), pltpu.VMEM((tile, q_w), BF16),
                        pltpu.VMEM((tile, B_WIDTH), F32), pltpu.VMEM((tile, B_WIDTH), F32),
                        pltpu.VMEM((tile, A_WIDTH + B_WIDTH), F32),
                        pltpu.VMEM((tile, A_WIDTH + B_WIDTH), F32),
                        pltpu.VMEM((tile, B_WIDTH), F32), pltpu.VMEM((tile, B_WIDTH), F32),
                        pltpu.VMEM((1, tile + 2 * BLOCK), jnp.int32),
                        pltpu.VMEM((tile, A_WIDTH), F32),
                        pltpu.VMEM((A_GROUP * BLOCK, 3 * BLOCK), F32),
                        pltpu.VMEM((A_GROUP * BLOCK, 3 * BLOCK), F32),
                        pltpu.VMEM((A_GROUP * BLOCK, LANES), F32),
                        pltpu.VMEM((A_GROUP * BLOCK, LANES), F32)],
        compiler_params=_params(("arbitrary",)),
        name="even_layer",
    )(x2, x2, p.reshape(DEPTH, n_tok, PLE_DIM), pos_row, pos_row, pos_row,
      w_in.astype(BF16), conv_w, sink * LOG2E, a_norm.reshape(1, -1), b_norm.reshape(1, -1),
      w_out.astype(BF16), ln_g.reshape(1, -1), ln_b.reshape(1, -1),
      ple_gate.astype(BF16), ple_proj.astype(BF16))
    return out.reshape(bsz, seq, D_MODEL)


def _odd_weights(w_in, w_uq, w_ukv, b_s):
    d = w_in.shape[0]
    o_kr = C_Q_RANK + C_KV_RANK
    o_du = o_kr + C_ROPE
    w_kr = w_in[:, o_kr:o_du]
    kr_tile = jnp.pad(w_kr, ((0, 0), (C_NOPE, LANES - C_NOPE - C_ROPE)))
    w1 = jnp.concatenate([w_in[:, :o_kr], kr_tile, w_in[:, o_du:]], axis=1).astype(BF16)

    uq = w_uq.reshape(C_Q_RANK, C_HEADS, C_NOPE + C_ROPE)
    pad = SLOT - C_NOPE - C_ROPE
    wq = jnp.pad(uq, ((0, 0), (0, 0), (0, pad)))
    ukv = w_ukv.reshape(C_KV_RANK, C_HEADS, C_NOPE + C_V)
    wkn = jnp.pad(ukv[..., :C_NOPE], ((0, 0), (0, 0), (0, SLOT - C_NOPE)))
    wv = jnp.pad(ukv[..., C_NOPE:], ((0, 0), (0, 0), (0, SLOT - C_V)))
    flat = lambda w: w.reshape(w.shape[0], C_HEADS * SLOT).astype(BF16)

    lane = jnp.arange(LANES)
    rope_lane = (lane >= C_NOPE) & (lane < C_NOPE + C_ROPE)
    wke = jnp.tile(jnp.where(rope_lane[:, None] & (lane[:, None] == lane[None, :]), 1.0, 0.0),
                   (1, C_HEADS)).astype(BF16)
    vones = jnp.tile(jnp.where(lane == C_V, 1.0, 0.0), C_HEADS).reshape(1, -1).astype(F32)
    half = C_ROPE // 2
    inv = ROPE_THETA ** (-jnp.arange(half, dtype=F32) / half)
    inv_lane = jnp.tile(inv, LANES // half).reshape(1, LANES)
    bs_full = jnp.repeat(b_s.T, D_GROUP_DIM, axis=1)
    wkv = jnp.concatenate([jnp.concatenate([flat(wkn), flat(wv)], axis=1),
                           jnp.concatenate([wke, jnp.zeros_like(wke)], axis=1)], axis=0)
    return w1, flat(wq), wkv, vones, inv_lane, bs_full


def _odd_layer(x, p, layer, positions, w_in, q_norm, w_uq, kv_norm, w_ukv, v_ln_g, v_ln_b,
               w_s, b_s, c_norm, d_norm, w_out, ln_g, ln_b, ple_proj, ple_gate):
    bsz, seq, _ = x.shape
    n_tok = bsz * seq
    w1, wq, wkv, vones, inv_lane, bs_full = _odd_weights(w_in, w_uq, w_ukv, b_s)
    tm = 512
    row_spec = lambda w: pl.BlockSpec((tm, w), lambda t: (t, 0))
    wide = C_HEADS * SLOT
    q, k, v, yd, szc = pl.pallas_call(
        _odd_proj_kernel,
        grid=(n_tok // tm,),
        in_specs=[row_spec(D_MODEL), pl.BlockSpec((tm // LANES, 1, LANES), lambda t: (t, 0, 0)),
                  _full(w1.shape), _full((1, LANES)),
                  _full((1, C_Q_RANK)), _full(wq.shape),
                  _full((1, C_KV_RANK)), _full(wkv.shape),
                  _full((1, wide)), _full((1, D_WIDTH)), _full((1, D_WIDTH)),
                  _full((D_GROUPS, D_CHUNK, D_CHUNK)), _full((D_CHUNK, D_WIDTH)),
                  _full((1, D_WIDTH))],
        out_specs=[row_spec(wide), row_spec(wide), pl.BlockSpec((wide, tm), lambda t: (0, t)),
                   row_spec(D_WIDTH),
                   row_spec(C_WIDTH)],
        out_shape=[jax.ShapeDtypeStruct((n_tok, wide), BF16),
                   jax.ShapeDtypeStruct((n_tok, wide), BF16),
                   jax.ShapeDtypeStruct((wide, n_tok), BF16),
                   jax.ShapeDtypeStruct((n_tok, D_WIDTH), BF16),
                   jax.ShapeDtypeStruct((n_tok, C_WIDTH), F32)],
        compiler_params=_params(("parallel",)),
        name="odd_proj",
    )(x.reshape(n_tok, D_MODEL), positions.reshape(n_tok // LANES, 1, LANES), w1, inv_lane,
      q_norm.reshape(1, -1), wq, kv_norm.reshape(1, -1), wkv, vones,
      v_ln_g.reshape(1, -1), v_ln_b.reshape(1, -1), w_s.astype(BF16), bs_full,
      d_norm.reshape(1, -1))

    tq = 256
    tk = 512
    q = q.reshape(bsz, seq, wide)
    k = k.reshape(bsz, seq, wide)
    pair = lambda w: pl.BlockSpec((None, seq, w), lambda b, h: (b, 0, h))
    yc = pl.pallas_call(
        functools.partial(_mla_kernel, seq, tq, tk, 8),
        grid=(bsz, C_HEADS // 2),
        in_specs=[pair(2 * SLOT), pair(2 * SLOT),
                  pl.BlockSpec((2 * SLOT, seq), lambda b, h: (h, b))],
        out_specs=pair(2 * C_V),
        out_shape=jax.ShapeDtypeStruct((bsz, seq, C_WIDTH), F32),
        scratch_shapes=[pltpu.VMEM((seq, tq), F32), pltpu.VMEM((seq, tq), F32),
                        pltpu.VMEM((8, tq), F32), pltpu.VMEM((8, tq), F32)],
        compiler_params=_params(("parallel", "parallel")),
        name="mla_attn",
    )(q, k, v)

    to = 1024
    out_spec = lambda w: pl.BlockSpec((to, w), lambda t: (t, 0))
    out = pl.pallas_call(
        _odd_out_kernel,
        grid=(n_tok // to,),
        in_specs=[out_spec(C_WIDTH), out_spec(C_WIDTH), out_spec(D_WIDTH), out_spec(D_MODEL),
                  pl.BlockSpec((None, to, PLE_DIM), lambda t: (layer, t, 0)),
                  _once((1, C_WIDTH)), _once((D_MODEL, D_MODEL)), _once((1, D_MODEL)),
                  _once((1, D_MODEL)), _once((D_MODEL, D_MODEL)), _once((PLE_DIM, D_MODEL))],
        out_specs=out_spec(D_MODEL),
        out_shape=jax.ShapeDtypeStruct((n_tok, D_MODEL), F32),
        compiler_params=_params(("parallel",)),
        name="odd_out",
    )(yc.reshape(n_tok, C_WIDTH), szc, yd, x.reshape(n_tok, D_MODEL),
      p.reshape(DEPTH, n_tok, PLE_DIM), c_norm.reshape(1, -1), w_out.astype(BF16),
      ln_g.reshape(1, -1), ln_b.reshape(1, -1), ple_gate.astype(BF16), ple_proj.astype(BF16))
    return out.reshape(bsz, seq, D_MODEL)


def kernel(x, p, positions, ev_w_in, ev_conv_w, ev_sink, ev_a_norm, ev_b_norm, ev_w_out, od_w_in, od_q_norm, od_w_uq, od_kv_norm, od_w_ukv, od_v_ln_g, od_v_ln_b, od_w_s, od_b_s, od_c_norm, od_d_norm, od_w_out, post_ln_g, post_ln_b, ple_proj, ple_gate):
    for i in range(DEPTH):
        j = i // 2
        if i % 2 == 0:
            x = _even_layer(x, p, i, positions, ev_w_in[j], ev_conv_w[j], ev_sink[j],
                            ev_a_norm[j], ev_b_norm[j], ev_w_out[j],
                            post_ln_g[i], post_ln_b[i], ple_proj[i], ple_gate[i])
        else:
            x = _odd_layer(x, p, i, positions, od_w_in[j], od_q_norm[j], od_w_uq[j],
                           od_kv_norm[j], od_w_ukv[j], od_v_ln_g[j], od_v_ln_b[j],
                           od_w_s[j], od_b_s[j], od_c_norm[j], od_d_norm[j], od_w_out[j],
                           post_ln_g[i], post_ln_b[i], ple_proj[i], ple_gate[i])
    return x
```

```python
import functools
import math

import jax
import jax.numpy as jnp
from jax import lax
from jax.experimental import pallas as pl
from jax.experimental.pallas import tpu as pltpu

F32 = jnp.float32
BF16 = jnp.bfloat16

D_MODEL = 1024
DEPTH = 2
PLE_DIM = 256
DEEPNORM_ALPHA = (2.0 * DEPTH) ** 0.25
BLOCK = 128
EPS = 1e-6
A_HEAD_DIM = 64
A_HEADS = 8
A_KV_HEADS = 2
A_GROUP = A_HEADS // A_KV_HEADS
A_WIDTH = A_HEADS * A_HEAD_DIM
WINDOW = 128
B_WIDTH = 512
C_HEADS = 8
C_NOPE = 64
C_ROPE = 32
C_V = 64
C_WIDTH = C_HEADS * C_V
C_Q_RANK = 256
C_KV_RANK = 128
ROPE_THETA = 10000.0
D_WIDTH = 512
D_GROUPS = 4
D_GROUP_DIM = D_WIDTH // D_GROUPS
D_CHUNK = 128

LANES = 128
SLOT = LANES
VMEM_LIMIT = 56 * 1024 * 1024
V_ROWS = 80
OUT_SPLIT = 4

_EV_OFF = (0, 512, 640, 768, 1280, 1792, 2304, 3328)
_OD_OFF = (0, 384, 512, 1024, 1536, 2560)

A_SLOPES = tuple(2.0 ** (-8.0 * (h + 1) / A_HEADS) for h in range(A_HEADS))
LOG2E = math.log2(math.e)
A_SCALE_LOG2E = A_HEAD_DIM ** -0.5 * LOG2E
MLA_SCALE_LOG2E = (C_NOPE + C_ROPE) ** -0.5 * LOG2E


def _dot(a, b):
    return jnp.dot(a, b, preferred_element_type=F32)


def _dot_nt(a, b):
    return lax.dot_general(a, b, (((1,), (1,)), ((), ())), preferred_element_type=F32)


def _column(row):
    return jnp.transpose(jnp.broadcast_to(row, (LANES, LANES)))


def _rms(x, g):
    return x * lax.rsqrt(jnp.mean(x * x, axis=-1, keepdims=True) + EPS) * g


def _layer_norm(x, g, b):
    mu = jnp.mean(x, axis=-1, keepdims=True)
    xc = x - mu
    var = jnp.mean(xc * xc, axis=-1, keepdims=True)
    return xc * lax.rsqrt(var + EPS) * g + b


def _silu(z):
    return z * jax.nn.sigmoid(z)


def _gelu_tanh(x):
    c = math.sqrt(2.0 / math.pi)
    return 0.5 * x * (1.0 + jnp.tanh(c * (x + 0.044715 * (x * x * x))))


def _post_block(y_bf16_dot, x, p, g_ref, b_ref, gate_ref, proj_ref):
    h = _layer_norm(DEEPNORM_ALPHA * x + y_bf16_dot, g_ref[...], b_ref[...])
    gate = jax.nn.sigmoid(_dot(h.astype(BF16), gate_ref[...]))
    return h + gate * _dot(p.astype(BF16), proj_ref[...])


def _even_kernel(seq_len, tile,
                 xc_ref, xp_ref, p_ref, prc_ref, prp_ref, prn_ref,
                 win_ref, convw_ref, sink_ref, anorm_ref, bnorm_ref, wout_ref,
                 g_ref, b_ref, gate_ref, proj_ref,
                 o_ref,
                 kbuf, vbuf, q_a, q_b, bg_a, bg_b, sz_a, sz_b, cz_a, cz_b,
                 posbuf, ybuf, lg_a, lg_b, m_a, m_b):
    t = pl.program_id(0)

    @pl.when(t == 0)
    def _():
        for ref in (kbuf, vbuf, q_a, q_b, bg_a, bg_b, sz_a, sz_b, cz_a, cz_b):
            ref[...] = jnp.zeros(ref.shape, ref.dtype)

    shared = (seq_len, tile, xc_ref, xp_ref, p_ref, prc_ref, prp_ref, prn_ref,
              win_ref, convw_ref, sink_ref, anorm_ref, bnorm_ref, wout_ref,
              g_ref, b_ref, gate_ref, proj_ref, o_ref,
              kbuf, vbuf, posbuf, ybuf, lg_a, lg_b, m_a, m_b)

    @pl.when(t % 2 == 0)
    def _():
        _even_step(*shared, (q_a, bg_a, sz_a, cz_a), (q_b, bg_b, sz_b, cz_b))

    @pl.when(t % 2 == 1)
    def _():
        _even_step(*shared, (q_b, bg_b, sz_b, cz_b), (q_a, bg_a, sz_a, cz_a))


def _even_step(seq_len, tile, xc_ref, xp_ref, p_ref, prc_ref, prp_ref, prn_ref,
               win_ref, convw_ref, sink_ref, anorm_ref, bnorm_ref, wout_ref,
               g_ref, b_ref, gate_ref, proj_ref, o_ref,
               kbuf, vbuf, posbuf, ybuf, lg_a, lg_b, m_a, m_b, proj_bufs, mix_bufs):
    t = pl.program_id(0)
    tiles_per_seq = seq_len // tile
    sub = tile // BLOCK
    pq, pbg, psz, pcz = proj_bufs
    mq, mbg, msz, mcz = mix_bufs

    before_rows = pcz[tile - 8:tile, :]

    xb = xc_ref[...].astype(BF16)

    def seg(n, lo=0, hi=None):
        hi = _EV_OFF[n + 1] - _EV_OFF[n] if hi is None else hi
        return _dot(xb, win_ref[:, _EV_OFF[n] + lo:_EV_OFF[n] + hi])

    krow_p = pl.multiple_of(BLOCK + (t % tiles_per_seq) * tile, BLOCK)

    lane_t = lax.broadcasted_iota(jnp.int32, (tile, LANES), 1)
    low = lane_t < A_HEAD_DIM
    one = jnp.where(lane_t == A_HEAD_DIM, 1.0, 0.0)

    def slots(v, fill):
        return (jnp.where(low, v, fill),
                jnp.where(low, pltpu.roll(v, A_HEAD_DIM, axis=1), fill))

    def proj_kv():
        lo, hi = slots(seg(1), 0.0)
        kbuf[pl.ds(krow_p, tile), 0:SLOT] = lo.astype(BF16)
        kbuf[pl.ds(krow_p, tile), SLOT:] = hi.astype(BF16)
        lo, hi = slots(seg(2), one)
        vbuf[pl.ds(krow_p, tile), 0:SLOT] = lo.astype(BF16)
        vbuf[pl.ds(krow_p, tile), SLOT:] = hi.astype(BF16)

    piece_w = 2 * LANES

    def proj_q(part):
        q = seg(0, part * piece_w, (part + 1) * piece_w) * A_SCALE_LOG2E
        for n in range(piece_w // LANES):
            lo, hi = slots(q[:, n * LANES:(n + 1) * LANES], 0.0)
            h0 = 2 * (part * (piece_w // LANES) + n)
            pq[:, h0 * SLOT:(h0 + 1) * SLOT] = lo.astype(BF16)
            pq[:, (h0 + 1) * SLOT:(h0 + 2) * SLOT] = hi.astype(BF16)

    def proj_bg(part):
        lo, hi = part * piece_w, (part + 1) * piece_w
        pbg[:, lo:hi] = seg(3, lo, hi)

    def proj_cg(part):
        lo, hi = part * piece_w, (part + 1) * piece_w
        pcz[:, lo:hi] = seg(4, lo, hi)

    def proj_xin(part):
        lo, hi = part * piece_w, (part + 1) * piece_w
        pcz[:, lo:hi] = pcz[:, lo:hi] * seg(5, lo, hi)

    def proj_sz(part):
        lo, hi = part * piece_w, (part + 1) * piece_w
        psz[:, lo:hi] = _silu(seg(6, lo, hi))

    proj_kv()
    pieces = [functools.partial(fn, part)
              for fn, parts in ((proj_q, 2), (proj_bg, 2), (proj_cg, 2), (proj_xin, 2), (proj_sz, 4))
              for part in range(parts)]

    i = jnp.maximum(t - 1, 0) % tiles_per_seq
    krow_m = pl.multiple_of(i * tile, BLOCK)

    posbuf[:, 0:BLOCK] = prp_ref[...]
    posbuf[:, BLOCK:BLOCK + tile] = prc_ref[...]
    posbuf[:, BLOCK + tile:] = prn_ref[...]

    q_idx = lax.broadcasted_iota(jnp.int32, (BLOCK, 3 * BLOCK), 0)
    w_idx = lax.broadcasted_iota(jnp.int32, (BLOCK, 3 * BLOCK), 1)
    in_band = jnp.abs(w_idx - BLOCK - q_idx) <= WINDOW
    lane = lax.broadcasted_iota(jnp.int32, (BLOCK, LANES), 1)
    dists = {}

    def masked_dist(j):
        if j not in dists:
            k_glob = (i * sub + j) * BLOCK + w_idx - BLOCK
            valid = in_band & (k_glob >= 0) & (k_glob < seq_len)
            pos_q = _column(posbuf[:, (j + 1) * BLOCK:(j + 2) * BLOCK])
            d = jnp.concatenate([jnp.abs(pos_q - posbuf[:, (j + n) * BLOCK:(j + n + 1) * BLOCK])
                                 for n in range(3)], axis=1).astype(F32)
            dists[j] = jnp.where(valid, d, jnp.inf)
        return dists[j]

    def scores(j, g, lg_ref, m_ref):
        qrow = slice(j * BLOCK, (j + 1) * BLOCK)
        qg = jnp.concatenate([mq[qrow, (A_GROUP * g + hh) * SLOT:(A_GROUP * g + hh + 1) * SLOT]
                              for hh in range(A_GROUP)], axis=0)
        s = _dot_nt(qg, kbuf[pl.ds(krow_m + j * BLOCK, 3 * BLOCK), g * SLOT:(g + 1) * SLOT])
        d = masked_dist(j)
        for hh in range(A_GROUP):
            h = A_GROUP * g + hh
            r = slice(hh * BLOCK, (hh + 1) * BLOCK)
            lg = s[r, :] - (A_SLOPES[h] * LOG2E) * d
            lg_ref[r, :] = lg
            mx = jnp.maximum(jnp.max(lg, axis=-1, keepdims=True), sink_ref[h])
            m_ref[r, :] = jnp.broadcast_to(mx, (BLOCK, LANES))

    def attend(j, g, lg_ref, m_ref):
        rows = slice(j * BLOCK, (j + 1) * BLOCK)
        m = m_ref[...]
        e = jnp.concatenate([jnp.exp2(lg_ref[:, n * LANES:(n + 1) * LANES] - m).astype(BF16)
                             for n in range(3)], axis=1)
        pv = _dot(e, vbuf[pl.ds(krow_m + j * BLOCK, 3 * BLOCK), g * SLOT:(g + 1) * SLOT])
        outs = []
        for hh in range(A_GROUP):
            r = slice(hh * BLOCK, (hh + 1) * BLOCK)
            denom = (jnp.broadcast_to(pv[r, A_HEAD_DIM:A_HEAD_DIM + 1], (BLOCK, LANES))
                     + jnp.exp2(sink_ref[A_GROUP * g + hh] - m[r, :]))
            outs.append(pv[r, :] / denom)
        for pr in range(A_GROUP // 2):
            c0 = (A_GROUP // 2 * g + pr) * LANES
            ybuf[rows, c0:c0 + LANES] = jnp.where(
                lane < A_HEAD_DIM, outs[2 * pr], pltpu.roll(outs[2 * pr + 1], A_HEAD_DIM, axis=1))

    chains = [(j, g) for j in range(sub) for g in range(A_KV_HEADS)]
    bufs = [(lg_a, m_a), (lg_b, m_b)]
    scores(*chains[0], *bufs[0])
    late = [pieces.pop() for _ in range(4)][::-1]
    for c, (j, g) in enumerate(chains):
        if c + 1 < len(chains):
            scores(*chains[c + 1], *bufs[(c + 1) % 2])
        attend(j, g, *bufs[c % 2])
        pieces.pop(0)()
    assert not pieces

    late[0]()
    cz = mcz[...]
    row = lax.broadcasted_iota(jnp.int32, cz.shape, 0)
    prev_row = jnp.where(i > 0, before_rows[7:8, :], 0.0)
    next_row = jnp.where(i < tiles_per_seq - 1, pcz[0:1, :], 0.0)
    z_prev = jnp.where(row == 0, prev_row, pltpu.roll(cz, 1, axis=0))
    z_next = jnp.where(row == tile - 1, next_row, pltpu.roll(cz, tile - 1, axis=0))
    conv = (convw_ref[0:1, :] * z_prev + convw_ref[1:2, :] * cz
            + convw_ref[2:3, :] * z_next)
    yb = mbg[...] * conv

    late[1]()
    groups = [slice(r, r + tile // 2) for r in range(0, tile, tile // 2)]
    ya_n = [(_rms(ybuf[r, :], anorm_ref[...]) * msz[r, 0:A_WIDTH]).astype(BF16) for r in groups]
    yb_n = [(_rms(yb[r, :], bnorm_ref[...]) * msz[r, A_WIDTH:]).astype(BF16) for r in groups]
    emb = [_dot(p_ref[r, :].astype(BF16), proj_ref[...]) for r in groups]
    mix = [_dot(a, wout_ref[0:A_WIDTH, :]) + _dot(b, wout_ref[A_WIDTH:, :])
           for a, b in zip(ya_n, yb_n)]
    late[2]()
    h = [_layer_norm(DEEPNORM_ALPHA * xp_ref[r, :] + m, g_ref[...], b_ref[...])
         for m, r in zip(mix, groups)]
    late[3]()
    gate = [jax.nn.sigmoid(_dot(hh.astype(BF16), gate_ref[...])) for hh in h]
    for r, hh, gt, e in zip(groups, h, gate, emb):
        o_ref[r, :] = hh + gt * e


def _odd_proj_kernel(x_ref, pos_ref, w_ref, inv_ref, qn_ref, wq_ref,
                     kvn_ref, wkv_ref, vones_ref,
                     lng_ref, lnb_ref, ws_ref, bs_ref, dnorm_ref,
                     q_ref, k_ref, v_ref, yd_ref, szc_ref):
    tile = x_ref.shape[0]
    wide = C_HEADS * SLOT
    xb = x_ref[...].astype(BF16)

    def seg(i):
        return _dot(xb, w_ref[:, _OD_OFF[i]:_OD_OFF[i + 1]])

    packs = LANES // C_ROPE
    rb = tile // packs
    lane = lax.broadcasted_iota(jnp.int32, (rb, LANES), 1)
    packed = _column(pos_ref[0]).astype(F32)
    for u in range(1, packs):
        packed = jnp.where(lane >= u * C_ROPE, _column(pos_ref[u]).astype(F32), packed)
    ang = packed * inv_ref[...]
    cos_p = jnp.cos(ang)
    sin_p = jnp.sin(ang)
    rope_lane = (lane >= C_NOPE) & (lane < C_NOPE + C_ROPE)
    cos_blocks, sin_blocks = [], []
    for u in range(packs):
        shift = (C_NOPE - u * C_ROPE) % LANES
        cu = pltpu.roll(cos_p, shift, axis=1) if shift else cos_p
        su = pltpu.roll(sin_p, shift, axis=1) if shift else sin_p
        cos_blocks.append(jnp.where(rope_lane, cu, 1.0))
        sin_blocks.append(jnp.where(rope_lane, su, 0.0))
    cos_t = jnp.concatenate(cos_blocks, axis=0)
    sin_t = jnp.concatenate(sin_blocks, axis=0)
    first_half = lax.broadcasted_iota(jnp.int32, (tile, LANES), 1) < C_NOPE + C_ROPE // 2
    sin_s = jnp.where(first_half, -sin_t, sin_t)

    def rotary(t):
        swapped = jnp.where(first_half, pltpu.roll(t, LANES - C_ROPE // 2, axis=1),
                            pltpu.roll(t, C_ROPE // 2, axis=1))
        return t * cos_t + swapped * sin_s

    a = seg(0)
    kr = seg(1)
    cqn = _rms(a[:, 0:C_Q_RANK], qn_ref[...]).astype(BF16)
    ckvn = _rms(a[:, C_Q_RANK:], kvn_ref[...]).astype(BF16)
    du = seg(2)
    qa = _dot(cqn, wq_ref[...])
    gu = _gelu_tanh(du)
    dv = seg(3)
    for h in range(C_HEADS):
        sl = slice(h * SLOT, (h + 1) * SLOT)
        q_ref[:, sl] = (rotary(qa[:, sl]) * MLA_SCALE_LOG2E).astype(BF16)

    kr_rot = rotary(kr).astype(BF16)
    kv = _dot(jnp.concatenate([ckvn, kr_rot], axis=1), wkv_ref[...])
    vn = _layer_norm(_gelu_tanh(dv), lng_ref[...], lnb_ref[...]).astype(BF16)
    z = seg(4)
    k_ref[...] = kv[:, 0:wide].astype(BF16)
    v_ref[...] = jnp.transpose(kv[:, wide:] + vones_ref[...]).astype(BF16)

    n_ch = tile // D_CHUNK
    mixed = []
    for g in range(D_GROUPS):
        cols = slice(g * D_GROUP_DIM, (g + 1) * D_GROUP_DIM)
        rhs = jnp.concatenate([vn[c * D_CHUNK:(c + 1) * D_CHUNK, cols] for c in range(n_ch)], axis=1)
        mixed.append(_dot(ws_ref[g], rhs))
    sz = _silu(z)
    szc_ref[...] = sz[:, 0:C_WIDTH]
    chunks = [jnp.concatenate([mixed[g][:, c * D_GROUP_DIM:(c + 1) * D_GROUP_DIM]
                               for g in range(D_GROUPS)], axis=1) + bs_ref[...]
              for c in range(n_ch)]
    yd = gu * jnp.concatenate(chunks, axis=0)
    yd_ref[...] = (_rms(yd, dnorm_ref[...]) * sz[:, C_WIDTH:]).astype(BF16)


def _mla_kernel(seq_len, tq, tk, unroll, q_ref, k_ref, vt_ref, o_ref, s_a, s_b, m_a, m_b):
    n_qt = seq_len // tq
    n_c = seq_len // tk

    def scores(qt, hh, s_ref, m_ref):
        sl = slice(hh * SLOT, (hh + 1) * SLOT)
        q = q_ref[pl.ds(pl.multiple_of(qt * tq, tq), tq), sl]
        part = None
        for c in range(n_c):
            st = _dot_nt(k_ref[c * tk:(c + 1) * tk, sl], q)
            s_ref[c * tk:(c + 1) * tk, :] = st
            cm = jnp.max(st.reshape(tk // 8, 8, tq), axis=0)
            part = cm if part is None else jnp.maximum(part, cm)
        m_ref[...] = jnp.broadcast_to(jnp.max(part, axis=0, keepdims=True), m_ref.shape)

    def attend(hh, s_ref, m_ref):
        sl = slice(hh * SLOT, hh * SLOT + V_ROWS)
        m = m_ref[0:1, :]
        acc = None
        for c in range(n_c):
            pt = jnp.exp2(s_ref[c * tk:(c + 1) * tk, :] - m).astype(BF16)
            d = _dot(vt_ref[sl, c * tk:(c + 1) * tk], pt)
            acc = d if acc is None else acc + d
        return acc[0:C_V, :] / acc[C_V:C_V + 1, :]

    scores(0, 0, s_a, m_a)

    def one_tile(qt):
        scores(qt, 1, s_b, m_b)
        out0 = attend(0, s_a, m_a)
        out1 = attend(1, s_b, m_b)
        scores(jnp.minimum(qt + 1, n_qt - 1), 0, s_a, m_a)
        both = jnp.concatenate([out0, out1], axis=0)
        o_ref[pl.ds(pl.multiple_of(qt * tq, tq), tq), :] = jnp.transpose(both)

    def body(it, carry):
        for u in range(unroll):
            one_tile(it * unroll + u)
        return carry

    lax.fori_loop(0, n_qt // unroll, body, 0)


def _odd_out_kernel(yc_ref, szc_ref, yd_ref, x_ref, p_ref, cnorm_ref, wout_ref,
                    g_ref, b_ref, gate_ref, proj_ref, o_ref):
    tile = x_ref.shape[0]
    groups = [slice(r, r + tile // OUT_SPLIT) for r in range(0, tile, tile // OUT_SPLIT)]
    yc_n = [(_rms(yc_ref[r, :], cnorm_ref[...]) * szc_ref[r, :]).astype(BF16) for r in groups]
    emb = [_dot(p_ref[r, :].astype(BF16), proj_ref[...]) for r in groups]
    mix = [_dot(y, wout_ref[0:C_WIDTH, :]) + _dot(yd_ref[r, :], wout_ref[C_WIDTH:, :])
           for y, r in zip(yc_n, groups)]
    h = [_layer_norm(DEEPNORM_ALPHA * x_ref[r, :] + m, g_ref[...], b_ref[...])
         for m, r in zip(mix, groups)]
    gate = [jax.nn.sigmoid(_dot(hh.astype(BF16), gate_ref[...])) for hh in h]
    for r, hh, gt, e in zip(groups, h, gate, emb):
        o_ref[r, :] = hh + gt * e


def _full(shape):
    return pl.BlockSpec(shape, lambda *_: (0,) * len(shape))


def _once(shape):
    return pl.BlockSpec(shape, lambda *_: (0,) * len(shape), pipeline_mode=pl.Buffered(1))


def _params(sem):
    return pltpu.CompilerParams(dimension_semantics=sem, vmem_limit_bytes=VMEM_LIMIT)


def _even_layer(x, p, layer, positions, w_in, conv_w, sink, a_norm, b_norm, w_out,
                ln_g, ln_b, ple_proj, ple_gate):
    bsz, seq, _ = x.shape
    n_tok = bsz * seq
    tile = 512
    n_tiles = n_tok // tile
    tiles_per_seq = seq // tile
    sub = tile // BLOCK
    nb = seq // BLOCK
    q_w = A_HEADS * SLOT
    kv_w = A_KV_HEADS * SLOT
    x2 = x.reshape(n_tok, D_MODEL)
    pos_row = positions.reshape(bsz, 1, seq)

    cur = lambda t: jnp.minimum(t, n_tiles - 1)
    prv = lambda t: jnp.maximum(t - 1, 0)
    seq_of = lambda t: prv(t) // tiles_per_seq
    blk_of = lambda t: (prv(t) % tiles_per_seq) * sub
    in_specs = [
        pl.BlockSpec((tile, D_MODEL), lambda t: (cur(t), 0)),
        pl.BlockSpec((tile, D_MODEL), lambda t: (prv(t), 0)),
        pl.BlockSpec((None, tile, PLE_DIM), lambda t: (layer, prv(t), 0)),
        pl.BlockSpec((None, 1, tile), lambda t: (seq_of(t), 0, prv(t) % tiles_per_seq)),
        pl.BlockSpec((None, 1, BLOCK), lambda t: (seq_of(t), 0, jnp.maximum(blk_of(t) - 1, 0))),
        pl.BlockSpec((None, 1, BLOCK), lambda t: (seq_of(t), 0, jnp.minimum(blk_of(t) + sub, nb - 1))),
        _once((D_MODEL, _EV_OFF[-1])),
        _once((3, B_WIDTH)),
        pl.BlockSpec(memory_space=pltpu.SMEM),
        _once((1, A_WIDTH)), _once((1, B_WIDTH)), _once((D_MODEL, D_MODEL)),
        _once((1, D_MODEL)), _once((1, D_MODEL)), _once((D_MODEL, D_MODEL)),
        _once((PLE_DIM, D_MODEL)),
    ]
    out = pl.pallas_call(
        functools.partial(_even_kernel, seq, tile),
        grid=(n_tiles + 1,),
        in_specs=in_specs,
        out_specs=pl.BlockSpec((tile, D_MODEL), lambda t: (prv(t), 0)),
        out_shape=jax.ShapeDtypeStruct((n_tok, D_MODEL), F32),
        scratch_shapes=[pltpu.VMEM((seq + 2 * BLOCK, kv_w), BF16),
                        pltpu.VMEM((seq + 2 * BLOCK, kv_w), BF16),
                        pltpu.VMEM((tile, q_w), BF16), pltpu.VMEM((tile, q_w), BF16),
                        pltpu.VMEM((tile, B_WIDTH), F32), pltpu.VMEM((tile, B_WIDTH), F32),
                        pltpu.VMEM((tile, A_WIDTH + B_WIDTH), F32),
                        pltpu.VMEM((tile, A_WIDTH + B_WIDTH), F32),
                        pltpu.VMEM((tile, B_WIDTH), F32), pltpu.VMEM((tile, B_WIDTH), F32),
                        pltpu.VMEM((1, tile + 2 * BLOCK), jnp.int32),
                        pltpu.VMEM((tile, A_WIDTH), F32),
                        pltpu.VMEM((A_GROUP * BLOCK, 3 * BLOCK), F32),
                        pltpu.VMEM((A_GROUP * BLOCK, 3 * BLOCK), F32),
                        pltpu.VMEM((A_GROUP * BLOCK, LANES), F32),
                        pltpu.VMEM((A_GROUP * BLOCK, LANES), F32)],
        compiler_params=_params(("arbitrary",)),
        name="even_layer",
    )(x2, x2, p.reshape(DEPTH, n_tok, PLE_DIM), pos_row, pos_row, pos_row,
      w_in.astype(BF16), conv_w, sink * LOG2E, a_norm.reshape(1, -1), b_norm.reshape(1, -1),
      w_out.astype(BF16), ln_g.reshape(1, -1), ln_b.reshape(1, -1),
      ple_gate.astype(BF16), ple_proj.astype(BF16))
    return out.reshape(bsz, seq, D_MODEL)


def _odd_weights(w_in, w_uq, w_ukv, b_s):
    d = w_in.shape[0]
    o_kr = C_Q_RANK + C_KV_RANK
    o_du = o_kr + C_ROPE
    w_kr = w_in[:, o_kr:o_du]
    kr_tile = jnp.pad(w_kr, ((0, 0), (C_NOPE, LANES - C_NOPE - C_ROPE)))
    w1 = jnp.concatenate([w_in[:, :o_kr], kr_tile, w_in[:, o_du:]], axis=1).astype(BF16)

    uq = w_uq.reshape(C_Q_RANK, C_HEADS, C_NOPE + C_ROPE)
    pad = SLOT - C_NOPE - C_ROPE
    wq = jnp.pad(uq, ((0, 0), (0, 0), (0, pad)))
    ukv = w_ukv.reshape(C_KV_RANK, C_HEADS, C_NOPE + C_V)
    wkn = jnp.pad(ukv[..., :C_NOPE], ((0, 0), (0, 0), (0, SLOT - C_NOPE)))
    wv = jnp.pad(ukv[..., C_NOPE:], ((0, 0), (0, 0), (0, SLOT - C_V)))
    flat = lambda w: w.reshape(w.shape[0], C_HEADS * SLOT).astype(BF16)

    lane = jnp.arange(LANES)
    rope_lane = (lane >= C_NOPE) & (lane < C_NOPE + C_ROPE)
    wke = jnp.tile(jnp.where(rope_lane[:, None] & (lane[:, None] == lane[None, :]), 1.0, 0.0),
                   (1, C_HEADS)).astype(BF16)
    vones = jnp.tile(jnp.where(lane == C_V, 1.0, 0.0), C_HEADS).reshape(1, -1).astype(F32)
    half = C_ROPE // 2
    inv = ROPE_THETA ** (-jnp.arange(half, dtype=F32) / half)
    inv_lane = jnp.tile(inv, LANES // half).reshape(1, LANES)
    bs_full = jnp.repeat(b_s.T, D_GROUP_DIM, axis=1)
    wkv = jnp.concatenate([jnp.concatenate([flat(wkn), flat(wv)], axis=1),
                           jnp.concatenate([wke, jnp.zeros_like(wke)], axis=1)], axis=0)
    return w1, flat(wq), wkv, vones, inv_lane, bs_full


def _odd_layer(x, p, layer, positions, w_in, q_norm, w_uq, kv_norm, w_ukv, v_ln_g, v_ln_b,
               w_s, b_s, c_norm, d_norm, w_out, ln_g, ln_b, ple_proj, ple_gate):
    bsz, seq, _ = x.shape
    n_tok = bsz * seq
    w1, wq, wkv, vones, inv_lane, bs_full = _odd_weights(w_in, w_uq, w_ukv, b_s)
    tm = 512
    row_spec = lambda w: pl.BlockSpec((tm, w), lambda t: (t, 0))
    wide = C_HEADS * SLOT
    q, k, v, yd, szc = pl.pallas_call(
        _odd_proj_kernel,
        grid=(n_tok // tm,),
        in_specs=[row_spec(D_MODEL), pl.BlockSpec((tm // LANES, 1, LANES), lambda t: (t, 0, 0)),
                  _full(w1.shape), _full((1, LANES)),
                  _full((1, C_Q_RANK)), _full(wq.shape),
                  _full((1, C_KV_RANK)), _full(wkv.shape),
                  _full((1, wide)), _full((1, D_WIDTH)), _full((1, D_WIDTH)),
                  _full((D_GROUPS, D_CHUNK, D_CHUNK)), _full((D_CHUNK, D_WIDTH)),
                  _full((1, D_WIDTH))],
        out_specs=[row_spec(wide), row_spec(wide), pl.BlockSpec((wide, tm), lambda t: (0, t)),
                   row_spec(D_WIDTH),
                   row_spec(C_WIDTH)],
        out_shape=[jax.ShapeDtypeStruct((n_tok, wide), BF16),
                   jax.ShapeDtypeStruct((n_tok, wide), BF16),
                   jax.ShapeDtypeStruct((wide, n_tok), BF16),
                   jax.ShapeDtypeStruct((n_tok, D_WIDTH), BF16),
                   jax.ShapeDtypeStruct((n_tok, C_WIDTH), F32)],
        compiler_params=_params(("parallel",)),
        name="odd_proj",
    )(x.reshape(n_tok, D_MODEL), positions.reshape(n_tok // LANES, 1, LANES), w1, inv_lane,
      q_norm.reshape(1, -1), wq, kv_norm.reshape(1, -1), wkv, vones,
      v_ln_g.reshape(1, -1), v_ln_b.reshape(1, -1), w_s.astype(BF16), bs_full,
      d_norm.reshape(1, -1))

    tq = 256
    tk = 512
    q = q.reshape(bsz, seq, wide)
    k = k.reshape(bsz, seq, wide)
    pair = lambda w: pl.BlockSpec((None, seq, w), lambda b, h: (b, 0, h))
    yc = pl.pallas_call(
        functools.partial(_mla_kernel, seq, tq, tk, 16),
        grid=(bsz, C_HEADS // 2),
        in_specs=[pair(2 * SLOT), pair(2 * SLOT),
                  pl.BlockSpec((2 * SLOT, seq), lambda b, h: (h, b))],
        out_specs=pair(2 * C_V),
        out_shape=jax.ShapeDtypeStruct((bsz, seq, C_WIDTH), F32),
        scratch_shapes=[pltpu.VMEM((seq, tq), F32), pltpu.VMEM((seq, tq), F32),
                        pltpu.VMEM((8, tq), F32), pltpu.VMEM((8, tq), F32)],
        compiler_params=_params(("parallel", "parallel")),
        name="mla_attn",
    )(q, k, v)

    to = 1024
    out_spec = lambda w: pl.BlockSpec((to, w), lambda t: (t, 0))
    out = pl.pallas_call(
        _odd_out_kernel,
        grid=(n_tok // to,),
        in_specs=[out_spec(C_WIDTH), out_spec(C_WIDTH), out_spec(D_WIDTH), out_spec(D_MODEL),
                  pl.BlockSpec((None, to, PLE_DIM), lambda t: (layer, t, 0)),
                  _once((1, C_WIDTH)), _once((D_MODEL, D_MODEL)), _once((1, D_MODEL)),
                  _once((1, D_MODEL)), _once((D_MODEL, D_MODEL)), _once((PLE_DIM, D_MODEL))],
        out_specs=out_spec(D_MODEL),
        out_shape=jax.ShapeDtypeStruct((n_tok, D_MODEL), F32),
        compiler_params=_params(("parallel",)),
        name="odd_out",
    )(yc.reshape(n_tok, C_WIDTH), szc, yd, x.reshape(n_tok, D_MODEL),
      p.reshape(DEPTH, n_tok, PLE_DIM), c_norm.reshape(1, -1), w_out.astype(BF16),
      ln_g.reshape(1, -1), ln_b.reshape(1, -1), ple_gate.astype(BF16), ple_proj.astype(BF16))
    return out.reshape(bsz, seq, D_MODEL)


def kernel(x, p, positions, ev_w_in, ev_conv_w, ev_sink, ev_a_norm, ev_b_norm, ev_w_out, od_w_in, od_q_norm, od_w_uq, od_kv_norm, od_w_ukv, od_v_ln_g, od_v_ln_b, od_w_s, od_b_s, od_c_norm, od_d_norm, od_w_out, post_ln_g, post_ln_b, ple_proj, ple_gate):
    for i in range(DEPTH):
        j = i // 2
        if i % 2 == 0:
            x = _even_layer(x, p, i, positions, ev_w_in[j], ev_conv_w[j], ev_sink[j],
                            ev_a_norm[j], ev_b_norm[j], ev_w_out[j],
                            post_ln_g[i], post_ln_b[i], ple_proj[i], ple_gate[i])
        else:
            x = _odd_layer(x, p, i, positions, od_w_in[j], od_q_norm[j], od_w_uq[j],
                           od_kv_norm[j], od_w_ukv[j], od_v_ln_g[j], od_v_ln_b[j],
                           od_w_s[j], od_b_s[j], od_c_norm[j], od_d_norm[j], od_w_out[j],
                           post_ln_g[i], post_ln_b[i], ple_proj[i], ple_gate[i])
    return x
```

```python
import functools
import math

import jax
import jax.numpy as jnp
from jax import lax
from jax.experimental import pallas as pl
from jax.experimental.pallas import tpu as pltpu

F32 = jnp.float32
BF16 = jnp.bfloat16

D_MODEL = 1024
DEPTH = 2
PLE_DIM = 256
DEEPNORM_ALPHA = (2.0 * DEPTH) ** 0.25
BLOCK = 128
EPS = 1e-6
A_HEAD_DIM = 64
A_HEADS = 8
A_KV_HEADS = 2
A_GROUP = A_HEADS // A_KV_HEADS
A_WIDTH = A_HEADS * A_HEAD_DIM
WINDOW = 128
B_WIDTH = 512
C_HEADS = 8
C_NOPE = 64
C_ROPE = 32
C_V = 64
C_WIDTH = C_HEADS * C_V
C_Q_RANK = 256
C_KV_RANK = 128
ROPE_THETA = 10000.0
D_WIDTH = 512
D_GROUPS = 4
D_GROUP_DIM = D_WIDTH // D_GROUPS
D_CHUNK = 128

LANES = 128
SLOT = LANES
VMEM_LIMIT = 56 * 1024 * 1024
PV_LAG = 3
V_ROWS = 80
OUT_SPLIT = 4

_EV_OFF = (0, 512, 640, 768, 1280, 1792, 2304, 3328)
_OD_OFF = (0, 384, 512, 1024, 1536, 2560)

A_SLOPES = tuple(2.0 ** (-8.0 * (h + 1) / A_HEADS) for h in range(A_HEADS))
LOG2E = math.log2(math.e)
A_SCALE_LOG2E = A_HEAD_DIM ** -0.5 * LOG2E
MLA_SCALE_LOG2E = (C_NOPE + C_ROPE) ** -0.5 * LOG2E


def _dot(a, b):
    return jnp.dot(a, b, preferred_element_type=F32)


def _dot_nt(a, b):
    return lax.dot_general(a, b, (((1,), (1,)), ((), ())), preferred_element_type=F32)


def _column(row):
    return jnp.transpose(jnp.broadcast_to(row, (LANES, LANES)))


def _rms(x, g):
    return x * lax.rsqrt(jnp.mean(x * x, axis=-1, keepdims=True) + EPS) * g


def _layer_norm(x, g, b):
    mu = jnp.mean(x, axis=-1, keepdims=True)
    xc = x - mu
    var = jnp.mean(xc * xc, axis=-1, keepdims=True)
    return xc * lax.rsqrt(var + EPS) * g + b


def _silu(z):
    return z * jax.nn.sigmoid(z)


def _gelu_tanh(x):
    c = math.sqrt(2.0 / math.pi)
    return 0.5 * x * (1.0 + jnp.tanh(c * (x + 0.044715 * (x * x * x))))


def _post_block(y_bf16_dot, x, p, g_ref, b_ref, gate_ref, proj_ref):
    h = _layer_norm(DEEPNORM_ALPHA * x + y_bf16_dot, g_ref[...], b_ref[...])
    gate = jax.nn.sigmoid(_dot(h.astype(BF16), gate_ref[...]))
    return h + gate * _dot(p.astype(BF16), proj_ref[...])


def _even_kernel(seq_len, tile,
                 xc_ref, xp_ref, p_ref, prc_ref, prp_ref, prn_ref,
                 win_ref, convw_ref, sink_ref, anorm_ref, bnorm_ref, wout_ref,
                 g_ref, b_ref, gate_ref, proj_ref,
                 o_ref,
                 kbuf, vbuf, q_a, q_b, bg_a, bg_b, sz_a, sz_b, cz_a, cz_b,
                 posbuf, ybuf, lg_a, lg_b, m_a, m_b):
    t = pl.program_id(0)

    @pl.when(t == 0)
    def _():
        for ref in (kbuf, vbuf, q_a, q_b, bg_a, bg_b, sz_a, sz_b, cz_a, cz_b):
            ref[...] = jnp.zeros(ref.shape, ref.dtype)

    shared = (seq_len, tile, xc_ref, xp_ref, p_ref, prc_ref, prp_ref, prn_ref,
              win_ref, convw_ref, sink_ref, anorm_ref, bnorm_ref, wout_ref,
              g_ref, b_ref, gate_ref, proj_ref, o_ref,
              kbuf, vbuf, posbuf, ybuf, lg_a, lg_b, m_a, m_b)

    @pl.when(t % 2 == 0)
    def _():
        _even_step(*shared, (q_a, bg_a, sz_a, cz_a), (q_b, bg_b, sz_b, cz_b))

    @pl.when(t % 2 == 1)
    def _():
        _even_step(*shared, (q_b, bg_b, sz_b, cz_b), (q_a, bg_a, sz_a, cz_a))


def _even_step(seq_len, tile, xc_ref, xp_ref, p_ref, prc_ref, prp_ref, prn_ref,
               win_ref, convw_ref, sink_ref, anorm_ref, bnorm_ref, wout_ref,
               g_ref, b_ref, gate_ref, proj_ref, o_ref,
               kbuf, vbuf, posbuf, ybuf, lg_a, lg_b, m_a, m_b, proj_bufs, mix_bufs):
    t = pl.program_id(0)
    tiles_per_seq = seq_len // tile
    sub = tile // BLOCK
    pq, pbg, psz, pcz = proj_bufs
    mq, mbg, msz, mcz = mix_bufs

    before_rows = pcz[tile - 8:tile, :]

    xb = xc_ref[...].astype(BF16)

    def seg(n, lo=0, hi=None):
        hi = _EV_OFF[n + 1] - _EV_OFF[n] if hi is None else hi
        return _dot(xb, win_ref[:, _EV_OFF[n] + lo:_EV_OFF[n] + hi])

    krow_p = pl.multiple_of(BLOCK + (t % tiles_per_seq) * tile, BLOCK)

    lane_t = lax.broadcasted_iota(jnp.int32, (tile, LANES), 1)
    low = lane_t < A_HEAD_DIM
    one = jnp.where(lane_t == A_HEAD_DIM, 1.0, 0.0)

    def slots(v, fill):
        return (jnp.where(low, v, fill),
                jnp.where(low, pltpu.roll(v, A_HEAD_DIM, axis=1), fill))

    def proj_kv():
        lo, hi = slots(seg(1), 0.0)
        kbuf[pl.ds(krow_p, tile), 0:SLOT] = lo.astype(BF16)
        kbuf[pl.ds(krow_p, tile), SLOT:] = hi.astype(BF16)
        lo, hi = slots(seg(2), one)
        vbuf[pl.ds(krow_p, tile), 0:SLOT] = lo.astype(BF16)
        vbuf[pl.ds(krow_p, tile), SLOT:] = hi.astype(BF16)

    piece_w = 2 * LANES

    def proj_q(part):
        q = seg(0, part * piece_w, (part + 1) * piece_w) * A_SCALE_LOG2E
        for n in range(piece_w // LANES):
            lo, hi = slots(q[:, n * LANES:(n + 1) * LANES], 0.0)
            h0 = 2 * (part * (piece_w // LANES) + n)
            pq[:, h0 * SLOT:(h0 + 1) * SLOT] = lo.astype(BF16)
            pq[:, (h0 + 1) * SLOT:(h0 + 2) * SLOT] = hi.astype(BF16)

    def proj_bg(part):
        lo, hi = part * piece_w, (part + 1) * piece_w
        pbg[:, lo:hi] = seg(3, lo, hi)

    def proj_cg(part):
        lo, hi = part * piece_w, (part + 1) * piece_w
        pcz[:, lo:hi] = seg(4, lo, hi)

    def proj_xin(part):
        lo, hi = part * piece_w, (part + 1) * piece_w
        pcz[:, lo:hi] = pcz[:, lo:hi] * seg(5, lo, hi)

    def proj_sz(part):
        lo, hi = part * piece_w, (part + 1) * piece_w
        psz[:, lo:hi] = _silu(seg(6, lo, hi))

    proj_kv()
    pieces = [functools.partial(fn, part)
              for fn, parts in ((proj_q, 2), (proj_bg, 2), (proj_cg, 2), (proj_xin, 2), (proj_sz, 4))
              for part in range(parts)]

    i = jnp.maximum(t - 1, 0) % tiles_per_seq
    krow_m = pl.multiple_of(i * tile, BLOCK)

    posbuf[:, 0:BLOCK] = prp_ref[...]
    posbuf[:, BLOCK:BLOCK + tile] = prc_ref[...]
    posbuf[:, BLOCK + tile:] = prn_ref[...]

    q_idx = lax.broadcasted_iota(jnp.int32, (BLOCK, 3 * BLOCK), 0)
    w_idx = lax.broadcasted_iota(jnp.int32, (BLOCK, 3 * BLOCK), 1)
    in_band = jnp.abs(w_idx - BLOCK - q_idx) <= WINDOW
    lane = lax.broadcasted_iota(jnp.int32, (BLOCK, LANES), 1)
    dists = {}

    def masked_dist(j):
        if j not in dists:
            k_glob = (i * sub + j) * BLOCK + w_idx - BLOCK
            valid = in_band & (k_glob >= 0) & (k_glob < seq_len)
            pos_q = _column(posbuf[:, (j + 1) * BLOCK:(j + 2) * BLOCK])
            d = jnp.concatenate([jnp.abs(pos_q - posbuf[:, (j + n) * BLOCK:(j + n + 1) * BLOCK])
                                 for n in range(3)], axis=1).astype(F32)
            dists[j] = jnp.where(valid, d, jnp.inf)
        return dists[j]

    def scores(j, g, lg_ref, m_ref):
        qrow = slice(j * BLOCK, (j + 1) * BLOCK)
        qg = jnp.concatenate([mq[qrow, (A_GROUP * g + hh) * SLOT:(A_GROUP * g + hh + 1) * SLOT]
                              for hh in range(A_GROUP)], axis=0)
        s = _dot_nt(qg, kbuf[pl.ds(krow_m + j * BLOCK, 3 * BLOCK), g * SLOT:(g + 1) * SLOT])
        d = masked_dist(j)
        for hh in range(A_GROUP):
            h = A_GROUP * g + hh
            r = slice(hh * BLOCK, (hh + 1) * BLOCK)
            lg = s[r, :] - (A_SLOPES[h] * LOG2E) * d
            lg_ref[r, :] = lg
            mx = jnp.maximum(jnp.max(lg, axis=-1, keepdims=True), sink_ref[h])
            m_ref[r, :] = jnp.broadcast_to(mx, (BLOCK, LANES))

    def attend(j, g, lg_ref, m_ref):
        rows = slice(j * BLOCK, (j + 1) * BLOCK)
        m = m_ref[...]
        e = jnp.concatenate([jnp.exp2(lg_ref[:, n * LANES:(n + 1) * LANES] - m).astype(BF16)
                             for n in range(3)], axis=1)
        pv = _dot(e, vbuf[pl.ds(krow_m + j * BLOCK, 3 * BLOCK), g * SLOT:(g + 1) * SLOT])
        outs = []
        for hh in range(A_GROUP):
            r = slice(hh * BLOCK, (hh + 1) * BLOCK)
            denom = (jnp.broadcast_to(pv[r, A_HEAD_DIM:A_HEAD_DIM + 1], (BLOCK, LANES))
                     + jnp.exp2(sink_ref[A_GROUP * g + hh] - m[r, :]))
            outs.append(pv[r, :] / denom)
        for pr in range(A_GROUP // 2):
            c0 = (A_GROUP // 2 * g + pr) * LANES
            ybuf[rows, c0:c0 + LANES] = jnp.where(
                lane < A_HEAD_DIM, outs[2 * pr], pltpu.roll(outs[2 * pr + 1], A_HEAD_DIM, axis=1))

    chains = [(j, g) for j in range(sub) for g in range(A_KV_HEADS)]
    bufs = [(lg_a, m_a), (lg_b, m_b)]
    scores(*chains[0], *bufs[0])
    late = [pieces.pop() for _ in range(4)][::-1]
    for c, (j, g) in enumerate(chains):
        if c + 1 < len(chains):
            scores(*chains[c + 1], *bufs[(c + 1) % 2])
        attend(j, g, *bufs[c % 2])
        pieces.pop(0)()
    assert not pieces

    late[0]()
    cz = mcz[...]
    row = lax.broadcasted_iota(jnp.int32, cz.shape, 0)
    prev_row = jnp.where(i > 0, before_rows[7:8, :], 0.0)
    next_row = jnp.where(i < tiles_per_seq - 1, pcz[0:1, :], 0.0)
    z_prev = jnp.where(row == 0, prev_row, pltpu.roll(cz, 1, axis=0))
    z_next = jnp.where(row == tile - 1, next_row, pltpu.roll(cz, tile - 1, axis=0))
    conv = (convw_ref[0:1, :] * z_prev + convw_ref[1:2, :] * cz
            + convw_ref[2:3, :] * z_next)
    yb = mbg[...] * conv

    late[1]()
    groups = [slice(r, r + tile // 2) for r in range(0, tile, tile // 2)]
    ya_n = [(_rms(ybuf[r, :], anorm_ref[...]) * msz[r, 0:A_WIDTH]).astype(BF16) for r in groups]
    yb_n = [(_rms(yb[r, :], bnorm_ref[...]) * msz[r, A_WIDTH:]).astype(BF16) for r in groups]
    emb = [_dot(p_ref[r, :].astype(BF16), proj_ref[...]) for r in groups]
    mix = [_dot(a, wout_ref[0:A_WIDTH, :]) + _dot(b, wout_ref[A_WIDTH:, :])
           for a, b in zip(ya_n, yb_n)]
    late[2]()
    h = [_layer_norm(DEEPNORM_ALPHA * xp_ref[r, :] + m, g_ref[...], b_ref[...])
         for m, r in zip(mix, groups)]
    late[3]()
    gate = [jax.nn.sigmoid(_dot(hh.astype(BF16), gate_ref[...])) for hh in h]
    for r, hh, gt, e in zip(groups, h, gate, emb):
        o_ref[r, :] = hh + gt * e


def _odd_proj_kernel(x_ref, pos_ref, w_ref, inv_ref, qn_ref, wq_ref,
                     kvn_ref, wkv_ref, vones_ref,
                     lng_ref, lnb_ref, ws_ref, bs_ref, dnorm_ref,
                     q_ref, k_ref, v_ref, yd_ref, szc_ref):
    tile = x_ref.shape[0]
    wide = C_HEADS * SLOT
    xb = x_ref[...].astype(BF16)

    def seg(i):
        return _dot(xb, w_ref[:, _OD_OFF[i]:_OD_OFF[i + 1]])

    packs = LANES // C_ROPE
    rb = tile // packs
    lane = lax.broadcasted_iota(jnp.int32, (rb, LANES), 1)
    packed = _column(pos_ref[0]).astype(F32)
    for u in range(1, packs):
        packed = jnp.where(lane >= u * C_ROPE, _column(pos_ref[u]).astype(F32), packed)
    ang = packed * inv_ref[...]
    cos_p = jnp.cos(ang)
    sin_p = jnp.sin(ang)
    rope_lane = (lane >= C_NOPE) & (lane < C_NOPE + C_ROPE)
    cos_blocks, sin_blocks = [], []
    for u in range(packs):
        shift = (C_NOPE - u * C_ROPE) % LANES
        cu = pltpu.roll(cos_p, shift, axis=1) if shift else cos_p
        su = pltpu.roll(sin_p, shift, axis=1) if shift else sin_p
        cos_blocks.append(jnp.where(rope_lane, cu, 1.0))
        sin_blocks.append(jnp.where(rope_lane, su, 0.0))
    cos_t = jnp.concatenate(cos_blocks, axis=0)
    sin_t = jnp.concatenate(sin_blocks, axis=0)
    first_half = lax.broadcasted_iota(jnp.int32, (tile, LANES), 1) < C_NOPE + C_ROPE // 2
    sin_s = jnp.where(first_half, -sin_t, sin_t)

    def rotary(t):
        swapped = jnp.where(first_half, pltpu.roll(t, LANES - C_ROPE // 2, axis=1),
                            pltpu.roll(t, C_ROPE // 2, axis=1))
        return t * cos_t + swapped * sin_s

    a = seg(0)
    kr = seg(1)
    cqn = _rms(a[:, 0:C_Q_RANK], qn_ref[...]).astype(BF16)
    ckvn = _rms(a[:, C_Q_RANK:], kvn_ref[...]).astype(BF16)
    du = seg(2)
    qa = _dot(cqn, wq_ref[...])
    gu = _gelu_tanh(du)
    dv = seg(3)
    for h in range(C_HEADS):
        sl = slice(h * SLOT, (h + 1) * SLOT)
        q_ref[:, sl] = (rotary(qa[:, sl]) * MLA_SCALE_LOG2E).astype(BF16)

    kr_rot = rotary(kr).astype(BF16)
    kv = _dot(jnp.concatenate([ckvn, kr_rot], axis=1), wkv_ref[...])
    vn = _layer_norm(_gelu_tanh(dv), lng_ref[...], lnb_ref[...]).astype(BF16)
    z = seg(4)
    k_ref[...] = kv[:, 0:wide].astype(BF16)
    v_ref[...] = jnp.transpose(kv[:, wide:] + vones_ref[...]).astype(BF16)

    n_ch = tile // D_CHUNK
    mixed = []
    for g in range(D_GROUPS):
        cols = slice(g * D_GROUP_DIM, (g + 1) * D_GROUP_DIM)
        rhs = jnp.concatenate([vn[c * D_CHUNK:(c + 1) * D_CHUNK, cols] for c in range(n_ch)], axis=1)
        mixed.append(_dot(ws_ref[g], rhs))
    sz = _silu(z)
    szc_ref[...] = sz[:, 0:C_WIDTH]
    chunks = [jnp.concatenate([mixed[g][:, c * D_GROUP_DIM:(c + 1) * D_GROUP_DIM]
                               for g in range(D_GROUPS)], axis=1) + bs_ref[...]
              for c in range(n_ch)]
    yd = gu * jnp.concatenate(chunks, axis=0)
    yd_ref[...] = (_rms(yd, dnorm_ref[...]) * sz[:, C_WIDTH:]).astype(BF16)


def _mla_kernel(seq_len, tq, tk, unroll, q_ref, k_ref, vt_ref, o_ref, s_a, s_b, m_a, m_b):
    n_qt = seq_len // tq
    n_c = seq_len // tk
    dyn_zero = pl.program_id(0) // 1024

    def scores(qt, hh, s_ref, m_ref):
        sl = slice(hh * SLOT, (hh + 1) * SLOT)
        q = q_ref[pl.ds(pl.multiple_of(qt * tq, tq), tq), sl]
        part = None
        for c in range(n_c):
            st = _dot_nt(k_ref[c * tk:(c + 1) * tk, sl], q)
            s_ref[c * tk:(c + 1) * tk, :] = st
            cm = jnp.max(st.reshape(tk // 8, 8, tq), axis=0)
            part = cm if part is None else jnp.maximum(part, cm)
        m_ref[...] = jnp.broadcast_to(jnp.max(part, axis=0, keepdims=True), m_ref.shape)

    def attend(hh, s_ref, m_ref):
        sl = slice(hh * SLOT, hh * SLOT + V_ROWS)
        m = m_ref[0:1, :]
        acc = None
        done = []
        for c in range(n_c):
            if c >= PV_LAG:
                s_ref[seq_len:seq_len + 8, :] = done[c - PV_LAG][0:8, :]
            rows = pl.ds(pl.multiple_of(c * tk + dyn_zero, tk), tk)
            pt = jnp.exp2(s_ref[rows, :] - m).astype(BF16)
            d = _dot(vt_ref[sl, c * tk:(c + 1) * tk], pt)
            done.append(d)
            acc = d if acc is None else acc + d
        return acc[0:C_V, :] / acc[C_V:C_V + 1, :]

    scores(0, 0, s_a, m_a)

    def one_tile(qt):
        scores(qt, 1, s_b, m_b)
        out0 = attend(0, s_a, m_a)
        out1 = attend(1, s_b, m_b)
        scores(jnp.minimum(qt + 1, n_qt - 1), 0, s_a, m_a)
        both = jnp.concatenate([out0, out1], axis=0)
        o_ref[pl.ds(pl.multiple_of(qt * tq, tq), tq), :] = jnp.transpose(both)

    def body(it, carry):
        for u in range(unroll):
            one_tile(it * unroll + u)
        return carry

    lax.fori_loop(0, n_qt // unroll, body, 0)


def _odd_out_kernel(yc_ref, szc_ref, yd_ref, x_ref, p_ref, cnorm_ref, wout_ref,
                    g_ref, b_ref, gate_ref, proj_ref, o_ref):
    tile = x_ref.shape[0]
    groups = [slice(r, r + tile // OUT_SPLIT) for r in range(0, tile, tile // OUT_SPLIT)]
    yc_n = [(_rms(yc_ref[r, :], cnorm_ref[...]) * szc_ref[r, :]).astype(BF16) for r in groups]
    emb = [_dot(p_ref[r, :].astype(BF16), proj_ref[...]) for r in groups]
    mix = [_dot(y, wout_ref[0:C_WIDTH, :]) + _dot(yd_ref[r, :], wout_ref[C_WIDTH:, :])
           for y, r in zip(yc_n, groups)]
    h = [_layer_norm(DEEPNORM_ALPHA * x_ref[r, :] + m, g_ref[...], b_ref[...])
         for m, r in zip(mix, groups)]
    gate = [jax.nn.sigmoid(_dot(hh.astype(BF16), gate_ref[...])) for hh in h]
    for r, hh, gt, e in zip(groups, h, gate, emb):
        o_ref[r, :] = hh + gt * e


def _full(shape):
    return pl.BlockSpec(shape, lambda *_: (0,) * len(shape))


def _once(shape):
    return pl.BlockSpec(shape, lambda *_: (0,) * len(shape), pipeline_mode=pl.Buffered(1))


def _params(sem):
    return pltpu.CompilerParams(dimension_semantics=sem, vmem_limit_bytes=VMEM_LIMIT)


def _even_layer(x, p, layer, positions, w_in, conv_w, sink, a_norm, b_norm, w_out,
                ln_g, ln_b, ple_proj, ple_gate):
    bsz, seq, _ = x.shape
    n_tok = bsz * seq
    tile = 512
    n_tiles = n_tok // tile
    tiles_per_seq = seq // tile
    sub = tile // BLOCK
    nb = seq // BLOCK
    q_w = A_HEADS * SLOT
    kv_w = A_KV_HEADS * SLOT
    x2 = x.reshape(n_tok, D_MODEL)
    pos_row = positions.reshape(bsz, 1, seq)

    cur = lambda t: jnp.minimum(t, n_tiles - 1)
    prv = lambda t: jnp.maximum(t - 1, 0)
    seq_of = lambda t: prv(t) // tiles_per_seq
    blk_of = lambda t: (prv(t) % tiles_per_seq) * sub
    in_specs = [
        pl.BlockSpec((tile, D_MODEL), lambda t: (cur(t), 0)),
        pl.BlockSpec((tile, D_MODEL), lambda t: (prv(t), 0)),
        pl.BlockSpec((None, tile, PLE_DIM), lambda t: (layer, prv(t), 0)),
        pl.BlockSpec((None, 1, tile), lambda t: (seq_of(t), 0, prv(t) % tiles_per_seq)),
        pl.BlockSpec((None, 1, BLOCK), lambda t: (seq_of(t), 0, jnp.maximum(blk_of(t) - 1, 0))),
        pl.BlockSpec((None, 1, BLOCK), lambda t: (seq_of(t), 0, jnp.minimum(blk_of(t) + sub, nb - 1))),
        _once((D_MODEL, _EV_OFF[-1])),
        _once((3, B_WIDTH)),
        pl.BlockSpec(memory_space=pltpu.SMEM),
        _once((1, A_WIDTH)), _once((1, B_WIDTH)), _once((D_MODEL, D_MODEL)),
        _once((1, D_MODEL)), _once((1, D_MODEL)), _once((D_MODEL, D_MODEL)),
        _once((PLE_DIM, D_MODEL)),
    ]
    out = pl.pallas_call(
        functools.partial(_even_kernel, seq, tile),
        grid=(n_tiles + 1,),
        in_specs=in_specs,
        out_specs=pl.BlockSpec((tile, D_MODEL), lambda t: (prv(t), 0)),
        out_shape=jax.ShapeDtypeStruct((n_tok, D_MODEL), F32),
        scratch_shapes=[pltpu.VMEM((seq + 2 * BLOCK, kv_w), BF16),
                        pltpu.VMEM((seq + 2 * BLOCK, kv_w), BF16),
                        pltpu.VMEM((tile, q_w), BF16), pltpu.VMEM((tile, q_w), BF16),
                        pltpu.VMEM((tile, B_WIDTH), F32), pltpu.VMEM((tile, B_WIDTH), F32),
                        pltpu.VMEM((tile, A_WIDTH + B_WIDTH), F32),
                        pltpu.VMEM((tile, A_WIDTH + B_WIDTH), F32),
                        pltpu.VMEM((tile, B_WIDTH), F32), pltpu.VMEM((tile, B_WIDTH), F32),
                        pltpu.VMEM((1, tile + 2 * BLOCK), jnp.int32),
                        pltpu.VMEM((tile, A_WIDTH), F32),
                        pltpu.VMEM((A_GROUP * BLOCK, 3 * BLOCK), F32),
                        pltpu.VMEM((A_GROUP * BLOCK, 3 * BLOCK), F32),
                        pltpu.VMEM((A_GROUP * BLOCK, LANES), F32),
                        pltpu.VMEM((A_GROUP * BLOCK, LANES), F32)],
        compiler_params=_params(("arbitrary",)),
        name="even_layer",
    )(x2, x2, p.reshape(DEPTH, n_tok, PLE_DIM), pos_row, pos_row, pos_row,
      w_in.astype(BF16), conv_w, sink * LOG2E, a_norm.reshape(1, -1), b_norm.reshape(1, -1),
      w_out.astype(BF16), ln_g.reshape(1, -1), ln_b.reshape(1, -1),
      ple_gate.astype(BF16), ple_proj.astype(BF16))
    return out.reshape(bsz, seq, D_MODEL)


def _odd_weights(w_in, w_uq, w_ukv, b_s):
    d = w_in.shape[0]
    o_kr = C_Q_RANK + C_KV_RANK
    o_du = o_kr + C_ROPE
    w_kr = w_in[:, o_kr:o_du]
    kr_tile = jnp.pad(w_kr, ((0, 0), (C_NOPE, LANES - C_NOPE - C_ROPE)))
    w1 = jnp.concatenate([w_in[:, :o_kr], kr_tile, w_in[:, o_du:]], axis=1).astype(BF16)

    uq = w_uq.reshape(C_Q_RANK, C_HEADS, C_NOPE + C_ROPE)
    pad = SLOT - C_NOPE - C_ROPE
    wq = jnp.pad(uq, ((0, 0), (0, 0), (0, pad)))
    ukv = w_ukv.reshape(C_KV_RANK, C_HEADS, C_NOPE + C_V)
    wkn = jnp.pad(ukv[..., :C_NOPE], ((0, 0), (0, 0), (0, SLOT - C_NOPE)))
    wv = jnp.pad(ukv[..., C_NOPE:], ((0, 0), (0, 0), (0, SLOT - C_V)))
    flat = lambda w: w.reshape(w.shape[0], C_HEADS * SLOT).astype(BF16)

    lane = jnp.arange(LANES)
    rope_lane = (lane >= C_NOPE) & (lane < C_NOPE + C_ROPE)
    wke = jnp.tile(jnp.where(rope_lane[:, None] & (lane[:, None] == lane[None, :]), 1.0, 0.0),
                   (1, C_HEADS)).astype(BF16)
    vones = jnp.tile(jnp.where(lane == C_V, 1.0, 0.0), C_HEADS).reshape(1, -1).astype(F32)
    half = C_ROPE // 2
    inv = ROPE_THETA ** (-jnp.arange(half, dtype=F32) / half)
    inv_lane = jnp.tile(inv, LANES // half).reshape(1, LANES)
    bs_full = jnp.repeat(b_s.T, D_GROUP_DIM, axis=1)
    wkv = jnp.concatenate([jnp.concatenate([flat(wkn), flat(wv)], axis=1),
                           jnp.concatenate([wke, jnp.zeros_like(wke)], axis=1)], axis=0)
    return w1, flat(wq), wkv, vones, inv_lane, bs_full


def _odd_layer(x, p, layer, positions, w_in, q_norm, w_uq, kv_norm, w_ukv, v_ln_g, v_ln_b,
               w_s, b_s, c_norm, d_norm, w_out, ln_g, ln_b, ple_proj, ple_gate):
    bsz, seq, _ = x.shape
    n_tok = bsz * seq
    w1, wq, wkv, vones, inv_lane, bs_full = _odd_weights(w_in, w_uq, w_ukv, b_s)
    tm = 512
    row_spec = lambda w: pl.BlockSpec((tm, w), lambda t: (t, 0))
    wide = C_HEADS * SLOT
    q, k, v, yd, szc = pl.pallas_call(
        _odd_proj_kernel,
        grid=(n_tok // tm,),
        in_specs=[row_spec(D_MODEL), pl.BlockSpec((tm // LANES, 1, LANES), lambda t: (t, 0, 0)),
                  _full(w1.shape), _full((1, LANES)),
                  _full((1, C_Q_RANK)), _full(wq.shape),
                  _full((1, C_KV_RANK)), _full(wkv.shape),
                  _full((1, wide)), _full((1, D_WIDTH)), _full((1, D_WIDTH)),
                  _full((D_GROUPS, D_CHUNK, D_CHUNK)), _full((D_CHUNK, D_WIDTH)),
                  _full((1, D_WIDTH))],
        out_specs=[row_spec(wide), row_spec(wide), pl.BlockSpec((wide, tm), lambda t: (0, t)),
                   row_spec(D_WIDTH),
                   row_spec(C_WIDTH)],
        out_shape=[jax.ShapeDtypeStruct((n_tok, wide), BF16),
                   jax.ShapeDtypeStruct((n_tok, wide), BF16),
                   jax.ShapeDtypeStruct((wide, n_tok), BF16),
                   jax.ShapeDtypeStruct((n_tok, D_WIDTH), BF16),
                   jax.ShapeDtypeStruct((n_tok, C_WIDTH), F32)],
        compiler_params=_params(("parallel",)),
        name="odd_proj",
    )(x.reshape(n_tok, D_MODEL), positions.reshape(n_tok // LANES, 1, LANES), w1, inv_lane,
      q_norm.reshape(1, -1), wq, kv_norm.reshape(1, -1), wkv, vones,
      v_ln_g.reshape(1, -1), v_ln_b.reshape(1, -1), w_s.astype(BF16), bs_full,
      d_norm.reshape(1, -1))

    tq = 256
    tk = 512
    q = q.reshape(bsz, seq, wide)
    k = k.reshape(bsz, seq, wide)
    pair = lambda w: pl.BlockSpec((None, seq, w), lambda b, h: (b, 0, h))
    yc = pl.pallas_call(
        functools.partial(_mla_kernel, seq, tq, tk, 8),
        grid=(bsz, C_HEADS // 2),
        in_specs=[pair(2 * SLOT), pair(2 * SLOT),
                  pl.BlockSpec((2 * SLOT, seq), lambda b, h: (h, b))],
        out_specs=pair(2 * C_V),
        out_shape=jax.ShapeDtypeStruct((bsz, seq, C_WIDTH), F32),
        scratch_shapes=[pltpu.VMEM((seq + 8, tq), F32), pltpu.VMEM((seq + 8, tq), F32),
                        pltpu.VMEM((8, tq), F32), pltpu.VMEM((8, tq), F32)],
        compiler_params=_params(("parallel", "parallel")),
        name="mla_attn",
    )(q, k, v)

    to = 1024
    out_spec = lambda w: pl.BlockSpec((to, w), lambda t: (t, 0))
    out = pl.pallas_call(
        _odd_out_kernel,
        grid=(n_tok // to,),
        in_specs=[out_spec(C_WIDTH), out_spec(C_WIDTH), out_spec(D_WIDTH), out_spec(D_MODEL),
                  pl.BlockSpec((None, to, PLE_DIM), lambda t: (layer, t, 0)),
                  _once((1, C_WIDTH)), _once((D_MODEL, D_MODEL)), _once((1, D_MODEL)),
                  _once((1, D_MODEL)), _once((D_MODEL, D_MODEL)), _once((PLE_DIM, D_MODEL))],
        out_specs=out_spec(D_MODEL),
        out_shape=jax.ShapeDtypeStruct((n_tok, D_MODEL), F32),
        compiler_params=_params(("parallel",)),
        name="odd_out",
    )(yc.reshape(n_tok, C_WIDTH), szc, yd, x.reshape(n_tok, D_MODEL),
      p.reshape(DEPTH, n_tok, PLE_DIM), c_norm.reshape(1, -1), w_out.astype(BF16),
      ln_g.reshape(1, -1), ln_b.reshape(1, -1), ple_gate.astype(BF16), ple_proj.astype(BF16))
    return out.reshape(bsz, seq, D_MODEL)


def kernel(x, p, positions, ev_w_in, ev_conv_w, ev_sink, ev_a_norm, ev_b_norm, ev_w_out, od_w_in, od_q_norm, od_w_uq, od_kv_norm, od_w_ukv, od_v_ln_g, od_v_ln_b, od_w_s, od_b_s, od_c_norm, od_d_norm, od_w_out, post_ln_g, post_ln_b, ple_proj, ple_gate):
    for i in range(DEPTH):
        j = i // 2
        if i % 2 == 0:
            x = _even_layer(x, p, i, positions, ev_w_in[j], ev_conv_w[j], ev_sink[j],
                            ev_a_norm[j], ev_b_norm[j], ev_w_out[j],
                            post_ln_g[i], post_ln_b[i], ple_proj[i], ple_gate[i])
        else:
            x = _odd_layer(x, p, i, positions, od_w_in[j], od_q_norm[j], od_w_uq[j],
                           od_kv_norm[j], od_w_ukv[j], od_v_ln_g[j], od_v_ln_b[j],
                           od_w_s[j], od_b_s[j], od_c_norm[j], od_d_norm[j], od_w_out[j],
                           post_ln_g[i], post_ln_b[i], ple_proj[i], ple_gate[i])
    return x
```
